```python
import jax, jax.numpy as jnp
from jax import lax
import numpy as np

D_MODEL = 1024
BATCH = 4
SEQ = 8192
DEPTH = 4

D_MIX = D_MODEL
MLA_V = 128
MLA_NOPE = 128
MLA_ROPE = 64
MLA_WIDTH = D_MIX // 2
MLA_HEADS = MLA_WIDTH // MLA_V
MLA_Q_RANK = 384
MLA_KV_RANK = 256
GDN_DK = 128
GDN_DV = 128
GDN_WIDTH = D_MIX - MLA_WIDTH
GDN_HEADS = GDN_WIDTH // GDN_DV
GDN_CONV = 4
GDN_CHUNK = 64
Q_BLOCK = 128
ROPE_THETA = 10000.0
NORM_EPS = 1e-6
GDN_QKV = 2 * GDN_HEADS * GDN_DK + GDN_HEADS * GDN_DV

IN_SIZES = (MLA_Q_RANK, MLA_KV_RANK, MLA_ROPE, MLA_WIDTH, GDN_QKV, GDN_HEADS, GDN_HEADS, GDN_WIDTH)
IN_COLS = MLA_Q_RANK + MLA_KV_RANK + MLA_ROPE + MLA_WIDTH + GDN_QKV + GDN_HEADS + GDN_HEADS + GDN_WIDTH
IN_SPLIT_POINTS = tuple(int(s) for s in np.cumsum(IN_SIZES)[:-1])

kernel_name = "hybrid_mla_gdn_parallel_heads"


def rmsnorm(x, w):
    xf = x.astype(jnp.float32)
    y = xf * lax.rsqrt(jnp.mean(xf * xf, axis=-1, keepdims=True) + NORM_EPS)
    return (y * w.astype(jnp.float32)).astype(x.dtype)


def l2norm(x):
    return x * lax.rsqrt(jnp.sum(x * x, axis=-1, keepdims=True) + NORM_EPS)


def apply_rope(x, pos):
    r = x.shape[-1]
    half = r // 2
    inv_freq = jnp.power(ROPE_THETA, -jnp.arange(half, dtype=jnp.float32) * 2.0 / r)
    ang = pos.astype(jnp.float32)[:, :, None, None] * inv_freq
    cos, sin = jnp.cos(ang), jnp.sin(ang)
    xf = x.astype(jnp.float32)
    x1, x2 = xf[..., :half], xf[..., half:]
    return jnp.concatenate([x1 * cos - x2 * sin, x2 * cos + x1 * sin], axis=-1).astype(x.dtype)


def causal_attention(q, k, v, scale):
    b, s, h, dq = q.shape
    dv = v.shape[-1]
    nb = s // Q_BLOCK
    qb = q.reshape(b, nb, Q_BLOCK, h, dq).transpose(1, 0, 2, 3, 4)
    key_pos = jnp.arange(s)

    def one_block(args):
        i, q_blk = args
        sc = jnp.einsum('bqhd,bkhd->bhqk', q_blk, k, preferred_element_type=jnp.float32) * scale
        q_pos = i * Q_BLOCK + jnp.arange(Q_BLOCK)
        sc = jnp.where(key_pos[None, :] <= q_pos[:, None], sc, -1e30)
        p = jax.nn.softmax(sc, axis=-1)
        return jnp.einsum('bhqk,bkhd->bqhd', p.astype(v.dtype), v)

    o = lax.map(one_block, (jnp.arange(nb), qb))
    return o.transpose(1, 0, 2, 3, 4).reshape(b, s, h, dv)


def mla_branch(q_lat, kv_lat, k_pe_raw, z, pos, q_norm_w, q_up, kv_norm_w, kv_up):
    b, s, _ = q_lat.shape
    q = (rmsnorm(q_lat, q_norm_w) @ q_up).reshape(b, s, MLA_HEADS, MLA_NOPE + MLA_ROPE)
    q = jnp.concatenate([q[..., :MLA_NOPE], apply_rope(q[..., MLA_NOPE:], pos)], axis=-1)
    kv = (rmsnorm(kv_lat, kv_norm_w) @ kv_up).reshape(b, s, MLA_HEADS, MLA_NOPE + MLA_V)
    k_nope, v = kv[..., :MLA_NOPE], kv[..., MLA_NOPE:]
    k_pe = apply_rope(k_pe_raw[:, :, None, :], pos)
    k = jnp.concatenate([k_nope, jnp.broadcast_to(k_pe, (b, s, MLA_HEADS, MLA_ROPE))], axis=-1)
    o = causal_attention(q, k, v, (MLA_NOPE + MLA_ROPE) ** -0.5)
    return o.reshape(b, s, MLA_WIDTH) * jax.nn.silu(z)


def causal_conv(x, w):
    kw = w.shape[0]
    return lax.conv_general_dilated(x, w[:, None, :], window_strides=(1,), padding=[(kw - 1, 0)],
                                    dimension_numbers=('NWC', 'WIO', 'NWC'),
                                    feature_group_count=x.shape[-1])


def chunk_gated_delta_rule(q, k, v, g, beta):
    b, s, h, dk = q.shape
    dv = v.shape[-1]
    c = GDN_CHUNK
    n = s // c

    def chunks(t):
        t = t.reshape((b, n, c, h) + t.shape[3:])
        return jnp.moveaxis(t, 3, 1)

    q, k, v, g, beta = chunks(q), chunks(k), chunks(v), chunks(g), chunks(beta)
    g = jnp.cumsum(g, axis=-1)
    incl = jnp.tril(jnp.ones((c, c), dtype=bool))
    strict = jnp.tril(jnp.ones((c, c), dtype=bool), -1)
    decay = jnp.exp(jnp.where(incl, g[..., :, None] - g[..., None, :], -jnp.inf))
    kb = k * beta[..., None]
    kk = jnp.einsum('bhnid,bhnjd->bhnij', kb, k)
    a_mat = jnp.eye(c, dtype=jnp.float32) + jnp.where(strict, kk * decay, 0.0)
    rhs = jnp.concatenate([v * beta[..., None], kb * jnp.exp(g)[..., None]], axis=-1)
    sol = lax.linalg.triangular_solve(a_mat, rhs, left_side=True, lower=True, unit_diagonal=True)
    u, w = sol[..., :dv], sol[..., dv:]
    qk = jnp.einsum('bhnid,bhnjd->bhnij', q, k) * decay
    q_dec = q * jnp.exp(g)[..., None]
    k_dec = k * jnp.exp(g[..., -1:] - g)[..., None]
    g_last = jnp.exp(g[..., -1])

    def step(state, inp):
        qd, kd, u_c, w_c, qk_c, gl = inp
        v_new = u_c - jnp.einsum('bhcd,bhde->bhce', w_c, state)
        o = jnp.einsum('bhcd,bhde->bhce', qd, state) + jnp.einsum('bhij,bhje->bhie', qk_c, v_new)
        state = state * gl[..., None, None] + jnp.einsum('bhcd,bhce->bhde', kd, v_new)
        return state, o

    xs = tuple(jnp.moveaxis(t, 2, 0) for t in (q_dec, k_dec, u, w, qk, g_last))
    _, o = lax.scan(step, jnp.zeros((b, h, dk, dv), jnp.float32), xs)
    return o.transpose(1, 0, 3, 2, 4).reshape(b, s, h, dv)


def gdn_branch(qkv, a, bt, z, conv_w, a_log, dt_bias, o_norm_w):
    bsz, s, _ = qkv.shape
    qkv = jax.nn.silu(causal_conv(qkv, conv_w)).astype(jnp.float32)
    nqk = GDN_HEADS * GDN_DK
    q = l2norm(qkv[..., :nqk].reshape(bsz, s, GDN_HEADS, GDN_DK)) * (GDN_DK ** -0.5)
    k = l2norm(qkv[..., nqk:2 * nqk].reshape(bsz, s, GDN_HEADS, GDN_DK))
    v = qkv[..., 2 * nqk:].reshape(bsz, s, GDN_HEADS, GDN_DV)
    beta = jax.nn.sigmoid(bt.astype(jnp.float32))
    g = -jnp.exp(a_log.astype(jnp.float32)) * jax.nn.softplus(a.astype(jnp.float32) + dt_bias.astype(jnp.float32))
    o = chunk_gated_delta_rule(q, k, v, g, beta)
    o = rmsnorm(o, o_norm_w).astype(z.dtype).reshape(bsz, s, GDN_WIDTH)
    return o * jax.nn.silu(z)


def setup_inputs(seed: int = 0) -> dict:
    key = jax.random.key(seed)
    ks = jax.random.split(key, 20)
    f32 = jnp.float32
    nrm = lambda k, shape, sc: jax.random.normal(k, shape, f32) * sc
    dt = jnp.exp(jax.random.uniform(ks[14], (DEPTH, GDN_HEADS), f32, np.log(1e-3), np.log(1e-1)))
    offsets = jax.random.randint(ks[2], (BATCH, 1), 0, 4096, dtype=jnp.int32)
    return {
        "x": nrm(ks[0], (BATCH, SEQ, D_MODEL), 1.0),
        "c": nrm(ks[1], (BATCH, D_MODEL), 1.0),
        "positions": offsets + jnp.arange(SEQ, dtype=jnp.int32)[None, :],
        "w_mod": nrm(ks[3], (DEPTH, D_MODEL, 3 * D_MODEL), 0.5 * D_MODEL ** -0.5),
        "b_mod": nrm(ks[4], (DEPTH, 3 * D_MODEL), 0.02),
        "pre_norm_w": 1.0 + nrm(ks[5], (DEPTH, D_MODEL), 0.05),
        "post_norm_w": 1.0 + nrm(ks[6], (DEPTH, D_MODEL), 0.05),
        "w_in": nrm(ks[7], (DEPTH, D_MODEL, IN_COLS), D_MODEL ** -0.5),
        "mla_q_norm_w": 1.0 + nrm(ks[8], (DEPTH, MLA_Q_RANK), 0.05),
        "mla_q_up": nrm(ks[9], (DEPTH, MLA_Q_RANK, MLA_HEADS * (MLA_NOPE + MLA_ROPE)), MLA_Q_RANK ** -0.5),
        "mla_kv_norm_w": 1.0 + nrm(ks[10], (DEPTH, MLA_KV_RANK), 0.05),
        "mla_kv_up": nrm(ks[11], (DEPTH, MLA_KV_RANK, MLA_HEADS * (MLA_NOPE + MLA_V)), MLA_KV_RANK ** -0.5),
        "gdn_conv_w": nrm(ks[12], (DEPTH, GDN_CONV, GDN_QKV), 0.5),
        "gdn_a_log": jnp.log(jax.random.uniform(ks[13], (DEPTH, GDN_HEADS), f32, 1.0, 16.0)),
        "gdn_dt_bias": dt + jnp.log(-jnp.expm1(-dt)),
        "gdn_o_norm_w": 1.0 + nrm(ks[15], (DEPTH, GDN_DV), 0.05),
        "w_out": nrm(ks[16], (DEPTH, D_MIX, D_MODEL), D_MIX ** -0.5),
    }


def reference(x, c, positions, w_mod, b_mod, pre_norm_w, post_norm_w, w_in, mla_q_norm_w, mla_q_up,
              mla_kv_norm_w, mla_kv_up, gdn_conv_w, gdn_a_log, gdn_dt_bias, gdn_o_norm_w, w_out):
    c_act = jax.nn.silu(c)
    for l in range(DEPTH):
        mod = c_act @ w_mod[l] + b_mod[l]
        shift, scale, gate = jnp.split(mod, 3, axis=-1)
        h = rmsnorm(x, pre_norm_w[l]) * (1.0 + scale[:, None, :]) + shift[:, None, :]
        proj = h @ w_in[l]
        q_lat, kv_lat, k_pe, z_mla, qkv, a, bt, z_gdn = jnp.split(proj, IN_SPLIT_POINTS, axis=-1)
        y_mla = mla_branch(q_lat, kv_lat, k_pe, z_mla, positions, mla_q_norm_w[l], mla_q_up[l],
                           mla_kv_norm_w[l], mla_kv_up[l])
        y_gdn = gdn_branch(qkv, a, bt, z_gdn, gdn_conv_w[l], gdn_a_log[l], gdn_dt_bias[l],
                           gdn_o_norm_w[l])
        y = jnp.concatenate([y_mla, y_gdn], axis=-1) @ w_out[l]
        x = x + gate[:, None, :] * rmsnorm(y, post_norm_w[l])
    return x
```

```python
import functools

import jax
import jax.numpy as jnp
import numpy as np
from jax import lax
from jax.experimental import pallas as pl
from jax.experimental.pallas import tpu as pltpu

HEADS = 4
NOPE = 128
ROPE = 64
DQK = NOPE + ROPE
DV = 128
Q_RANK = 384
KV_RANK = 256
WIDTH = HEADS * DV
CONV_K = 4
ROPE_THETA = 10000.0
EPS = 1e-6

LANES = 128

C_QLAT = 0
C_KVLAT = C_QLAT + Q_RANK
C_KPE = C_KVLAT + KV_RANK
C_ZMLA = C_KPE + LANES
C_QKV = C_ZMLA + WIDTH
C_AB = C_QKV + 3 * WIDTH
C_ZGDN = C_AB + LANES
N_COLS = C_ZGDN + WIDTH

TOKEN_TILE = 512
ATTN_TILE = 512
GDN_CHUNK = 256
ROPE_TILE = 1024
VMEM_LIMIT = 56 * 1024 * 1024

_F32 = jnp.float32
_BF16 = jnp.bfloat16
_NEG = -1e30


def _sigmoid(x):
    return 1.0 / (1.0 + jnp.exp(-x))


def _silu(x):
    return x * _sigmoid(x)


def _softplus(x):
    return jnp.maximum(x, 0.0) + jnp.log(1.0 + jnp.exp(-jnp.abs(x)))


def _rms(x, w):
    return x * lax.rsqrt(jnp.mean(x * x, axis=-1, keepdims=True) + EPS) * w


def _dot(a, b):
    return jnp.dot(a.astype(_BF16), b.astype(_BF16), preferred_element_type=_F32)


def _dot_nt(a, b):
    return lax.dot_general(a.astype(_BF16), b.astype(_BF16), (((1,), (1,)), ((), ())),
                           preferred_element_type=_F32)


def _dot_tn(a, b):
    return lax.dot_general(a.astype(_BF16), b.astype(_BF16), (((0,), (0,)), ((), ())),
                           preferred_element_type=_F32)


def _mod_kernel(c_ref, w_ref, b_ref, o_ref):
    c_act = _silu(c_ref[...])
    o_ref[0] = jnp.dot(c_act, w_ref[0], preferred_element_type=_F32,
                       precision=lax.Precision.HIGHEST) + b_ref[0]


def _modulation(c, w_mod, b_mod):
    depth, d, d3 = w_mod.shape
    b = c.shape[0]
    return pl.pallas_call(
        _mod_kernel,
        out_shape=jax.ShapeDtypeStruct((depth, b, d3), _F32),
        grid=(depth, d3 // d),
        in_specs=[
            pl.BlockSpec((b, d), lambda l, j: (0, 0)),
            pl.BlockSpec((1, d, d), lambda l, j: (l, 0, j)),
            pl.BlockSpec((1, 1, d), lambda l, j: (l, 0, j)),
        ],
        out_specs=pl.BlockSpec((1, b, d), lambda l, j: (l, 0, j)),
        compiler_params=pltpu.CompilerParams(dimension_semantics=("arbitrary", "arbitrary")),
        name="modulation",
    )(c, w_mod, b_mod.reshape(depth, 1, d3))


def _rope_kernel(pos_ref, freq_ref, o_ref):
    ang = pos_ref[0].astype(_F32) * freq_ref[...]
    lane = lax.broadcasted_iota(jnp.int32, ang.shape, 1)
    o_ref[0] = jnp.where(lane < ROPE, jnp.cos(ang), jnp.sin(ang))


def _rope_table(positions):
    b, s = positions.shape
    half = ROPE // 2
    inv_freq = jnp.power(ROPE_THETA, -jnp.arange(half, dtype=_F32) * 2.0 / ROPE)
    freq = jnp.tile(inv_freq, LANES // half).reshape(1, LANES)
    t = min(ROPE_TILE, s)
    return pl.pallas_call(
        _rope_kernel,
        out_shape=jax.ShapeDtypeStruct((b, s, LANES), _F32),
        grid=(b, s // t),
        in_specs=[
            pl.BlockSpec((1, t, 1), lambda i, j: (i, j, 0)),
            pl.BlockSpec((1, LANES), lambda i, j: (0, 0)),
        ],
        out_specs=pl.BlockSpec((1, t, LANES), lambda i, j: (i, j, 0)),
        compiler_params=pltpu.CompilerParams(dimension_semantics=("arbitrary", "arbitrary")),
        name="rope_table",
    )(positions.reshape(b, s, 1), freq)


def _in_kernel(x_ref, mod_ref, prew_ref, win_ref, qnw_ref, qup_ref, kvnw_ref, kvup_ref, tab_ref,
               convw_ref, alog_ref, dtb_ref,
               q_ref, k_ref, v_ref, zm_ref, gq_ref, gk_ref, gv_ref, gb_ref, zg_ref,
               tail_ref):
    d = x_ref.shape[-1]
    tm = x_ref.shape[1]

    @pl.when(pl.program_id(1) == 0)
    def _():
        tail_ref[...] = jnp.zeros_like(tail_ref)

    x = x_ref[0]
    mod = mod_ref[0]
    h = _rms(x, prew_ref[...]) * (1.0 + mod[:, d:2 * d]) + mod[:, 0:d]
    proj = _dot(h, win_ref[...])

    tab = tab_ref[0]
    q_scale = DQK ** -0.5

    def rope(block):
        y = block * tab
        return y + pltpu.roll(y, ROPE, 1)

    qf = _dot(_rms(proj[:, C_QLAT:C_QLAT + Q_RANK], qnw_ref[...]), qup_ref[...])
    kvf = _dot(_rms(proj[:, C_KVLAT:C_KVLAT + KV_RANK], kvnw_ref[...]), kvup_ref[...])
    k_pe = rope(proj[:, C_KPE:C_KPE + LANES])[:, :ROPE].astype(k_ref.dtype)
    for hd in range(HEADS):
        lo = hd * NOPE
        q_pe = rope(qf[:, WIDTH + hd * LANES:WIDTH + (hd + 1) * LANES])[:, :ROPE]
        q_ref[0, hd, :, 0:NOPE] = (qf[:, lo:lo + NOPE] * q_scale).astype(q_ref.dtype)
        q_ref[0, hd, :, NOPE:DQK] = (q_pe * q_scale).astype(q_ref.dtype)
        k_ref[0, hd, :, 0:NOPE] = kvf[:, lo:lo + NOPE].astype(k_ref.dtype)
        k_ref[0, hd, :, NOPE:DQK] = k_pe
    v_ref[0] = kvf[:, WIDTH:2 * WIDTH].astype(v_ref.dtype)
    zm_ref[0] = proj[:, C_ZMLA:C_ZMLA + WIDTH].astype(zm_ref.dtype)

    qkv = proj[:, C_QKV:C_QKV + 3 * WIDTH]
    ext = jnp.concatenate([tail_ref[...], qkv], axis=0)
    cw = convw_ref[...]
    conv = qkv * cw[CONV_K - 1:CONV_K, :]
    for j in range(CONV_K - 1):
        shift = CONV_K - 1 - j
        conv = conv + ext[8 - shift:8 - shift + tm, :] * cw[j:j + 1, :]
    tail_ref[...] = qkv[tm - 8:tm, :]
    act = _silu(conv)
    for hd in range(HEADS):
        lo = hd * DV
        qh = act[:, lo:lo + DV]
        kh = act[:, WIDTH + lo:WIDTH + lo + DV]
        qn = qh * lax.rsqrt(jnp.sum(qh * qh, axis=-1, keepdims=True) + EPS) * (DV ** -0.5)
        kn = kh * lax.rsqrt(jnp.sum(kh * kh, axis=-1, keepdims=True) + EPS)
        gq_ref[0, :, lo:lo + DV] = qn.astype(gq_ref.dtype)
        gk_ref[0, :, lo:lo + DV] = kn.astype(gk_ref.dtype)
    gv_ref[0] = act[:, 2 * WIDTH:3 * WIDTH].astype(gv_ref.dtype)

    ab = proj[:, C_AB:C_AB + LANES]
    g = -jnp.exp(alog_ref[...]) * _softplus(ab + dtb_ref[...])
    lane = lax.broadcasted_iota(jnp.int32, ab.shape, 1)
    gb_ref[0] = jnp.where(lane < HEADS, g, _sigmoid(ab))
    zg_ref[0] = proj[:, C_ZGDN:C_ZGDN + WIDTH].astype(zg_ref.dtype)


def _in_call(x, mod_l, prew, win, qnw, qup, kvnw, kvup, tab, convw, alog, dtb):
    b, s, d = x.shape
    tm = min(TOKEN_TILE, s)
    const = lambda shape: pl.BlockSpec(shape, lambda i, j: (0,) * len(shape))
    tok = lambda w: pl.BlockSpec((1, tm, w), lambda i, j: (i, j, 0))
    head = pl.BlockSpec((1, HEADS, tm, DQK), lambda i, j: (i, 0, j, 0))
    out_shape = (
        jax.ShapeDtypeStruct((b, HEADS, s, DQK), _BF16),
        jax.ShapeDtypeStruct((b, HEADS, s, DQK), _BF16),
        jax.ShapeDtypeStruct((b, s, WIDTH), _BF16),
        jax.ShapeDtypeStruct((b, s, WIDTH), _BF16),
        jax.ShapeDtypeStruct((b, s, WIDTH), _BF16),
        jax.ShapeDtypeStruct((b, s, WIDTH), _BF16),
        jax.ShapeDtypeStruct((b, s, WIDTH), _BF16),
        jax.ShapeDtypeStruct((b, s, LANES), _F32),
        jax.ShapeDtypeStruct((b, s, WIDTH), _BF16),
    )
    return pl.pallas_call(
        _in_kernel,
        out_shape=out_shape,
        grid=(b, s // tm),
        in_specs=[
            tok(d),
            pl.BlockSpec((1, 1, 3 * d), lambda i, j: (i, 0, 0)),
            const((1, d)),
            const((d, N_COLS)),
            const((1, Q_RANK)),
            const((Q_RANK, 2 * WIDTH)),
            const((1, KV_RANK)),
            const((KV_RANK, 2 * WIDTH)),
            tok(LANES),
            const((CONV_K, 3 * WIDTH)),
            const((1, LANES)),
            const((1, LANES)),
        ],
        out_specs=(head, head, tok(WIDTH), tok(WIDTH), tok(WIDTH), tok(WIDTH), tok(WIDTH),
                   tok(LANES), tok(WIDTH)),
        scratch_shapes=[pltpu.VMEM((8, 3 * WIDTH), _F32)],
        compiler_params=pltpu.CompilerParams(dimension_semantics=("arbitrary", "arbitrary"),
                                             vmem_limit_bytes=VMEM_LIMIT),
        name="in_proj",
    )(x, mod_l, prew, win, qnw, qup, kvnw, kvup, tab, convw, alog, dtb)


def _attn_kernel(q_ref, k_ref, v_ref, o_ref):
    t = q_ref.shape[2]
    i = pl.program_id(2)
    q = q_ref[0, 0]

    def step(j, carry, diagonal):
        m, l, acc = carry
        start = pl.multiple_of(j * t, t)
        s = _dot_nt(q, k_ref[0, 0, pl.ds(start, t), :])
        if diagonal:
            row = lax.broadcasted_iota(jnp.int32, s.shape, 0)
            col = lax.broadcasted_iota(jnp.int32, s.shape, 1)
            s = jnp.where(col <= row, s, _NEG)
        m_new = jnp.maximum(m, jnp.max(s, axis=-1, keepdims=True))
        alpha = jnp.exp(m - m_new)
        p = jnp.exp(s - m_new)
        l = alpha * l + jnp.sum(p, axis=-1, keepdims=True)
        acc = alpha * acc + _dot(p, v_ref[0, pl.ds(start, t), :])
        return m_new, l, acc

    init = (jnp.full((t, 1), _NEG, _F32), jnp.zeros((t, 1), _F32), jnp.zeros((t, DV), _F32))
    carry = lax.fori_loop(0, i, lambda j, c: step(j, c, False), init)
    _, l, acc = step(i, carry, True)
    o_ref[0] = (acc / l).astype(o_ref.dtype)


def _attn_call(q, k, v):
    b, _, s, _ = q.shape
    t = min(ATTN_TILE, s)
    return pl.pallas_call(
        _attn_kernel,
        out_shape=jax.ShapeDtypeStruct((b, s, WIDTH), _BF16),
        grid=(b, HEADS, s // t),
        in_specs=[
            pl.BlockSpec((1, 1, t, DQK), lambda bi, h, i: (bi, h, i, 0)),
            pl.BlockSpec((1, 1, s, DQK), lambda bi, h, i: (bi, h, 0, 0)),
            pl.BlockSpec((1, s, DV), lambda bi, h, i: (bi, 0, h)),
        ],
        out_specs=pl.BlockSpec((1, t, DV), lambda bi, h, i: (bi, i, h)),
        compiler_params=pltpu.CompilerParams(
            dimension_semantics=("arbitrary", "arbitrary", "arbitrary"), vmem_limit_bytes=VMEM_LIMIT),
        name="mla_attention",
    )(q, k, v)


def _gdn_kernel(q_ref, k_ref, v_ref, gb_ref, o_ref, state_ref):
    c = q_ref.shape[1]

    @pl.when(pl.program_id(1) == 0)
    def _():
        state_ref[...] = jnp.zeros_like(state_ref)

    row = lax.broadcasted_iota(jnp.int32, (c, c), 0)
    col = lax.broadcasted_iota(jnp.int32, (c, c), 1)
    incl = col <= row
    strict = col < row
    eye = (col == row).astype(_F32)

    gb = gb_ref[0]
    gcum = jnp.dot(incl.astype(_F32), gb, preferred_element_type=_F32,
                   precision=lax.Precision.HIGHEST)
    gcum_t = gcum.T

    for hd in range(HEADS):
        lo = hd * DV
        q = q_ref[0, :, lo:lo + DV].astype(_F32)
        k = k_ref[0, :, lo:lo + DV].astype(_F32)
        v = v_ref[0, :, lo:lo + DV].astype(_F32)
        beta = gb[:, HEADS + hd:HEADS + hd + 1]
        g_col = gcum[:, hd:hd + 1]
        g_row = gcum_t[hd:hd + 1, :]
        g_last = gcum[c - 1:c, hd:hd + 1]
        decay = jnp.exp(jnp.where(incl, g_col - g_row, _NEG))
        e_col = jnp.exp(g_col)

        kb = k * beta
        n_mat = jnp.where(strict, _dot_nt(kb, k) * decay, 0.0)
        xor = row ^ col
        t_inv = eye - jnp.where(xor == 1, n_mat, 0.0)
        m = 2
        while m < c:
            l_m = jnp.where((xor >= m) & (xor < 2 * m), n_mat, 0.0)
            t_inv = t_inv - _dot(_dot(t_inv, l_m), t_inv)
            m *= 2

        sol = _dot(t_inv, jnp.concatenate([v * beta, kb * e_col], axis=1))
        u = sol[:, :DV]
        w = sol[:, DV:]
        qk = _dot_nt(q, k) * decay
        state = state_ref[hd]
        v_new = u - _dot(w, state)
        o = _dot(q * e_col, state) + _dot(qk, v_new)
        k_dec = k * jnp.exp(g_last - g_col)
        state_ref[hd] = state * jnp.exp(g_last) + _dot_tn(k_dec, v_new)
        o_ref[0, :, lo:lo + DV] = o.astype(o_ref.dtype)


def _gdn_call(gq, gk, gv, gb):
    b, s, _ = gq.shape
    c = min(GDN_CHUNK, s)
    tok = lambda w: pl.BlockSpec((1, c, w), lambda i, j: (i, j, 0))
    return pl.pallas_call(
        _gdn_kernel,
        out_shape=jax.ShapeDtypeStruct((b, s, WIDTH), _F32),
        grid=(b, s // c),
        in_specs=[tok(WIDTH), tok(WIDTH), tok(WIDTH), tok(LANES)],
        out_specs=tok(WIDTH),
        scratch_shapes=[pltpu.VMEM((HEADS, DV, DV), _F32)],
        compiler_params=pltpu.CompilerParams(dimension_semantics=("arbitrary", "arbitrary"),
                                             vmem_limit_bytes=VMEM_LIMIT),
        name="gated_delta_rule",
    )(gq, gk, gv, gb)


def _out_kernel(x_ref, mod_ref, om_ref, zm_ref, og_ref, zg_ref, onw_ref, wout_ref, postw_ref, o_ref):
    d = x_ref.shape[-1]
    y_mla = om_ref[0].astype(_F32) * _silu(zm_ref[0].astype(_F32))
    og = og_ref[0]
    zg = _silu(zg_ref[0].astype(_F32))
    parts = [y_mla]
    for hd in range(HEADS):
        lo = hd * DV
        parts.append(_rms(og[:, lo:lo + DV], onw_ref[...]) * zg[:, lo:lo + DV])
    y = _dot(jnp.concatenate(parts, axis=1), wout_ref[...])
    gate = mod_ref[0][:, 2 * d:3 * d]
    o_ref[0] = x_ref[0] + gate * _rms(y, postw_ref[...])


def _out_call(x, mod_l, o_mla, z_mla, o_gdn, z_gdn, onw, wout, postw):
    b, s, d = x.shape
    tm = min(TOKEN_TILE, s)
    const = lambda shape: pl.BlockSpec(shape, lambda i, j: (0,) * len(shape))
    tok = lambda w: pl.BlockSpec((1, tm, w), lambda i, j: (i, j, 0))
    return pl.pallas_call(
        _out_kernel,
        out_shape=jax.ShapeDtypeStruct((b, s, d), _F32),
        grid=(b, s // tm),
        in_specs=[
            tok(d),
            pl.BlockSpec((1, 1, 3 * d), lambda i, j: (i, 0, 0)),
            tok(WIDTH), tok(WIDTH), tok(WIDTH), tok(WIDTH),
            const((1, DV)),
            const((2 * WIDTH, d)),
            const((1, d)),
        ],
        out_specs=tok(d),
        compiler_params=pltpu.CompilerParams(dimension_semantics=("arbitrary", "arbitrary"),
                                             vmem_limit_bytes=VMEM_LIMIT),
        name="out_proj",
    )(x, mod_l, o_mla, z_mla, o_gdn, z_gdn, onw, wout, postw)


def _rot_cols(w):
    half = w.shape[-1] // 2
    return jnp.concatenate([-w[:, half:], w[:, :half]], axis=-1)


def _pad_cols(w, n):
    return jnp.pad(w, ((0, 0), (0, n - w.shape[-1])))


def _prep_w_in(w):
    o_kv = Q_RANK
    o_kpe = o_kv + KV_RANK
    o_zm = o_kpe + ROPE
    o_qkv = o_zm + WIDTH
    o_a = o_qkv + 3 * WIDTH
    o_zg = o_a + 2 * HEADS
    k_pe = w[:, o_kpe:o_zm]
    return jnp.concatenate([
        w[:, :o_kpe], k_pe, _rot_cols(k_pe), w[:, o_zm:o_qkv], w[:, o_qkv:o_a],
        _pad_cols(w[:, o_a:o_zg], LANES), w[:, o_zg:],
    ], axis=1).astype(_BF16)


def _prep_q_up(w):
    per = w.reshape(Q_RANK, HEADS, DQK)
    nope = per[:, :, :NOPE].reshape(Q_RANK, HEADS * NOPE)
    pe = [jnp.concatenate([per[:, h, NOPE:], _rot_cols(per[:, h, NOPE:])], axis=1) for h in range(HEADS)]
    return jnp.concatenate([nope] + pe, axis=1).astype(_BF16)


def _prep_kv_up(w):
    per = w.reshape(KV_RANK, HEADS, NOPE + DV)
    return jnp.concatenate([per[:, :, :NOPE].reshape(KV_RANK, HEADS * NOPE),
                            per[:, :, NOPE:].reshape(KV_RANK, HEADS * DV)], axis=1).astype(_BF16)


def _lane_row(vals):
    return _pad_cols(vals.reshape(1, -1).astype(_F32), LANES)


def kernel(x, c, positions, w_mod, b_mod, pre_norm_w, post_norm_w, w_in, mla_q_norm_w, mla_q_up,
           mla_kv_norm_w, mla_kv_up, gdn_conv_w, gdn_a_log, gdn_dt_bias, gdn_o_norm_w, w_out):
    depth = w_mod.shape[0]
    b, s, d = x.shape
    assert s % min(TOKEN_TILE, s) == 0 and s % min(ATTN_TILE, s) == 0 and s % min(GDN_CHUNK, s) == 0
    mod = _modulation(c, w_mod, b_mod)
    tab = _rope_table(positions)
    for l in range(depth):
        mod_l = mod[l].reshape(b, 1, 3 * d)
        q, k, v, z_mla, gq, gk, gv, gb, z_gdn = _in_call(
            x, mod_l, pre_norm_w[l].reshape(1, d), _prep_w_in(w_in[l]),
            mla_q_norm_w[l].reshape(1, Q_RANK), _prep_q_up(mla_q_up[l]),
            mla_kv_norm_w[l].reshape(1, KV_RANK), _prep_kv_up(mla_kv_up[l]), tab,
            gdn_conv_w[l], _lane_row(gdn_a_log[l]), _lane_row(gdn_dt_bias[l]))
        o_mla = _attn_call(q, k, v)
        o_gdn = _gdn_call(gq, gk, gv, gb)
        x = _out_call(x, mod_l, o_mla, z_mla, o_gdn, z_gdn, gdn_o_norm_w[l].reshape(1, DV),
                      w_out[l].astype(_BF16), post_norm_w[l].reshape(1, d))
    return x
```

```python
import functools

import jax
import jax.numpy as jnp
import numpy as np
from jax import lax
from jax.experimental import pallas as pl
from jax.experimental.pallas import tpu as pltpu

HEADS = 4
NOPE = 128
ROPE = 64
DQK = NOPE + ROPE
DV = 128
Q_RANK = 384
KV_RANK = 256
WIDTH = HEADS * DV
CONV_K = 4
ROPE_THETA = 10000.0
EPS = 1e-6

LANES = 128

C_QLAT = 0
C_KVLAT = C_QLAT + Q_RANK
C_KPE = C_KVLAT + KV_RANK
C_ZMLA = C_KPE + LANES
C_QKV = C_ZMLA + WIDTH
C_AB = C_QKV + 3 * WIDTH
C_ZGDN = C_AB + LANES
N_COLS = C_ZGDN + WIDTH

TOKEN_TILE = 512
ATTN_SUB = 256
GDN_CHUNK = 256
GDN_TILE = 512
ROPE_TILE = 1024
VMEM_LIMIT = 56 * 1024 * 1024

_F32 = jnp.float32
_BF16 = jnp.bfloat16
_NEG = -1e30
LOG2E = float(np.log2(np.e))


def _sigmoid(x):
    return 1.0 / (1.0 + jnp.exp(-x))


def _silu(x):
    return x * _sigmoid(x)


def _softplus(x):
    return jnp.maximum(x, 0.0) + jnp.log(1.0 + jnp.exp(-jnp.abs(x)))


def _rms(x, w):
    return x * lax.rsqrt(jnp.mean(x * x, axis=-1, keepdims=True) + EPS) * w


def _dot(a, b):
    return jnp.dot(a.astype(_BF16), b.astype(_BF16), preferred_element_type=_F32)


def _dot_nt(a, b):
    return lax.dot_general(a.astype(_BF16), b.astype(_BF16), (((1,), (1,)), ((), ())),
                           preferred_element_type=_F32)


def _dot_tn(a, b):
    return lax.dot_general(a.astype(_BF16), b.astype(_BF16), (((0,), (0,)), ((), ())),
                           preferred_element_type=_F32)


def _mod_kernel(c_ref, w_ref, b_ref, o_ref):
    c_act = _silu(c_ref[...])
    o_ref[0] = jnp.dot(c_act, w_ref[0], preferred_element_type=_F32,
                       precision=lax.Precision.HIGHEST) + b_ref[0]


def _modulation(c, w_mod, b_mod):
    depth, d, d3 = w_mod.shape
    b = c.shape[0]
    return pl.pallas_call(
        _mod_kernel,
        out_shape=jax.ShapeDtypeStruct((depth, b, d3), _F32),
        grid=(depth, d3 // d),
        in_specs=[
            pl.BlockSpec((b, d), lambda l, j: (0, 0)),
            pl.BlockSpec((1, d, d), lambda l, j: (l, 0, j)),
            pl.BlockSpec((1, 1, d), lambda l, j: (l, 0, j)),
        ],
        out_specs=pl.BlockSpec((1, b, d), lambda l, j: (l, 0, j)),
        compiler_params=pltpu.CompilerParams(dimension_semantics=("arbitrary", "arbitrary")),
        name="modulation",
    )(c, w_mod, b_mod.reshape(depth, 1, d3))


def _rope_kernel(pos_ref, freq_ref, o_ref):
    ang = pos_ref[0].astype(_F32) * freq_ref[...]
    lane = lax.broadcasted_iota(jnp.int32, ang.shape, 1)
    o_ref[0] = jnp.where(lane < ROPE, jnp.cos(ang), jnp.sin(ang))


def _rope_table(positions):
    b, s = positions.shape
    half = ROPE // 2
    inv_freq = jnp.power(ROPE_THETA, -jnp.arange(half, dtype=_F32) * 2.0 / ROPE)
    freq = jnp.tile(inv_freq, LANES // half).reshape(1, LANES)
    t = min(ROPE_TILE, s)
    return pl.pallas_call(
        _rope_kernel,
        out_shape=jax.ShapeDtypeStruct((b, s, LANES), _F32),
        grid=(b, s // t),
        in_specs=[
            pl.BlockSpec((1, t, 1), lambda i, j: (i, j, 0)),
            pl.BlockSpec((1, LANES), lambda i, j: (0, 0)),
        ],
        out_specs=pl.BlockSpec((1, t, LANES), lambda i, j: (i, j, 0)),
        compiler_params=pltpu.CompilerParams(dimension_semantics=("arbitrary", "arbitrary")),
        name="rope_table",
    )(positions.reshape(b, s, 1), freq)


def _in_kernel(x_ref, mod_ref, prew_ref, win_ref, qnw_ref, qup_ref, kvnw_ref, kvup_ref, tab_ref,
               convw_ref, alog_ref, dtb_ref,
               qt_ref, k_ref, vt_ref, zm_ref, gq_ref, gk_ref, gv_ref, gb_ref, zg_ref,
               tail_ref):
    d = x_ref.shape[-1]
    tm = x_ref.shape[1]

    @pl.when(pl.program_id(1) == 0)
    def _():
        tail_ref[...] = jnp.zeros_like(tail_ref)

    x = x_ref[0]
    mod = mod_ref[0]
    h = _rms(x, prew_ref[...]) * (1.0 + mod[:, d:2 * d]) + mod[:, 0:d]
    proj = _dot(h, win_ref[...])

    tab = tab_ref[0]
    q_scale = DQK ** -0.5 * LOG2E

    def rope(block):
        y = block * tab
        return y + pltpu.roll(y, ROPE, 1)

    qf = _dot(_rms(proj[:, C_QLAT:C_QLAT + Q_RANK], qnw_ref[...]), qup_ref[...])
    kvf = _dot(_rms(proj[:, C_KVLAT:C_KVLAT + KV_RANK], kvnw_ref[...]), kvup_ref[...])
    k_pe = rope(proj[:, C_KPE:C_KPE + LANES])[:, :ROPE].astype(k_ref.dtype)
    for hd in range(HEADS):
        lo = hd * NOPE
        q_pe = rope(qf[:, WIDTH + hd * LANES:WIDTH + (hd + 1) * LANES])
        qt_ref[0, hd, 0:NOPE, :] = (qf[:, lo:lo + NOPE] * q_scale).T.astype(qt_ref.dtype)
        qt_ref[0, hd, NOPE:DQK, :] = (q_pe * q_scale).T[:ROPE, :].astype(qt_ref.dtype)
        k_ref[0, hd, :, 0:NOPE] = kvf[:, lo:lo + NOPE].astype(k_ref.dtype)
        k_ref[0, hd, :, NOPE:DQK] = k_pe
        vt_ref[0, hd, 0] = kvf[:, WIDTH + lo:WIDTH + lo + DV].T.astype(vt_ref.dtype)
    zm_ref[0] = proj[:, C_ZMLA:C_ZMLA + WIDTH].astype(zm_ref.dtype)

    qkv = proj[:, C_QKV:C_QKV + 3 * WIDTH]
    ext = jnp.concatenate([tail_ref[...], qkv], axis=0)
    cw = convw_ref[...]
    conv = qkv * cw[CONV_K - 1:CONV_K, :]
    for j in range(CONV_K - 1):
        shift = CONV_K - 1 - j
        conv = conv + ext[8 - shift:8 - shift + tm, :] * cw[j:j + 1, :]
    tail_ref[...] = qkv[tm - 8:tm, :]
    act = _silu(conv)
    for hd in range(HEADS):
        lo = hd * DV
        qh = act[:, lo:lo + DV]
        kh = act[:, WIDTH + lo:WIDTH + lo + DV]
        qn = qh * lax.rsqrt(jnp.sum(qh * qh, axis=-1, keepdims=True) + EPS) * (DV ** -0.5)
        kn = kh * lax.rsqrt(jnp.sum(kh * kh, axis=-1, keepdims=True) + EPS)
        gq_ref[0, :, lo:lo + DV] = qn.astype(gq_ref.dtype)
        gk_ref[0, :, lo:lo + DV] = kn.astype(gk_ref.dtype)
    gv_ref[0] = act[:, 2 * WIDTH:3 * WIDTH].astype(gv_ref.dtype)

    ab = proj[:, C_AB:C_AB + LANES]
    g = -jnp.exp(alog_ref[...]) * _softplus(ab + dtb_ref[...])
    lane = lax.broadcasted_iota(jnp.int32, ab.shape, 1)
    gb_ref[0] = jnp.where(lane < HEADS, g, _sigmoid(ab))
    zg_ref[0] = proj[:, C_ZGDN:C_ZGDN + WIDTH].astype(zg_ref.dtype)


def _in_call(x, mod_l, prew, win, qnw, qup, kvnw, kvup, tab, convw, alog, dtb):
    b, s, d = x.shape
    tm = min(TOKEN_TILE, s)
    const = lambda shape: pl.BlockSpec(shape, lambda i, j: (0,) * len(shape))
    tok = lambda w: pl.BlockSpec((1, tm, w), lambda i, j: (i, j, 0))
    head = pl.BlockSpec((1, HEADS, tm, DQK), lambda i, j: (i, 0, j, 0))
    head_t = pl.BlockSpec((1, HEADS, DQK, tm), lambda i, j: (i, 0, 0, j))
    head_vt = pl.BlockSpec((1, HEADS, 1, DV, tm), lambda i, j: (i, 0, j, 0, 0))
    out_shape = (
        jax.ShapeDtypeStruct((b, HEADS, DQK, s), _BF16),
        jax.ShapeDtypeStruct((b, HEADS, s, DQK), _BF16),
        jax.ShapeDtypeStruct((b, HEADS, s // tm, DV, tm), _BF16),
        jax.ShapeDtypeStruct((b, s, WIDTH), _BF16),
        jax.ShapeDtypeStruct((b, s, WIDTH), _BF16),
        jax.ShapeDtypeStruct((b, s, WIDTH), _BF16),
        jax.ShapeDtypeStruct((b, s, WIDTH), _BF16),
        jax.ShapeDtypeStruct((b, s, LANES), _F32),
        jax.ShapeDtypeStruct((b, s, WIDTH), _BF16),
    )
    return pl.pallas_call(
        _in_kernel,
        out_shape=out_shape,
        grid=(b, s // tm),
        in_specs=[
            tok(d),
            pl.BlockSpec((1, 1, 3 * d), lambda i, j: (i, 0, 0)),
            const((1, d)),
            const((d, N_COLS)),
            const((1, Q_RANK)),
            const((Q_RANK, 2 * WIDTH)),
            const((1, KV_RANK)),
            const((KV_RANK, 2 * WIDTH)),
            tok(LANES),
            const((CONV_K, 3 * WIDTH)),
            const((1, LANES)),
            const((1, LANES)),
        ],
        out_specs=(head_t, head, head_vt, tok(WIDTH), tok(WIDTH), tok(WIDTH), tok(WIDTH),
                   tok(LANES), tok(WIDTH)),
        scratch_shapes=[pltpu.VMEM((8, 3 * WIDTH), _F32)],
        compiler_params=pltpu.CompilerParams(dimension_semantics=("arbitrary", "arbitrary"),
                                             vmem_limit_bytes=VMEM_LIMIT),
        name="in_proj",
    )(x, mod_l, prew, win, qnw, qup, kvnw, kvup, tab, convw, alog, dtb)


def _attn_kernel(qt_ref, k_ref, vt_ref, o_ref, sa_ref, sb_ref, mxa_ref, mxb_ref, m_ref, l_ref, acc_ref):
    t = qt_ref.shape[3]
    u = ATTN_SUB
    i = pl.program_id(2)
    qts = [qt_ref[0, 0, :, h * u:(h + 1) * u] for h in range(2)]
    tri = lax.broadcasted_iota(jnp.int32, (u, u), 0) <= lax.broadcasted_iota(jnp.int32, (u, u), 1)

    m_ref[...] = jnp.full_like(m_ref, _NEG)
    l_ref[...] = jnp.zeros_like(l_ref)
    acc_ref[...] = jnp.zeros_like(acc_ref)

    def produce(s_ref, mx_ref, key_start, subs=(0, 1)):
        kj = k_ref[0, 0, pl.ds(pl.multiple_of(key_start, u), u), :]
        for h in subs:
            s = jnp.dot(kj, qts[h], preferred_element_type=_F32)
            s_ref[h] = s
            mx_ref[h] = jnp.max(s, axis=0, keepdims=True)

    def consume(s_ref, mx_ref, vt, subs=(0, 1), masked=()):
        for h in subs:
            s = s_ref[h]
            if h in masked:
                s = jnp.where(tri, s, _NEG)
                mx = jnp.max(s, axis=0, keepdims=True)
            else:
                mx = mx_ref[h]
            m = m_ref[h]
            m_new = jnp.maximum(m, mx)
            alpha = jnp.exp2(m - m_new)
            p = jnp.exp2(s - m_new)
            l_ref[h] = alpha * l_ref[h] + jnp.sum(p, axis=0, keepdims=True)
            acc_ref[h] = alpha * acc_ref[h] + jnp.dot(vt, p.astype(_BF16), preferred_element_type=_F32)
            m_ref[h] = m_new

    produce(sa_ref, mxa_ref, 0)

    def body(j, carry):
        vt = vt_ref[0, 0, j]
        produce(sb_ref, mxb_ref, j * t + u)
        consume(sa_ref, mxa_ref, vt[:, :u])
        produce(sa_ref, mxa_ref, (j + 1) * t)
        consume(sb_ref, mxb_ref, vt[:, u:])
        return carry

    lax.fori_loop(0, i, body, 0)
    vt = vt_ref[0, 0, i]
    produce(sb_ref, mxb_ref, i * t + u, subs=(1,))
    consume(sa_ref, mxa_ref, vt[:, :u], masked=(0,))
    consume(sb_ref, mxb_ref, vt[:, u:], subs=(1,), masked=(1,))
    for h in range(2):
        o_ref[0, h * u:(h + 1) * u, :] = (acc_ref[h] / l_ref[h]).T.astype(o_ref.dtype)


def _attn_call(qt, k, vt):
    b, _, _, s = qt.shape
    t = vt.shape[-1]
    return pl.pallas_call(
        _attn_kernel,
        out_shape=jax.ShapeDtypeStruct((b, s, WIDTH), _BF16),
        grid=(b, HEADS, s // t),
        in_specs=[
            pl.BlockSpec((1, 1, DQK, t), lambda bi, h, i: (bi, h, 0, i)),
            pl.BlockSpec((1, 1, s, DQK), lambda bi, h, i: (bi, h, 0, 0)),
            pl.BlockSpec((1, 1, s // t, DV, t), lambda bi, h, i: (bi, h, 0, 0, 0)),
        ],
        out_specs=pl.BlockSpec((1, t, DV), lambda bi, h, i: (bi, i, h)),
        scratch_shapes=[
            pltpu.VMEM((2, ATTN_SUB, ATTN_SUB), _F32),
            pltpu.VMEM((2, ATTN_SUB, ATTN_SUB), _F32),
            pltpu.VMEM((2, 1, ATTN_SUB), _F32),
            pltpu.VMEM((2, 1, ATTN_SUB), _F32),
            pltpu.VMEM((2, 1, ATTN_SUB), _F32),
            pltpu.VMEM((2, 1, ATTN_SUB), _F32),
            pltpu.VMEM((2, DV, ATTN_SUB), _F32),
        ],
        compiler_params=pltpu.CompilerParams(
            dimension_semantics=("arbitrary", "arbitrary", "arbitrary"), vmem_limit_bytes=VMEM_LIMIT),
        name="mla_attention",
    )(qt, k, vt)


def _cumsum_rows(tril_bf16, x):
    lane = lax.broadcasted_iota(jnp.int32, x.shape, 1)
    hi = x.astype(_BF16).astype(_F32)
    r1 = x - hi
    mid = r1.astype(_BF16).astype(_F32)
    low = r1 - mid
    packed = jnp.where(lane < 8, hi, jnp.where(lane < 16, pltpu.roll(mid, 8, 1), pltpu.roll(low, 16, 1)))
    res = jnp.dot(tril_bf16, packed.astype(_BF16), preferred_element_type=_F32)
    return res + pltpu.roll(res, LANES - 8, 1) + pltpu.roll(res, LANES - 16, 1)


def _gdn_kernel(q_ref, k_ref, v_ref, gb_ref, o_ref, state_ref, gc_ref, n_ref, t_ref, qk_ref, rhs_ref,
                qe_ref, kd_ref, sol_ref, ou_ref, qw_ref, su_ref, sw_ref):
    c = GDN_CHUNK
    n_chunks = q_ref.shape[1] // c
    n_chains = n_chunks * HEADS

    @pl.when(pl.program_id(1) == 0)
    def _():
        state_ref[...] = jnp.zeros_like(state_ref)

    row = lax.broadcasted_iota(jnp.int32, (c, c), 0)
    col = lax.broadcasted_iota(jnp.int32, (c, c), 1)
    incl = col <= row
    xor = row ^ col
    tril = incl.astype(_BF16)

    for ci in range(n_chunks):
        rows = slice(ci * c, (ci + 1) * c)
        gb = gb_ref[0, rows, :]
        gcum = _cumsum_rows(tril, gb)
        gc_ref[ci] = gcum
        gcum_t = gcum.T
        for hd in range(HEADS):
            ch = ci * HEADS + hd
            lo = hd * DV
            q = q_ref[0, rows, lo:lo + DV].astype(_F32)
            k = k_ref[0, rows, lo:lo + DV].astype(_F32)
            v = v_ref[0, rows, lo:lo + DV].astype(_F32)
            beta = jnp.broadcast_to(gb[:, HEADS + hd:HEADS + hd + 1], (c, DV))
            g_col = jnp.broadcast_to(gcum[:, hd:hd + 1], (c, DV))
            g_row = gcum_t[hd:hd + 1, :]
            g_last = gcum[c - 1:c, hd:hd + 1]
            decay = jnp.exp(jnp.where(incl, jnp.concatenate([g_col, g_col], axis=1) - g_row, _NEG))
            e_col = jnp.exp(g_col)
            kb = k * beta
            n_mat = _dot_nt(kb, k) * decay
            n_ref[ch] = n_mat
            t_ref[ch] = jnp.where(xor == 0, 1.0, jnp.where(xor == 1, -n_mat, 0.0))
            qk_ref[ch] = (_dot_nt(q, k) * decay).astype(qk_ref.dtype)
            rhs_ref[ch] = jnp.concatenate([v * beta, kb * e_col], axis=1).astype(rhs_ref.dtype)
            qe_ref[ch] = (q * e_col).astype(qe_ref.dtype)
            kd_ref[ch] = (k * jnp.exp(g_last - g_col)).astype(kd_ref.dtype)

    def level(lvl, carry):
        m = jnp.left_shift(2, lvl)
        blk = lax.broadcasted_iota(jnp.int32, (c, c), 0) ^ lax.broadcasted_iota(jnp.int32, (c, c), 1)
        mask = (blk >= m) & (blk < 2 * m)
        left = []
        for ch in range(n_chains):
            left.append(_dot(t_ref[ch], jnp.where(mask, n_ref[ch], 0.0)))
        for ch in range(n_chains):
            t_inv = t_ref[ch]
            t_ref[ch] = t_inv - _dot(left[ch], t_inv)
        return carry

    lax.fori_loop(0, int(np.log2(c)) - 1, level, 0)

    for ch in range(n_chains):
        sol_ref[ch] = _dot(t_ref[ch], rhs_ref[ch]).astype(sol_ref.dtype)
    for ch in range(n_chains):
        qk_sol = _dot(qk_ref[ch], sol_ref[ch])
        ou_ref[ch] = qk_sol[:, :DV]
        qw_ref[ch] = (qe_ref[ch].astype(_F32) - qk_sol[:, DV:]).astype(qw_ref.dtype)
    for ch in range(n_chains):
        kd_sol = _dot_tn(kd_ref[ch], sol_ref[ch])
        su_ref[ch] = kd_sol[:, :DV]
        sw_ref[ch] = kd_sol[:, DV:].astype(sw_ref.dtype)
    for ci in range(n_chunks):
        rows = slice(ci * c, (ci + 1) * c)
        for hd in range(HEADS):
            ch = ci * HEADS + hd
            lo = hd * DV
            g_last = gc_ref[ci, c - 1:c, hd:hd + 1]
            state = state_ref[hd]
            o_ref[0, rows, lo:lo + DV] = (_dot(qw_ref[ch], state) + ou_ref[ch]).astype(o_ref.dtype)
            state_ref[hd] = state * jnp.exp(g_last) + su_ref[ch] - _dot(sw_ref[ch], state)


def _gdn_call(gq, gk, gv, gb):
    b, s, _ = gq.shape
    t = GDN_TILE
    n_chains = (t // GDN_CHUNK) * HEADS
    tok = lambda w: pl.BlockSpec((1, t, w), lambda i, j: (i, j, 0))
    return pl.pallas_call(
        _gdn_kernel,
        out_shape=jax.ShapeDtypeStruct((b, s, WIDTH), _F32),
        grid=(b, s // t),
        in_specs=[tok(WIDTH), tok(WIDTH), tok(WIDTH), tok(LANES)],
        out_specs=tok(WIDTH),
        scratch_shapes=[
            pltpu.VMEM((HEADS, DV, DV), _F32),
            pltpu.VMEM((t // GDN_CHUNK, GDN_CHUNK, LANES), _F32),
            pltpu.VMEM((n_chains, GDN_CHUNK, GDN_CHUNK), _F32),
            pltpu.VMEM((n_chains, GDN_CHUNK, GDN_CHUNK), _F32),
            pltpu.VMEM((n_chains, GDN_CHUNK, GDN_CHUNK), _BF16),
            pltpu.VMEM((n_chains, GDN_CHUNK, 2 * DV), _BF16),
            pltpu.VMEM((n_chains, GDN_CHUNK, DV), _BF16),
            pltpu.VMEM((n_chains, GDN_CHUNK, DV), _BF16),
            pltpu.VMEM((n_chains, GDN_CHUNK, 2 * DV), _BF16),
            pltpu.VMEM((n_chains, GDN_CHUNK, DV), _F32),
            pltpu.VMEM((n_chains, GDN_CHUNK, DV), _BF16),
            pltpu.VMEM((n_chains, DV, DV), _F32),
            pltpu.VMEM((n_chains, DV, DV), _BF16),
        ],
        compiler_params=pltpu.CompilerParams(dimension_semantics=("arbitrary", "arbitrary"),
                                             vmem_limit_bytes=VMEM_LIMIT),
        name="gated_delta_rule",
    )(gq, gk, gv, gb)


def _out_kernel(x_ref, mod_ref, om_ref, zm_ref, og_ref, zg_ref, onw_ref, wout_ref, postw_ref, o_ref):
    d = x_ref.shape[-1]
    y_mla = om_ref[0].astype(_F32) * _silu(zm_ref[0].astype(_F32))
    og = og_ref[0]
    zg = _silu(zg_ref[0].astype(_F32))
    parts = [y_mla]
    for hd in range(HEADS):
        lo = hd * DV
        parts.append(_rms(og[:, lo:lo + DV], onw_ref[...]) * zg[:, lo:lo + DV])
    y = _dot(jnp.concatenate(parts, axis=1), wout_ref[...])
    gate = mod_ref[0][:, 2 * d:3 * d]
    o_ref[0] = x_ref[0] + gate * _rms(y, postw_ref[...])


def _out_call(x, mod_l, o_mla, z_mla, o_gdn, z_gdn, onw, wout, postw):
    b, s, d = x.shape
    tm = min(TOKEN_TILE, s)
    const = lambda shape: pl.BlockSpec(shape, lambda i, j: (0,) * len(shape))
    tok = lambda w: pl.BlockSpec((1, tm, w), lambda i, j: (i, j, 0))
    return pl.pallas_call(
        _out_kernel,
        out_shape=jax.ShapeDtypeStruct((b, s, d), _F32),
        grid=(b, s // tm),
        in_specs=[
            tok(d),
            pl.BlockSpec((1, 1, 3 * d), lambda i, j: (i, 0, 0)),
            tok(WIDTH), tok(WIDTH), tok(WIDTH), tok(WIDTH),
            const((1, DV)),
            const((2 * WIDTH, d)),
            const((1, d)),
        ],
        out_specs=tok(d),
        compiler_params=pltpu.CompilerParams(dimension_semantics=("arbitrary", "arbitrary"),
                                             vmem_limit_bytes=VMEM_LIMIT),
        name="out_proj",
    )(x, mod_l, o_mla, z_mla, o_gdn, z_gdn, onw, wout, postw)


def _rot_cols(w):
    half = w.shape[-1] // 2
    return jnp.concatenate([-w[:, half:], w[:, :half]], axis=-1)


def _pad_cols(w, n):
    return jnp.pad(w, ((0, 0), (0, n - w.shape[-1])))


def _prep_w_in(w):
    o_kv = Q_RANK
    o_kpe = o_kv + KV_RANK
    o_zm = o_kpe + ROPE
    o_qkv = o_zm + WIDTH
    o_a = o_qkv + 3 * WIDTH
    o_zg = o_a + 2 * HEADS
    k_pe = w[:, o_kpe:o_zm]
    return jnp.concatenate([
        w[:, :o_kpe], k_pe, _rot_cols(k_pe), w[:, o_zm:o_qkv], w[:, o_qkv:o_a],
        _pad_cols(w[:, o_a:o_zg], LANES), w[:, o_zg:],
    ], axis=1).astype(_BF16)


def _prep_q_up(w):
    per = w.reshape(Q_RANK, HEADS, DQK)
    nope = per[:, :, :NOPE].reshape(Q_RANK, HEADS * NOPE)
    pe = [jnp.concatenate([per[:, h, NOPE:], _rot_cols(per[:, h, NOPE:])], axis=1) for h in range(HEADS)]
    return jnp.concatenate([nope] + pe, axis=1).astype(_BF16)


def _prep_kv_up(w):
    per = w.reshape(KV_RANK, HEADS, NOPE + DV)
    return jnp.concatenate([per[:, :, :NOPE].reshape(KV_RANK, HEADS * NOPE),
                            per[:, :, NOPE:].reshape(KV_RANK, HEADS * DV)], axis=1).astype(_BF16)


def _lane_row(vals):
    return _pad_cols(vals.reshape(1, -1).astype(_F32), LANES)


def kernel(x, c, positions, w_mod, b_mod, pre_norm_w, post_norm_w, w_in, mla_q_norm_w, mla_q_up,
           mla_kv_norm_w, mla_kv_up, gdn_conv_w, gdn_a_log, gdn_dt_bias, gdn_o_norm_w, w_out):
    depth = w_mod.shape[0]
    b, s, d = x.shape
    assert s % TOKEN_TILE == 0 and TOKEN_TILE == 2 * ATTN_SUB and s % GDN_TILE == 0
    mod = _modulation(c, w_mod, b_mod)
    tab = _rope_table(positions)
    for l in range(depth):
        mod_l = mod[l].reshape(b, 1, 3 * d)
        q, k, v, z_mla, gq, gk, gv, gb, z_gdn = _in_call(
            x, mod_l, pre_norm_w[l].reshape(1, d), _prep_w_in(w_in[l]),
            mla_q_norm_w[l].reshape(1, Q_RANK), _prep_q_up(mla_q_up[l]),
            mla_kv_norm_w[l].reshape(1, KV_RANK), _prep_kv_up(mla_kv_up[l]), tab,
            gdn_conv_w[l], _lane_row(gdn_a_log[l]), _lane_row(gdn_dt_bias[l]))
        o_mla = _attn_call(q, k, v)
        o_gdn = _gdn_call(gq, gk, gv, gb)
        x = _out_call(x, mod_l, o_mla, z_mla, o_gdn, z_gdn, gdn_o_norm_w[l].reshape(1, DV),
                      w_out[l].astype(_BF16), post_norm_w[l].reshape(1, d))
    return x
```

```python
import functools

import jax
import jax.numpy as jnp
import numpy as np
from jax import lax
from jax.experimental import pallas as pl
from jax.experimental.pallas import tpu as pltpu

HEADS = 4
NOPE = 128
ROPE = 64
DQK = NOPE + ROPE
DV = 128
Q_RANK = 384
KV_RANK = 256
WIDTH = HEADS * DV
CONV_K = 4
ROPE_THETA = 10000.0
EPS = 1e-6

LANES = 128

C_QLAT = 0
C_KVLAT = C_QLAT + Q_RANK
C_KPE = C_KVLAT + KV_RANK
C_ZMLA = C_KPE + LANES
C_QKV = C_ZMLA + WIDTH
C_AB = C_QKV + 3 * WIDTH
C_ZGDN = C_AB + LANES
N_COLS = C_ZGDN + WIDTH

TOKEN_TILE = 512
ATTN_Q = 1024
ATTN_SUB = 256
VT_ROWS = DV + 16
GDN_CHUNK = 256
GDN_TILE = 512
ROPE_TILE = 1024
VMEM_LIMIT = 56 * 1024 * 1024

_F32 = jnp.float32
_BF16 = jnp.bfloat16
_NEG = -1e30
LOG2E = float(np.log2(np.e))


def _sigmoid(x):
    return 1.0 / (1.0 + jnp.exp(-x))


def _silu(x):
    return x * _sigmoid(x)


def _softplus(x):
    return jnp.maximum(x, 0.0) + jnp.log(1.0 + jnp.exp(-jnp.abs(x)))


def _rms(x, w):
    return x * lax.rsqrt(jnp.mean(x * x, axis=-1, keepdims=True) + EPS) * w


def _dot(a, b):
    return jnp.dot(a.astype(_BF16), b.astype(_BF16), preferred_element_type=_F32)


def _dot_nt(a, b):
    return lax.dot_general(a.astype(_BF16), b.astype(_BF16), (((1,), (1,)), ((), ())),
                           preferred_element_type=_F32)


def _dot_tn(a, b):
    return lax.dot_general(a.astype(_BF16), b.astype(_BF16), (((0,), (0,)), ((), ())),
                           preferred_element_type=_F32)


def _mod_kernel(c_ref, w_ref, b_ref, o_ref):
    c_act = _silu(c_ref[...])
    o_ref[0] = jnp.dot(c_act, w_ref[0], preferred_element_type=_F32,
                       precision=lax.Precision.HIGHEST) + b_ref[0]


def _modulation(c, w_mod, b_mod):
    depth, d, d3 = w_mod.shape
    b = c.shape[0]
    return pl.pallas_call(
        _mod_kernel,
        out_shape=jax.ShapeDtypeStruct((depth, b, d3), _F32),
        grid=(depth, d3 // d),
        in_specs=[
            pl.BlockSpec((b, d), lambda l, j: (0, 0)),
            pl.BlockSpec((1, d, d), lambda l, j: (l, 0, j)),
            pl.BlockSpec((1, 1, d), lambda l, j: (l, 0, j)),
        ],
        out_specs=pl.BlockSpec((1, b, d), lambda l, j: (l, 0, j)),
        compiler_params=pltpu.CompilerParams(dimension_semantics=("arbitrary", "arbitrary")),
        name="modulation",
    )(c, w_mod, b_mod.reshape(depth, 1, d3))


def _rope_kernel(pos_ref, freq_ref, o_ref):
    ang = pos_ref[0].astype(_F32) * freq_ref[...]
    lane = lax.broadcasted_iota(jnp.int32, ang.shape, 1)
    o_ref[0] = jnp.where(lane < ROPE, jnp.cos(ang), jnp.sin(ang))


def _rope_table(positions):
    b, s = positions.shape
    half = ROPE // 2
    inv_freq = jnp.power(ROPE_THETA, -jnp.arange(half, dtype=_F32) * 2.0 / ROPE)
    freq = jnp.tile(inv_freq, LANES // half).reshape(1, LANES)
    t = min(ROPE_TILE, s)
    return pl.pallas_call(
        _rope_kernel,
        out_shape=jax.ShapeDtypeStruct((b, s, LANES), _F32),
        grid=(b, s // t),
        in_specs=[
            pl.BlockSpec((1, t, 1), lambda i, j: (i, j, 0)),
            pl.BlockSpec((1, LANES), lambda i, j: (0, 0)),
        ],
        out_specs=pl.BlockSpec((1, t, LANES), lambda i, j: (i, j, 0)),
        compiler_params=pltpu.CompilerParams(dimension_semantics=("arbitrary", "arbitrary")),
        name="rope_table",
    )(positions.reshape(b, s, 1), freq)


def _in_kernel(x_ref, mod_ref, prew_ref, win_ref, qnw_ref, qup_ref, kvnw_ref, kvup_ref, tab_ref,
               convw_ref, alog_ref, dtb_ref,
               qt_ref, k_ref, vt_ref, zm_ref, gq_ref, gk_ref, gv_ref, gb_ref, zg_ref,
               tail_ref):
    d = x_ref.shape[-1]
    tm = x_ref.shape[1]

    @pl.when(pl.program_id(1) == 0)
    def _():
        tail_ref[...] = jnp.zeros_like(tail_ref)

    x = x_ref[0]
    mod = mod_ref[0]
    h = _rms(x, prew_ref[...]) * (1.0 + mod[:, d:2 * d]) + mod[:, 0:d]
    proj = _dot(h, win_ref[...])

    tab = tab_ref[0]
    q_scale = DQK ** -0.5 * LOG2E

    def rope(block):
        y = block * tab
        return y + pltpu.roll(y, ROPE, 1)

    qf = _dot(_rms(proj[:, C_QLAT:C_QLAT + Q_RANK], qnw_ref[...]), qup_ref[...])
    kvf = _dot(_rms(proj[:, C_KVLAT:C_KVLAT + KV_RANK], kvnw_ref[...]), kvup_ref[...])
    k_pe = rope(proj[:, C_KPE:C_KPE + LANES])[:, :ROPE].astype(k_ref.dtype)
    for hd in range(HEADS):
        lo = hd * NOPE
        q_pe = rope(qf[:, WIDTH + hd * LANES:WIDTH + (hd + 1) * LANES])
        qt_ref[0, hd, 0:NOPE, :] = (qf[:, lo:lo + NOPE] * q_scale).T.astype(qt_ref.dtype)
        qt_ref[0, hd, NOPE:DQK, :] = (q_pe * q_scale).T[:ROPE, :].astype(qt_ref.dtype)
        k_ref[0, hd, :, 0:NOPE] = kvf[:, lo:lo + NOPE].astype(k_ref.dtype)
        k_ref[0, hd, :, NOPE:DQK] = k_pe
        vt_ref[0, hd, 0, 0:DV, :] = kvf[:, WIDTH + lo:WIDTH + lo + DV].T.astype(vt_ref.dtype)
        vt_ref[0, hd, 0, DV:, :] = jnp.ones((VT_ROWS - DV, tm), vt_ref.dtype)
    zm_ref[0] = proj[:, C_ZMLA:C_ZMLA + WIDTH].astype(zm_ref.dtype)

    qkv = proj[:, C_QKV:C_QKV + 3 * WIDTH]
    ext = jnp.concatenate([tail_ref[...], qkv], axis=0)
    cw = convw_ref[...]
    conv = qkv * cw[CONV_K - 1:CONV_K, :]
    for j in range(CONV_K - 1):
        shift = CONV_K - 1 - j
        conv = conv + ext[8 - shift:8 - shift + tm, :] * cw[j:j + 1, :]
    tail_ref[...] = qkv[tm - 8:tm, :]
    act = _silu(conv)
    for hd in range(HEADS):
        lo = hd * DV
        qh = act[:, lo:lo + DV]
        kh = act[:, WIDTH + lo:WIDTH + lo + DV]
        qn = qh * lax.rsqrt(jnp.sum(qh * qh, axis=-1, keepdims=True) + EPS) * (DV ** -0.5)
        kn = kh * lax.rsqrt(jnp.sum(kh * kh, axis=-1, keepdims=True) + EPS)
        gq_ref[0, :, lo:lo + DV] = qn.astype(gq_ref.dtype)
        gk_ref[0, :, lo:lo + DV] = kn.astype(gk_ref.dtype)
    gv_ref[0] = act[:, 2 * WIDTH:3 * WIDTH].astype(gv_ref.dtype)

    ab = proj[:, C_AB:C_AB + LANES]
    g = -jnp.exp(alog_ref[...]) * _softplus(ab + dtb_ref[...])
    lane = lax.broadcasted_iota(jnp.int32, ab.shape, 1)
    gb_ref[0] = jnp.where(lane < HEADS, g, _sigmoid(ab))
    zg_ref[0] = proj[:, C_ZGDN:C_ZGDN + WIDTH].astype(zg_ref.dtype)


def _in_call(x, mod_l, prew, win, qnw, qup, kvnw, kvup, tab, convw, alog, dtb):
    b, s, d = x.shape
    tm = min(TOKEN_TILE, s)
    const = lambda shape: pl.BlockSpec(shape, lambda i, j: (0,) * len(shape))
    tok = lambda w: pl.BlockSpec((1, tm, w), lambda i, j: (i, j, 0))
    head = pl.BlockSpec((1, HEADS, tm, DQK), lambda i, j: (i, 0, j, 0))
    head_t = pl.BlockSpec((1, HEADS, DQK, tm), lambda i, j: (i, 0, 0, j))
    head_vt = pl.BlockSpec((1, HEADS, 1, VT_ROWS, tm), lambda i, j: (i, 0, j, 0, 0))
    out_shape = (
        jax.ShapeDtypeStruct((b, HEADS, DQK, s), _BF16),
        jax.ShapeDtypeStruct((b, HEADS, s, DQK), _BF16),
        jax.ShapeDtypeStruct((b, HEADS, s // tm, VT_ROWS, tm), _BF16),
        jax.ShapeDtypeStruct((b, s, WIDTH), _BF16),
        jax.ShapeDtypeStruct((b, s, WIDTH), _BF16),
        jax.ShapeDtypeStruct((b, s, WIDTH), _BF16),
        jax.ShapeDtypeStruct((b, s, WIDTH), _BF16),
        jax.ShapeDtypeStruct((b, s, LANES), _F32),
        jax.ShapeDtypeStruct((b, s, WIDTH), _BF16),
    )
    return pl.pallas_call(
        _in_kernel,
        out_shape=out_shape,
        grid=(b, s // tm),
        in_specs=[
            tok(d),
            pl.BlockSpec((1, 1, 3 * d), lambda i, j: (i, 0, 0)),
            const((1, d)),
            const((d, N_COLS)),
            const((1, Q_RANK)),
            const((Q_RANK, 2 * WIDTH)),
            const((1, KV_RANK)),
            const((KV_RANK, 2 * WIDTH)),
            tok(LANES),
            const((CONV_K, 3 * WIDTH)),
            const((1, LANES)),
            const((1, LANES)),
        ],
        out_specs=(head_t, head, head_vt, tok(WIDTH), tok(WIDTH), tok(WIDTH), tok(WIDTH),
                   tok(LANES), tok(WIDTH)),
        scratch_shapes=[pltpu.VMEM((8, 3 * WIDTH), _F32)],
        compiler_params=pltpu.CompilerParams(dimension_semantics=("arbitrary", "arbitrary"),
                                             vmem_limit_bytes=VMEM_LIMIT),
        name="in_proj",
    )(x, mod_l, prew, win, qnw, qup, kvnw, kvup, tab, convw, alog, dtb)


def _attn_kernel(qt_ref, k_ref, vt_ref, o_ref, sa_ref, sb_ref, mxa_ref, mxb_ref, m_ref, acc_ref):
    tq = qt_ref.shape[3]
    uk = vt_ref.shape[4]
    uq = ATTN_SUB
    nq = tq // uq
    nu = tq // uk
    i = pl.program_id(2)
    qts = [qt_ref[0, 0, :, h * uq:(h + 1) * uq] for h in range(nq)]
    bufs = ((sa_ref, mxa_ref), (sb_ref, mxb_ref))

    m_ref[...] = jnp.full_like(m_ref, _NEG)
    acc_ref[...] = jnp.zeros_like(acc_ref)

    def produce(buf, blk, unit, subs):
        s_ref, mx_ref = bufs[buf]
        kj = k_ref[0, 0, pl.ds(pl.multiple_of(blk * tq + unit * uk, uk), uk), :]
        for h in subs:
            s = jnp.dot(kj, qts[h], preferred_element_type=_F32)
            s_ref[h] = s
            mx_ref[h] = jnp.max(s, axis=0, keepdims=True)

    def consume(buf, blk, unit, subs, masked=()):
        s_ref, mx_ref = bufs[buf]
        vt = vt_ref[0, 0, blk * nu + unit]
        for h in subs:
            s = s_ref[h]
            if h in masked:
                key = lax.broadcasted_iota(jnp.int32, s.shape, 0) + unit * uk
                qry = lax.broadcasted_iota(jnp.int32, s.shape, 1) + h * uq
                s = jnp.where(key <= qry, s, _NEG)
                mx = jnp.max(s, axis=0, keepdims=True)
            else:
                mx = mx_ref[h]
            m = m_ref[h]
            m_new = jnp.maximum(m, mx)
            p = jnp.exp2(s - m_new).astype(_BF16)
            acc_ref[h] = jnp.exp2(m - m_new) * acc_ref[h] + jnp.dot(vt, p, preferred_element_type=_F32)
            m_ref[h] = m_new

    every = tuple(range(nq))
    produce(0, 0, 0, every)

    def body(j, carry):
        for unit in range(nu):
            if unit + 1 < nu:
                produce((unit + 1) % 2, j, unit + 1, every)
            else:
                produce((unit + 1) % 2, j + 1, 0, every)
            consume(unit % 2, j, unit, every)
        return carry

    lax.fori_loop(0, i, body, 0)
    per = uk // uq
    for unit in range(nu):
        if unit + 1 < nu:
            produce((unit + 1) % 2, i, unit + 1, tuple(range((unit + 1) * per, nq)))
        consume(unit % 2, i, unit, tuple(range(unit * per, nq)),
                masked=tuple(range(unit * per, (unit + 1) * per)))
    for h in range(nq):
        acc = acc_ref[h]
        o_ref[0, h * uq:(h + 1) * uq, :] = (acc[:DV] / acc[DV:DV + 1]).T.astype(o_ref.dtype)


def _attn_call(qt, k, vt):
    b, _, _, s = qt.shape
    vrows, tv = vt.shape[-2:]
    t = ATTN_Q
    nq = t // ATTN_SUB
    return pl.pallas_call(
        _attn_kernel,
        out_shape=jax.ShapeDtypeStruct((b, s, WIDTH), _BF16),
        grid=(b, HEADS, s // t),
        in_specs=[
            pl.BlockSpec((1, 1, DQK, t), lambda bi, h, i: (bi, h, 0, i)),
            pl.BlockSpec((1, 1, s, DQK), lambda bi, h, i: (bi, h, 0, 0)),
            pl.BlockSpec((1, 1, s // tv, vrows, tv), lambda bi, h, i: (bi, h, 0, 0, 0)),
        ],
        out_specs=pl.BlockSpec((1, t, DV), lambda bi, h, i: (bi, i, h)),
        scratch_shapes=[
            pltpu.VMEM((nq, tv, ATTN_SUB), _F32),
            pltpu.VMEM((nq, tv, ATTN_SUB), _F32),
            pltpu.VMEM((nq, 1, ATTN_SUB), _F32),
            pltpu.VMEM((nq, 1, ATTN_SUB), _F32),
            pltpu.VMEM((nq, 1, ATTN_SUB), _F32),
            pltpu.VMEM((nq, vrows, ATTN_SUB), _F32),
        ],
        compiler_params=pltpu.CompilerParams(
            dimension_semantics=("arbitrary", "arbitrary", "arbitrary"), vmem_limit_bytes=VMEM_LIMIT),
        name="mla_attention",
    )(qt, k, vt)


def _cumsum_rows(tril_bf16, x):
    lane = lax.broadcasted_iota(jnp.int32, x.shape, 1)
    hi = x.astype(_BF16).astype(_F32)
    r1 = x - hi
    mid = r1.astype(_BF16).astype(_F32)
    low = r1 - mid
    packed = jnp.where(lane < 8, hi, jnp.where(lane < 16, pltpu.roll(mid, 8, 1), pltpu.roll(low, 16, 1)))
    res = jnp.dot(tril_bf16, packed.astype(_BF16), preferred_element_type=_F32)
    return res + pltpu.roll(res, LANES - 8, 1) + pltpu.roll(res, LANES - 16, 1)


def _gdn_kernel(q_ref, k_ref, v_ref, gb_ref, o_ref, state_ref, gc_ref, n_ref, t_ref, qk_ref, rhs_ref,
                qe_ref, kd_ref, sol_ref, ou_ref, qw_ref, su_ref, sw_ref):
    c = GDN_CHUNK
    n_chunks = q_ref.shape[1] // c
    n_chains = n_chunks * HEADS

    @pl.when(pl.program_id(1) == 0)
    def _():
        state_ref[...] = jnp.zeros_like(state_ref)

    row = lax.broadcasted_iota(jnp.int32, (c, c), 0)
    col = lax.broadcasted_iota(jnp.int32, (c, c), 1)
    incl = col <= row
    xor = row ^ col
    tril = incl.astype(_BF16)

    for ci in range(n_chunks):
        rows = slice(ci * c, (ci + 1) * c)
        gb = gb_ref[0, rows, :]
        gcum = _cumsum_rows(tril, gb)
        gc_ref[ci] = gcum
        gcum_t = gcum.T
        for hd in range(HEADS):
            ch = ci * HEADS + hd
            lo = hd * DV
            q = q_ref[0, rows, lo:lo + DV].astype(_F32)
            k = k_ref[0, rows, lo:lo + DV].astype(_F32)
            v = v_ref[0, rows, lo:lo + DV].astype(_F32)
            beta = jnp.broadcast_to(gb[:, HEADS + hd:HEADS + hd + 1], (c, DV))
            g_col = jnp.broadcast_to(gcum[:, hd:hd + 1], (c, DV))
            g_row = gcum_t[hd:hd + 1, :]
            g_last = gcum[c - 1:c, hd:hd + 1]
            decay = jnp.exp(jnp.where(incl, jnp.concatenate([g_col, g_col], axis=1) - g_row, _NEG))
            e_col = jnp.exp(g_col)
            kb = k * beta
            n_mat = _dot_nt(kb, k) * decay
            n_ref[ch] = n_mat
            t_ref[ch] = jnp.where(xor == 0, 1.0, jnp.where(xor == 1, -n_mat, 0.0))
            qk_ref[ch] = (_dot_nt(q, k) * decay).astype(qk_ref.dtype)
            rhs_ref[ch] = jnp.concatenate([v * beta, kb * e_col], axis=1).astype(rhs_ref.dtype)
            qe_ref[ch] = (q * e_col).astype(qe_ref.dtype)
            kd_ref[ch] = (k * jnp.exp(g_last - g_col)).astype(kd_ref.dtype)

    def level(lvl, carry):
        m = jnp.left_shift(2, lvl)
        blk = lax.broadcasted_iota(jnp.int32, (c, c), 0) ^ lax.broadcasted_iota(jnp.int32, (c, c), 1)
        mask = (blk >= m) & (blk < 2 * m)
        left = []
        for ch in range(n_chains):
            left.append(_dot(t_ref[ch], jnp.where(mask, n_ref[ch], 0.0)))
        for ch in range(n_chains):
            t_inv = t_ref[ch]
            t_ref[ch] = t_inv - _dot(left[ch], t_inv)
        return carry

    lax.fori_loop(0, int(np.log2(c)) - 1, level, 0)

    for ch in range(n_chains):
        sol_ref[ch] = _dot(t_ref[ch], rhs_ref[ch]).astype(sol_ref.dtype)
    for ch in range(n_chains):
        qk_sol = _dot(qk_ref[ch], sol_ref[ch])
        ou_ref[ch] = qk_sol[:, :DV]
        qw_ref[ch] = (qe_ref[ch].astype(_F32) - qk_sol[:, DV:]).astype(qw_ref.dtype)
    for ch in range(n_chains):
        kd_sol = _dot_tn(kd_ref[ch], sol_ref[ch])
        su_ref[ch] = kd_sol[:, :DV]
        sw_ref[ch] = kd_sol[:, DV:].astype(sw_ref.dtype)
    for ci in range(n_chunks):
        rows = slice(ci * c, (ci + 1) * c)
        for hd in range(HEADS):
            ch = ci * HEADS + hd
            lo = hd * DV
            g_last = gc_ref[ci, c - 1:c, hd:hd + 1]
            state = state_ref[hd]
            o_ref[0, rows, lo:lo + DV] = (_dot(qw_ref[ch], state) + ou_ref[ch]).astype(o_ref.dtype)
            state_ref[hd] = state * jnp.exp(g_last) + su_ref[ch] - _dot(sw_ref[ch], state)


def _gdn_call(gq, gk, gv, gb):
    b, s, _ = gq.shape
    t = GDN_TILE
    n_chains = (t // GDN_CHUNK) * HEADS
    tok = lambda w: pl.BlockSpec((1, t, w), lambda i, j: (i, j, 0))
    return pl.pallas_call(
        _gdn_kernel,
        out_shape=jax.ShapeDtypeStruct((b, s, WIDTH), _BF16),
        grid=(b, s // t),
        in_specs=[tok(WIDTH), tok(WIDTH), tok(WIDTH), tok(LANES)],
        out_specs=tok(WIDTH),
        scratch_shapes=[
            pltpu.VMEM((HEADS, DV, DV), _F32),
            pltpu.VMEM((t // GDN_CHUNK, GDN_CHUNK, LANES), _F32),
            pltpu.VMEM((n_chains, GDN_CHUNK, GDN_CHUNK), _F32),
            pltpu.VMEM((n_chains, GDN_CHUNK, GDN_CHUNK), _F32),
            pltpu.VMEM((n_chains, GDN_CHUNK, GDN_CHUNK), _BF16),
            pltpu.VMEM((n_chains, GDN_CHUNK, 2 * DV), _BF16),
            pltpu.VMEM((n_chains, GDN_CHUNK, DV), _BF16),
            pltpu.VMEM((n_chains, GDN_CHUNK, DV), _BF16),
            pltpu.VMEM((n_chains, GDN_CHUNK, 2 * DV), _BF16),
            pltpu.VMEM((n_chains, GDN_CHUNK, DV), _F32),
            pltpu.VMEM((n_chains, GDN_CHUNK, DV), _BF16),
            pltpu.VMEM((n_chains, DV, DV), _F32),
            pltpu.VMEM((n_chains, DV, DV), _BF16),
        ],
        compiler_params=pltpu.CompilerParams(dimension_semantics=("arbitrary", "arbitrary"),
                                             vmem_limit_bytes=VMEM_LIMIT),
        name="gated_delta_rule",
    )(gq, gk, gv, gb)


def _out_kernel(x_ref, mod_ref, om_ref, zm_ref, og_ref, zg_ref, onw_ref, wout_ref, postw_ref, o_ref):
    d = x_ref.shape[-1]
    y_mla = om_ref[0].astype(_F32) * _silu(zm_ref[0].astype(_F32))
    og = og_ref[0].astype(_F32)
    zg = _silu(zg_ref[0].astype(_F32))
    parts = [y_mla]
    for hd in range(HEADS):
        lo = hd * DV
        parts.append(_rms(og[:, lo:lo + DV], onw_ref[...]) * zg[:, lo:lo + DV])
    y = _dot(jnp.concatenate(parts, axis=1), wout_ref[...])
    gate = mod_ref[0][:, 2 * d:3 * d]
    o_ref[0] = x_ref[0] + gate * _rms(y, postw_ref[...])


def _out_call(x, mod_l, o_mla, z_mla, o_gdn, z_gdn, onw, wout, postw):
    b, s, d = x.shape
    tm = min(TOKEN_TILE, s)
    const = lambda shape: pl.BlockSpec(shape, lambda i, j: (0,) * len(shape))
    tok = lambda w: pl.BlockSpec((1, tm, w), lambda i, j: (i, j, 0))
    return pl.pallas_call(
        _out_kernel,
        out_shape=jax.ShapeDtypeStruct((b, s, d), _F32),
        grid=(b, s // tm),
        in_specs=[
            tok(d),
            pl.BlockSpec((1, 1, 3 * d), lambda i, j: (i, 0, 0)),
            tok(WIDTH), tok(WIDTH), tok(WIDTH), tok(WIDTH),
            const((1, DV)),
            const((2 * WIDTH, d)),
            const((1, d)),
        ],
        out_specs=tok(d),
        compiler_params=pltpu.CompilerParams(dimension_semantics=("arbitrary", "arbitrary"),
                                             vmem_limit_bytes=VMEM_LIMIT),
        name="out_proj",
    )(x, mod_l, o_mla, z_mla, o_gdn, z_gdn, onw, wout, postw)


def _rot_cols(w):
    half = w.shape[-1] // 2
    return jnp.concatenate([-w[:, half:], w[:, :half]], axis=-1)


def _pad_cols(w, n):
    return jnp.pad(w, ((0, 0), (0, n - w.shape[-1])))


def _prep_w_in(w):
    o_kv = Q_RANK
    o_kpe = o_kv + KV_RANK
    o_zm = o_kpe + ROPE
    o_qkv = o_zm + WIDTH
    o_a = o_qkv + 3 * WIDTH
    o_zg = o_a + 2 * HEADS
    k_pe = w[:, o_kpe:o_zm]
    return jnp.concatenate([
        w[:, :o_kpe], k_pe, _rot_cols(k_pe), w[:, o_zm:o_qkv], w[:, o_qkv:o_a],
        _pad_cols(w[:, o_a:o_zg], LANES), w[:, o_zg:],
    ], axis=1).astype(_BF16)


def _prep_q_up(w):
    per = w.reshape(Q_RANK, HEADS, DQK)
    nope = per[:, :, :NOPE].reshape(Q_RANK, HEADS * NOPE)
    pe = [jnp.concatenate([per[:, h, NOPE:], _rot_cols(per[:, h, NOPE:])], axis=1) for h in range(HEADS)]
    return jnp.concatenate([nope] + pe, axis=1).astype(_BF16)


def _prep_kv_up(w):
    per = w.reshape(KV_RANK, HEADS, NOPE + DV)
    return jnp.concatenate([per[:, :, :NOPE].reshape(KV_RANK, HEADS * NOPE),
                            per[:, :, NOPE:].reshape(KV_RANK, HEADS * DV)], axis=1).astype(_BF16)


def _lane_row(vals):
    return _pad_cols(vals.reshape(1, -1).astype(_F32), LANES)


def kernel(x, c, positions, w_mod, b_mod, pre_norm_w, post_norm_w, w_in, mla_q_norm_w, mla_q_up,
           mla_kv_norm_w, mla_kv_up, gdn_conv_w, gdn_a_log, gdn_dt_bias, gdn_o_norm_w, w_out):
    depth = w_mod.shape[0]
    b, s, d = x.shape
    assert s % TOKEN_TILE == 0 and s % GDN_TILE == 0 and s % ATTN_Q == 0
    assert TOKEN_TILE % ATTN_SUB == 0 and ATTN_Q % TOKEN_TILE == 0 and (ATTN_Q // ATTN_SUB) % 2 == 0
    mod = _modulation(c, w_mod, b_mod)
    tab = _rope_table(positions)
    for l in range(depth):
        mod_l = mod[l].reshape(b, 1, 3 * d)
        q, k, v, z_mla, gq, gk, gv, gb, z_gdn = _in_call(
            x, mod_l, pre_norm_w[l].reshape(1, d), _prep_w_in(w_in[l]),
            mla_q_norm_w[l].reshape(1, Q_RANK), _prep_q_up(mla_q_up[l]),
            mla_kv_norm_w[l].reshape(1, KV_RANK), _prep_kv_up(mla_kv_up[l]), tab,
            gdn_conv_w[l], _lane_row(gdn_a_log[l]), _lane_row(gdn_dt_bias[l]))
        o_mla = _attn_call(q, k, v)
        o_gdn = _gdn_call(gq, gk, gv, gb)
        x = _out_call(x, mod_l, o_mla, z_mla, o_gdn, z_gdn, gdn_o_norm_w[l].reshape(1, DV),
                      w_out[l].astype(_BF16), post_norm_w[l].reshape(1, d))
    return x
```

```python
import functools

import jax
import jax.numpy as jnp
import numpy as np
from jax import lax
from jax.experimental import pallas as pl
from jax.experimental.pallas import tpu as pltpu

HEADS = 4
NOPE = 128
ROPE = 64
DQK = NOPE + ROPE
DV = 128
Q_RANK = 384
KV_RANK = 256
WIDTH = HEADS * DV
CONV_K = 4
ROPE_THETA = 10000.0
EPS = 1e-6

LANES = 128

C_QLAT = 0
C_KVLAT = C_QLAT + Q_RANK
C_KPE = C_KVLAT + KV_RANK
C_ZMLA = C_KPE + LANES
C_QKV = C_ZMLA + WIDTH
C_AB = C_QKV + 3 * WIDTH
C_ZGDN = C_AB + LANES
N_COLS = C_ZGDN + WIDTH

TOKEN_TILE = 512
ATTN_Q = 1024
ATTN_SUB = 256
VT_ROWS = DV + 16
GDN_CHUNK = 256
GDN_TILE = 512
ROPE_TILE = 1024
VMEM_LIMIT = 56 * 1024 * 1024

_F32 = jnp.float32
_BF16 = jnp.bfloat16
_NEG = -1e30
LOG2E = float(np.log2(np.e))


def _sigmoid(x):
    return 1.0 / (1.0 + jnp.exp2(x * -LOG2E))


def _silu(x):
    return x * _sigmoid(x)


def _softplus(x):
    return jnp.maximum(x, 0.0) + jnp.log(1.0 + jnp.exp(-jnp.abs(x)))


def _rms(x, w):
    return x * lax.rsqrt(jnp.mean(x * x, axis=-1, keepdims=True) + EPS) * w


def _dot(a, b):
    return jnp.dot(a.astype(_BF16), b.astype(_BF16), preferred_element_type=_F32)


def _dot_nt(a, b):
    return lax.dot_general(a.astype(_BF16), b.astype(_BF16), (((1,), (1,)), ((), ())),
                           preferred_element_type=_F32)


def _dot_tn(a, b):
    return lax.dot_general(a.astype(_BF16), b.astype(_BF16), (((0,), (0,)), ((), ())),
                           preferred_element_type=_F32)


def _mod_kernel(c_ref, w_ref, b_ref, o_ref):
    c_act = _silu(c_ref[...])
    o_ref[0] = jnp.dot(c_act, w_ref[0], preferred_element_type=_F32,
                       precision=lax.Precision.HIGHEST) + b_ref[0]


def _modulation(c, w_mod, b_mod):
    depth, d, d3 = w_mod.shape
    b = c.shape[0]
    return pl.pallas_call(
        _mod_kernel,
        out_shape=jax.ShapeDtypeStruct((depth, b, d3), _F32),
        grid=(depth, d3 // d),
        in_specs=[
            pl.BlockSpec((b, d), lambda l, j: (0, 0)),
            pl.BlockSpec((1, d, d), lambda l, j: (l, 0, j)),
            pl.BlockSpec((1, 1, d), lambda l, j: (l, 0, j)),
        ],
        out_specs=pl.BlockSpec((1, b, d), lambda l, j: (l, 0, j)),
        compiler_params=pltpu.CompilerParams(dimension_semantics=("arbitrary", "arbitrary")),
        name="modulation",
    )(c, w_mod, b_mod.reshape(depth, 1, d3))


def _rope_kernel(pos_ref, freq_ref, o_ref):
    ang = pos_ref[0].astype(_F32) * freq_ref[...]
    lane = lax.broadcasted_iota(jnp.int32, ang.shape, 1)
    o_ref[0] = jnp.where(lane < ROPE, jnp.cos(ang), jnp.sin(ang))


def _rope_table(positions):
    b, s = positions.shape
    half = ROPE // 2
    inv_freq = jnp.power(ROPE_THETA, -jnp.arange(half, dtype=_F32) * 2.0 / ROPE)
    freq = jnp.tile(inv_freq, LANES // half).reshape(1, LANES)
    t = min(ROPE_TILE, s)
    return pl.pallas_call(
        _rope_kernel,
        out_shape=jax.ShapeDtypeStruct((b, s, LANES), _F32),
        grid=(b, s // t),
        in_specs=[
            pl.BlockSpec((1, t, 1), lambda i, j: (i, j, 0)),
            pl.BlockSpec((1, LANES), lambda i, j: (0, 0)),
        ],
        out_specs=pl.BlockSpec((1, t, LANES), lambda i, j: (i, j, 0)),
        compiler_params=pltpu.CompilerParams(dimension_semantics=("arbitrary", "arbitrary")),
        name="rope_table",
    )(positions.reshape(b, s, 1), freq)


def _in_kernel(x_ref, mod_ref, prew_ref, win_ref, qnw_ref, qup_ref, kvnw_ref, kvup_ref, tab_ref,
               convw_ref, alog_ref, dtb_ref,
               qt_ref, k_ref, vt_ref, zm_ref, gq_ref, gk_ref, gv_ref, gb_ref, zg_ref,
               tail_ref):
    d = x_ref.shape[-1]
    tm = x_ref.shape[1]

    @pl.when(pl.program_id(1) == 0)
    def _():
        tail_ref[tm:tm + 8, :] = jnp.zeros((8, tail_ref.shape[1]), tail_ref.dtype)

    x = x_ref[0]
    mod = mod_ref[0]
    w_eff = prew_ref[...] * (1.0 + mod[:, d:2 * d])
    h = x * lax.rsqrt(jnp.mean(x * x, axis=-1, keepdims=True) + EPS) * w_eff + mod[:, 0:d]
    proj = _dot(h, win_ref[...])

    tab = tab_ref[0]
    q_scale = DQK ** -0.5 * LOG2E

    def rope(block):
        y = block * tab
        return y + pltpu.roll(y, ROPE, 1)

    qf = _dot(_rms(proj[:, C_QLAT:C_QLAT + Q_RANK], qnw_ref[...]), qup_ref[...])
    kvf = _dot(_rms(proj[:, C_KVLAT:C_KVLAT + KV_RANK], kvnw_ref[...]), kvup_ref[...])
    k_pe = rope(proj[:, C_KPE:C_KPE + LANES])[:, :ROPE].astype(k_ref.dtype)
    for hd in range(HEADS):
        lo = hd * NOPE
        q_pe = rope(qf[:, WIDTH + hd * LANES:WIDTH + (hd + 1) * LANES])
        qt_ref[0, hd, 0:NOPE, :] = (qf[:, lo:lo + NOPE] * q_scale).T.astype(qt_ref.dtype)
        qt_ref[0, hd, NOPE:DQK, :] = (q_pe * q_scale).T[:ROPE, :].astype(qt_ref.dtype)
        k_ref[0, hd, :, 0:NOPE] = kvf[:, lo:lo + NOPE].astype(k_ref.dtype)
        k_ref[0, hd, :, NOPE:DQK] = k_pe
        vt_ref[0, hd, 0, 0:DV, :] = kvf[:, WIDTH + lo:WIDTH + lo + DV].T.astype(vt_ref.dtype)
        vt_ref[0, hd, 0, DV:, :] = jnp.ones((VT_ROWS - DV, tm), vt_ref.dtype)
    zm_ref[0] = proj[:, C_ZMLA:C_ZMLA + WIDTH].astype(zm_ref.dtype)

    qkv = proj[:, C_QKV:C_QKV + 3 * WIDTH]
    tail_ref[0:8, :] = tail_ref[tm:tm + 8, :]
    tail_ref[8:8 + tm, :] = qkv
    cw = convw_ref[...]
    conv = qkv * cw[CONV_K - 1:CONV_K, :]
    for j in range(CONV_K - 1):
        start = 8 - (CONV_K - 1 - j)
        conv = conv + tail_ref[start:start + tm, :] * cw[j:j + 1, :]
    act = _silu(conv)
    for hd in range(HEADS):
        lo = hd * DV
        qh = act[:, lo:lo + DV]
        kh = act[:, WIDTH + lo:WIDTH + lo + DV]
        qn = qh * lax.rsqrt(jnp.sum(qh * qh, axis=-1, keepdims=True) + EPS) * (DV ** -0.5)
        kn = kh * lax.rsqrt(jnp.sum(kh * kh, axis=-1, keepdims=True) + EPS)
        gq_ref[0, :, lo:lo + DV] = qn.astype(gq_ref.dtype)
        gk_ref[0, :, lo:lo + DV] = kn.astype(gk_ref.dtype)
    gv_ref[0] = act[:, 2 * WIDTH:3 * WIDTH].astype(gv_ref.dtype)

    ab = proj[:, C_AB:C_AB + LANES]
    g = -jnp.exp(alog_ref[...]) * _softplus(ab + dtb_ref[...])
    lane = lax.broadcasted_iota(jnp.int32, ab.shape, 1)
    gb_ref[0] = jnp.where(lane < HEADS, g, _sigmoid(ab))
    zg_ref[0] = proj[:, C_ZGDN:C_ZGDN + WIDTH].astype(zg_ref.dtype)


def _in_call(x, mod_l, prew, win, qnw, qup, kvnw, kvup, tab, convw, alog, dtb):
    b, s, d = x.shape
    tm = min(TOKEN_TILE, s)
    const = lambda shape: pl.BlockSpec(shape, lambda i, j: (0,) * len(shape))
    tok = lambda w: pl.BlockSpec((1, tm, w), lambda i, j: (i, j, 0))
    head = pl.BlockSpec((1, HEADS, tm, DQK), lambda i, j: (i, 0, j, 0))
    head_t = pl.BlockSpec((1, HEADS, DQK, tm), lambda i, j: (i, 0, 0, j))
    head_vt = pl.BlockSpec((1, HEADS, 1, VT_ROWS, tm), lambda i, j: (i, 0, j, 0, 0))
    out_shape = (
        jax.ShapeDtypeStruct((b, HEADS, DQK, s), _BF16),
        jax.ShapeDtypeStruct((b, HEADS, s, DQK), _BF16),
        jax.ShapeDtypeStruct((b, HEADS, s // tm, VT_ROWS, tm), _BF16),
        jax.ShapeDtypeStruct((b, s, WIDTH), _BF16),
        jax.ShapeDtypeStruct((b, s, WIDTH), _BF16),
        jax.ShapeDtypeStruct((b, s, WIDTH), _BF16),
        jax.ShapeDtypeStruct((b, s, WIDTH), _BF16),
        jax.ShapeDtypeStruct((b, s, LANES), _F32),
        jax.ShapeDtypeStruct((b, s, WIDTH), _BF16),
    )
    return pl.pallas_call(
        _in_kernel,
        out_shape=out_shape,
        grid=(b, s // tm),
        in_specs=[
            tok(d),
            pl.BlockSpec((1, 1, 3 * d), lambda i, j: (i, 0, 0)),
            const((1, d)),
            const((d, N_COLS)),
            const((1, Q_RANK)),
            const((Q_RANK, 2 * WIDTH)),
            const((1, KV_RANK)),
            const((KV_RANK, 2 * WIDTH)),
            tok(LANES),
            const((CONV_K, 3 * WIDTH)),
            const((1, LANES)),
            const((1, LANES)),
        ],
        out_specs=(head_t, head, head_vt, tok(WIDTH), tok(WIDTH), tok(WIDTH), tok(WIDTH),
                   tok(LANES), tok(WIDTH)),
        scratch_shapes=[pltpu.VMEM((tm + 8, 3 * WIDTH), _F32)],
        compiler_params=pltpu.CompilerParams(dimension_semantics=("arbitrary", "arbitrary"),
                                             vmem_limit_bytes=VMEM_LIMIT),
        name="in_proj",
    )(x, mod_l, prew, win, qnw, qup, kvnw, kvup, tab, convw, alog, dtb)


def _attn_kernel(qt_ref, k_ref, vt_ref, o_ref, sa_ref, sb_ref, mxa_ref, mxb_ref, m_ref, acc_ref):
    tq = qt_ref.shape[3]
    uk = vt_ref.shape[4]
    uq = ATTN_SUB
    nq = tq // uq
    nu = tq // uk
    i = pl.program_id(2)
    qts = [qt_ref[0, 0, :, h * uq:(h + 1) * uq] for h in range(nq)]
    bufs = ((sa_ref, mxa_ref), (sb_ref, mxb_ref))

    m_ref[...] = jnp.full_like(m_ref, _NEG)
    acc_ref[...] = jnp.zeros_like(acc_ref)

    def produce(buf, blk, unit, subs):
        s_ref, mx_ref = bufs[buf]
        kj = k_ref[0, 0, pl.ds(pl.multiple_of(blk * tq + unit * uk, uk), uk), :]
        for h in subs:
            s = jnp.dot(kj, qts[h], preferred_element_type=_F32)
            s_ref[h] = s
            mx_ref[h] = jnp.max(s, axis=0, keepdims=True)

    def consume(buf, blk, unit, subs, masked=()):
        s_ref, mx_ref = bufs[buf]
        vt = vt_ref[0, 0, blk * nu + unit]
        for h in subs:
            s = s_ref[h]
            if h in masked:
                key = lax.broadcasted_iota(jnp.int32, s.shape, 0) + unit * uk
                qry = lax.broadcasted_iota(jnp.int32, s.shape, 1) + h * uq
                s = jnp.where(key <= qry, s, _NEG)
                mx = jnp.max(s, axis=0, keepdims=True)
            else:
                mx = mx_ref[h]
            m = m_ref[h]
            m_new = jnp.maximum(m, mx)
            p = jnp.exp2(s - m_new).astype(_BF16)
            acc_ref[h] = jnp.exp2(m - m_new) * acc_ref[h] + jnp.dot(vt, p, preferred_element_type=_F32)
            m_ref[h] = m_new

    every = tuple(range(nq))
    produce(0, 0, 0, every)

    def body(j, carry):
        for unit in range(nu):
            if unit + 1 < nu:
                produce((unit + 1) % 2, j, unit + 1, every)
            else:
                produce((unit + 1) % 2, j + 1, 0, every)
            consume(unit % 2, j, unit, every)
        return carry

    lax.fori_loop(0, i, body, 0)
    per = uk // uq
    for unit in range(nu):
        if unit + 1 < nu:
            produce((unit + 1) % 2, i, unit + 1, tuple(range((unit + 1) * per, nq)))
        consume(unit % 2, i, unit, tuple(range(unit * per, nq)),
                masked=tuple(range(unit * per, (unit + 1) * per)))
    for h in range(nq):
        acc = acc_ref[h]
        o_ref[0, h * uq:(h + 1) * uq, :] = (acc[:DV] / acc[DV:DV + 1]).T.astype(o_ref.dtype)


def _attn_call(qt, k, vt):
    b, _, _, s = qt.shape
    vrows, tv = vt.shape[-2:]
    t = ATTN_Q
    nq = t // ATTN_SUB
    return pl.pallas_call(
        _attn_kernel,
        out_shape=jax.ShapeDtypeStruct((b, s, WIDTH), _BF16),
        grid=(b, HEADS, s // t),
        in_specs=[
            pl.BlockSpec((1, 1, DQK, t), lambda bi, h, i: (bi, h, 0, i)),
            pl.BlockSpec((1, 1, s, DQK), lambda bi, h, i: (bi, h, 0, 0)),
            pl.BlockSpec((1, 1, s // tv, vrows, tv), lambda bi, h, i: (bi, h, 0, 0, 0)),
        ],
        out_specs=pl.BlockSpec((1, t, DV), lambda bi, h, i: (bi, i, h)),
        scratch_shapes=[
            pltpu.VMEM((nq, tv, ATTN_SUB), _F32),
            pltpu.VMEM((nq, tv, ATTN_SUB), _F32),
            pltpu.VMEM((nq, 1, ATTN_SUB), _F32),
            pltpu.VMEM((nq, 1, ATTN_SUB), _F32),
            pltpu.VMEM((nq, 1, ATTN_SUB), _F32),
            pltpu.VMEM((nq, vrows, ATTN_SUB), _F32),
        ],
        compiler_params=pltpu.CompilerParams(
            dimension_semantics=("arbitrary", "arbitrary", "arbitrary"), vmem_limit_bytes=VMEM_LIMIT),
        name="mla_attention",
    )(qt, k, vt)


def _cumsum_rows(tril_bf16, x):
    lane = lax.broadcasted_iota(jnp.int32, x.shape, 1)
    hi = x.astype(_BF16).astype(_F32)
    r1 = x - hi
    mid = r1.astype(_BF16).astype(_F32)
    low = r1 - mid
    packed = jnp.where(lane < 8, hi, jnp.where(lane < 16, pltpu.roll(mid, 8, 1), pltpu.roll(low, 16, 1)))
    res = jnp.dot(tril_bf16, packed.astype(_BF16), preferred_element_type=_F32)
    return res + pltpu.roll(res, LANES - 8, 1) + pltpu.roll(res, LANES - 16, 1)


def _gdn_kernel(q_ref, k_ref, v_ref, gb_ref, o_ref, state_ref, gc_ref, n_ref, t_ref, qk_ref, rhs_ref,
                qe_ref, kd_ref, sol_ref, ou_ref, qw_ref, su_ref, sw_ref):
    c = GDN_CHUNK
    n_chunks = q_ref.shape[1] // c
    n_chains = n_chunks * HEADS

    @pl.when(pl.program_id(1) == 0)
    def _():
        state_ref[...] = jnp.zeros_like(state_ref)

    row = lax.broadcasted_iota(jnp.int32, (c, c), 0)
    col = lax.broadcasted_iota(jnp.int32, (c, c), 1)
    incl = col <= row
    xor = row ^ col
    tril = incl.astype(_BF16)

    for ci in range(n_chunks):
        rows = slice(ci * c, (ci + 1) * c)
        gb = gb_ref[0, rows, :]
        gcum = _cumsum_rows(tril, gb)
        gc_ref[ci] = gcum
        gcum_t = gcum.T
        for hd in range(HEADS):
            ch = ci * HEADS + hd
            lo = hd * DV
            q = q_ref[0, rows, lo:lo + DV].astype(_F32)
            k = k_ref[0, rows, lo:lo + DV].astype(_F32)
            v = v_ref[0, rows, lo:lo + DV].astype(_F32)
            beta = jnp.broadcast_to(gb[:, HEADS + hd:HEADS + hd + 1], (c, DV))
            g_col = jnp.broadcast_to(gcum[:, hd:hd + 1], (c, DV))
            g_row = gcum_t[hd:hd + 1, :]
            g_last = gcum[c - 1:c, hd:hd + 1]
            decay = jnp.exp(jnp.where(incl, jnp.concatenate([g_col, g_col], axis=1) - g_row, _NEG))
            e_col = jnp.exp(g_col)
            kb = k * beta
            n_mat = _dot_nt(kb, k) * decay
            n_ref[ch] = n_mat
            t_ref[ch] = jnp.where(xor == 0, 1.0, jnp.where(xor == 1, -n_mat, 0.0))
            qk_ref[ch] = (_dot_nt(q, k) * decay).astype(qk_ref.dtype)
            rhs_ref[ch] = jnp.concatenate([v * beta, kb * e_col], axis=1).astype(rhs_ref.dtype)
            qe_ref[ch] = (q * e_col).astype(qe_ref.dtype)
            kd_ref[ch] = (k * jnp.exp(g_last - g_col)).astype(kd_ref.dtype)

    m = 2
    while m < c:
        blk = lax.broadcasted_iota(jnp.int32, (c, c), 0) ^ lax.broadcasted_iota(jnp.int32, (c, c), 1)
        mask = (blk >= m) & (blk < 2 * m)
        size = m if m % 8 == 0 else c
        odd = [slice(b * m, (b + 1) * m) for b in range(1, c // m, 2)] if m % 8 == 0 else [slice(0, c)]

        def rows_of(ch, odd=odd):
            return jnp.concatenate([t_ref[ch, r, :] for r in odd], axis=0)

        left = []
        for ch in range(n_chains):
            left.append(_dot(rows_of(ch), jnp.where(mask, n_ref[ch], 0.0)))
        for ch in range(n_chains):
            upd = rows_of(ch) - _dot(left[ch], t_ref[ch])
            for j, r in enumerate(odd):
                t_ref[ch, r, :] = upd[j * size:(j + 1) * size]
        m *= 2

    for ch in range(n_chains):
        sol_ref[ch] = _dot(t_ref[ch], rhs_ref[ch]).astype(sol_ref.dtype)
    for ch in range(n_chains):
        qk_sol = _dot(qk_ref[ch], sol_ref[ch])
        ou_ref[ch] = qk_sol[:, :DV]
        qw_ref[ch] = (qe_ref[ch].astype(_F32) - qk_sol[:, DV:]).astype(qw_ref.dtype)
    for ch in range(n_chains):
        kd_sol = _dot_tn(kd_ref[ch], sol_ref[ch])
        su_ref[ch] = kd_sol[:, :DV]
        sw_ref[ch] = kd_sol[:, DV:].astype(sw_ref.dtype)
    for ci in range(n_chunks):
        rows = slice(ci * c, (ci + 1) * c)
        for hd in range(HEADS):
            ch = ci * HEADS + hd
            lo = hd * DV
            g_last = gc_ref[ci, c - 1:c, hd:hd + 1]
            state = state_ref[hd]
            o_ref[0, rows, lo:lo + DV] = (_dot(qw_ref[ch], state) + ou_ref[ch]).astype(o_ref.dtype)
            state_ref[hd] = state * jnp.exp(g_last) + su_ref[ch] - _dot(sw_ref[ch], state)


def _gdn_call(gq, gk, gv, gb):
    b, s, _ = gq.shape
    t = GDN_TILE
    n_chains = (t // GDN_CHUNK) * HEADS
    tok = lambda w: pl.BlockSpec((1, t, w), lambda i, j: (i, j, 0))
    return pl.pallas_call(
        _gdn_kernel,
        out_shape=jax.ShapeDtypeStruct((b, s, WIDTH), _BF16),
        grid=(b, s // t),
        in_specs=[tok(WIDTH), tok(WIDTH), tok(WIDTH), tok(LANES)],
        out_specs=tok(WIDTH),
        scratch_shapes=[
            pltpu.VMEM((HEADS, DV, DV), _F32),
            pltpu.VMEM((t // GDN_CHUNK, GDN_CHUNK, LANES), _F32),
            pltpu.VMEM((n_chains, GDN_CHUNK, GDN_CHUNK), _F32),
            pltpu.VMEM((n_chains, GDN_CHUNK, GDN_CHUNK), _F32),
            pltpu.VMEM((n_chains, GDN_CHUNK, GDN_CHUNK), _BF16),
            pltpu.VMEM((n_chains, GDN_CHUNK, 2 * DV), _BF16),
            pltpu.VMEM((n_chains, GDN_CHUNK, DV), _BF16),
            pltpu.VMEM((n_chains, GDN_CHUNK, DV), _BF16),
            pltpu.VMEM((n_chains, GDN_CHUNK, 2 * DV), _BF16),
            pltpu.VMEM((n_chains, GDN_CHUNK, DV), _F32),
            pltpu.VMEM((n_chains, GDN_CHUNK, DV), _BF16),
            pltpu.VMEM((n_chains, DV, DV), _F32),
            pltpu.VMEM((n_chains, DV, DV), _BF16),
        ],
        compiler_params=pltpu.CompilerParams(dimension_semantics=("arbitrary", "arbitrary"),
                                             vmem_limit_bytes=VMEM_LIMIT),
        name="gated_delta_rule",
    )(gq, gk, gv, gb)


def _out_kernel(x_ref, mod_ref, om_ref, zm_ref, og_ref, zg_ref, onw_ref, wout_ref, postw_ref, o_ref):
    d = x_ref.shape[-1]
    y_mla = om_ref[0].astype(_F32) * _silu(zm_ref[0].astype(_F32))
    og = og_ref[0].astype(_F32)
    zg = _silu(zg_ref[0].astype(_F32))
    parts = [y_mla]
    for hd in range(HEADS):
        lo = hd * DV
        parts.append(_rms(og[:, lo:lo + DV], onw_ref[...]) * zg[:, lo:lo + DV])
    y = _dot(jnp.concatenate(parts, axis=1), wout_ref[...])
    gate = mod_ref[0][:, 2 * d:3 * d]
    o_ref[0] = x_ref[0] + gate * _rms(y, postw_ref[...])


def _out_call(x, mod_l, o_mla, z_mla, o_gdn, z_gdn, onw, wout, postw):
    b, s, d = x.shape
    tm = min(TOKEN_TILE, s)
    const = lambda shape: pl.BlockSpec(shape, lambda i, j: (0,) * len(shape))
    tok = lambda w: pl.BlockSpec((1, tm, w), lambda i, j: (i, j, 0))
    return pl.pallas_call(
        _out_kernel,
        out_shape=jax.ShapeDtypeStruct((b, s, d), _F32),
        grid=(b, s // tm),
        in_specs=[
            tok(d),
            pl.BlockSpec((1, 1, 3 * d), lambda i, j: (i, 0, 0)),
            tok(WIDTH), tok(WIDTH), tok(WIDTH), tok(WIDTH),
            const((1, DV)),
            const((2 * WIDTH, d)),
            const((1, d)),
        ],
        out_specs=tok(d),
        compiler_params=pltpu.CompilerParams(dimension_semantics=("arbitrary", "arbitrary"),
                                             vmem_limit_bytes=VMEM_LIMIT),
        name="out_proj",
    )(x, mod_l, o_mla, z_mla, o_gdn, z_gdn, onw, wout, postw)


def _rot_cols(w):
    half = w.shape[-1] // 2
    return jnp.concatenate([-w[:, half:], w[:, :half]], axis=-1)


def _pad_cols(w, n):
    return jnp.pad(w, ((0, 0), (0, n - w.shape[-1])))


def _prep_w_in(w):
    o_kv = Q_RANK
    o_kpe = o_kv + KV_RANK
    o_zm = o_kpe + ROPE
    o_qkv = o_zm + WIDTH
    o_a = o_qkv + 3 * WIDTH
    o_zg = o_a + 2 * HEADS
    k_pe = w[:, o_kpe:o_zm]
    return jnp.concatenate([
        w[:, :o_kpe], k_pe, _rot_cols(k_pe), w[:, o_zm:o_qkv], w[:, o_qkv:o_a],
        _pad_cols(w[:, o_a:o_zg], LANES), w[:, o_zg:],
    ], axis=1).astype(_BF16)


def _prep_q_up(w):
    per = w.reshape(Q_RANK, HEADS, DQK)
    nope = per[:, :, :NOPE].reshape(Q_RANK, HEADS * NOPE)
    pe = [jnp.concatenate([per[:, h, NOPE:], _rot_cols(per[:, h, NOPE:])], axis=1) for h in range(HEADS)]
    return jnp.concatenate([nope] + pe, axis=1).astype(_BF16)


def _prep_kv_up(w):
    per = w.reshape(KV_RANK, HEADS, NOPE + DV)
    return jnp.concatenate([per[:, :, :NOPE].reshape(KV_RANK, HEADS * NOPE),
                            per[:, :, NOPE:].reshape(KV_RANK, HEADS * DV)], axis=1).astype(_BF16)


def _lane_row(vals):
    return _pad_cols(vals.reshape(1, -1).astype(_F32), LANES)


def kernel(x, c, positions, w_mod, b_mod, pre_norm_w, post_norm_w, w_in, mla_q_norm_w, mla_q_up,
           mla_kv_norm_w, mla_kv_up, gdn_conv_w, gdn_a_log, gdn_dt_bias, gdn_o_norm_w, w_out):
    depth = w_mod.shape[0]
    b, s, d = x.shape
    assert s % TOKEN_TILE == 0 and s % GDN_TILE == 0 and s % ATTN_Q == 0
    assert TOKEN_TILE % ATTN_SUB == 0 and ATTN_Q % TOKEN_TILE == 0 and (ATTN_Q // ATTN_SUB) % 2 == 0
    mod = _modulation(c, w_mod, b_mod)
    tab = _rope_table(positions)
    for l in range(depth):
        mod_l = mod[l].reshape(b, 1, 3 * d)
        q, k, v, z_mla, gq, gk, gv, gb, z_gdn = _in_call(
            x, mod_l, pre_norm_w[l].reshape(1, d), _prep_w_in(w_in[l]),
            mla_q_norm_w[l].reshape(1, Q_RANK), _prep_q_up(mla_q_up[l]),
            mla_kv_norm_w[l].reshape(1, KV_RANK), _prep_kv_up(mla_kv_up[l]), tab,
            gdn_conv_w[l], _lane_row(gdn_a_log[l]), _lane_row(gdn_dt_bias[l]))
        o_mla = _attn_call(q, k, v)
        o_gdn = _gdn_call(gq, gk, gv, gb)
        x = _out_call(x, mod_l, o_mla, z_mla, o_gdn, z_gdn, gdn_o_norm_w[l].reshape(1, DV),
                      w_out[l].astype(_BF16), post_norm_w[l].reshape(1, d))
    return x
```

```python
import functools

import jax
import jax.numpy as jnp
import numpy as np
from jax import lax
from jax.experimental import pallas as pl
from jax.experimental.pallas import tpu as pltpu

HEADS = 4
NOPE = 128
ROPE = 64
DQK = NOPE + ROPE
DV = 128
Q_RANK = 384
KV_RANK = 256
WIDTH = HEADS * DV
CONV_K = 4
ROPE_THETA = 10000.0
EPS = 1e-6

LANES = 128

C_QLAT = 0
C_KVLAT = C_QLAT + Q_RANK
C_KPE = C_KVLAT + KV_RANK
C_ZMLA = C_KPE + LANES
C_QKV = C_ZMLA + WIDTH
C_AB = C_QKV + 3 * WIDTH
C_ZGDN = C_AB + LANES
N_COLS = C_ZGDN + WIDTH

TOKEN_TILE = 512
ATTN_Q = 1024
ATTN_SUB = 256
VT_ROWS = DV + 16
GDN_CHUNK = 256
GDN_TILE = 512
ROPE_TILE = 1024
VMEM_LIMIT = 56 * 1024 * 1024

_F32 = jnp.float32
_BF16 = jnp.bfloat16
_NEG = -1e30
LOG2E = float(np.log2(np.e))


def _sigmoid(x):
    return 1.0 / (1.0 + jnp.exp2(x * -LOG2E))


def _silu(x):
    return x * _sigmoid(x)


def _softplus(x):
    return jnp.maximum(x, 0.0) + jnp.log(1.0 + jnp.exp(-jnp.abs(x)))


def _rms(x, w):
    return x * lax.rsqrt(jnp.mean(x * x, axis=-1, keepdims=True) + EPS) * w


def _dot(a, b):
    return jnp.dot(a.astype(_BF16), b.astype(_BF16), preferred_element_type=_F32)


def _dot_nt(a, b):
    return lax.dot_general(a.astype(_BF16), b.astype(_BF16), (((1,), (1,)), ((), ())),
                           preferred_element_type=_F32)


def _dot_tn(a, b):
    return lax.dot_general(a.astype(_BF16), b.astype(_BF16), (((0,), (0,)), ((), ())),
                           preferred_element_type=_F32)


def _mod_kernel(c_ref, w_ref, b_ref, o_ref):
    c_act = _silu(c_ref[...])
    o_ref[0] = jnp.dot(c_act, w_ref[0], preferred_element_type=_F32,
                       precision=lax.Precision.HIGHEST) + b_ref[0]


def _modulation(c, w_mod, b_mod):
    depth, d, d3 = w_mod.shape
    b = c.shape[0]
    return pl.pallas_call(
        _mod_kernel,
        out_shape=jax.ShapeDtypeStruct((depth, b, d3), _F32),
        grid=(depth, d3 // d),
        in_specs=[
            pl.BlockSpec((b, d), lambda l, j: (0, 0)),
            pl.BlockSpec((1, d, d), lambda l, j: (l, 0, j)),
            pl.BlockSpec((1, 1, d), lambda l, j: (l, 0, j)),
        ],
        out_specs=pl.BlockSpec((1, b, d), lambda l, j: (l, 0, j)),
        compiler_params=pltpu.CompilerParams(dimension_semantics=("arbitrary", "arbitrary")),
        name="modulation",
    )(c, w_mod, b_mod.reshape(depth, 1, d3))


def _rope_kernel(pos_ref, freq_ref, o_ref):
    ang = pos_ref[0].astype(_F32) * freq_ref[...]
    lane = lax.broadcasted_iota(jnp.int32, ang.shape, 1)
    o_ref[0] = jnp.where(lane < ROPE, jnp.cos(ang), jnp.sin(ang))


def _rope_table(positions):
    b, s = positions.shape
    half = ROPE // 2
    inv_freq = jnp.power(ROPE_THETA, -jnp.arange(half, dtype=_F32) * 2.0 / ROPE)
    freq = jnp.tile(inv_freq, LANES // half).reshape(1, LANES)
    t = min(ROPE_TILE, s)
    return pl.pallas_call(
        _rope_kernel,
        out_shape=jax.ShapeDtypeStruct((b, s, LANES), _F32),
        grid=(b, s // t),
        in_specs=[
            pl.BlockSpec((1, t, 1), lambda i, j: (i, j, 0)),
            pl.BlockSpec((1, LANES), lambda i, j: (0, 0)),
        ],
        out_specs=pl.BlockSpec((1, t, LANES), lambda i, j: (i, j, 0)),
        compiler_params=pltpu.CompilerParams(dimension_semantics=("arbitrary", "arbitrary")),
        name="rope_table",
    )(positions.reshape(b, s, 1), freq)


def _in_kernel(x_ref, mod_ref, prew_ref, win_ref, qnw_ref, qup_ref, kvnw_ref, kvup_ref, tab_ref,
               convw_ref, alog_ref, dtb_ref,
               qt_ref, k_ref, vt_ref, zm_ref, gq_ref, gk_ref, gv_ref, gb_ref, zg_ref,
               tail_ref):
    d = x_ref.shape[-1]
    tm = x_ref.shape[1]

    @pl.when(pl.program_id(1) == 0)
    def _():
        tail_ref[tm:tm + 8, :] = jnp.zeros((8, tail_ref.shape[1]), tail_ref.dtype)

    x = x_ref[0]
    mod = mod_ref[0]
    w_eff = prew_ref[...] * (1.0 + mod[:, d:2 * d])
    h = (x * lax.rsqrt(jnp.mean(x * x, axis=-1, keepdims=True) + EPS) * w_eff + mod[:, 0:d]).astype(_BF16)

    def project(lo, hi):
        return jnp.dot(h, win_ref[:, lo:hi], preferred_element_type=_F32)

    proj = project(0, C_ZMLA)

    tab = tab_ref[0]
    q_scale = DQK ** -0.5 * LOG2E

    def rope(block):
        y = block * tab
        return y + pltpu.roll(y, ROPE, 1)

    qf = _dot(_rms(proj[:, C_QLAT:C_QLAT + Q_RANK], qnw_ref[...]), qup_ref[...])
    kvf = _dot(_rms(proj[:, C_KVLAT:C_KVLAT + KV_RANK], kvnw_ref[...]), kvup_ref[...])
    k_pe = rope(proj[:, C_KPE:C_KPE + LANES])[:, :ROPE].astype(k_ref.dtype)
    for hd in range(HEADS):
        lo = hd * NOPE
        q_pe = rope(qf[:, WIDTH + hd * LANES:WIDTH + (hd + 1) * LANES])
        qt_ref[0, hd, 0:NOPE, :] = (qf[:, lo:lo + NOPE] * q_scale).T.astype(qt_ref.dtype)
        qt_ref[0, hd, NOPE:DQK, :] = (q_pe * q_scale).T[:ROPE, :].astype(qt_ref.dtype)
        k_ref[0, hd, :, 0:NOPE] = kvf[:, lo:lo + NOPE].astype(k_ref.dtype)
        k_ref[0, hd, :, NOPE:DQK] = k_pe
        vt_ref[0, hd, 0, 0:DV, :] = kvf[:, WIDTH + lo:WIDTH + lo + DV].T.astype(vt_ref.dtype)
        vt_ref[0, hd, 0, DV:, :] = jnp.ones((VT_ROWS - DV, tm), vt_ref.dtype)
    zm_ref[0] = project(C_ZMLA, C_QKV).astype(zm_ref.dtype)

    for part, out_ref in enumerate((gq_ref, gk_ref, gv_ref)):
        cols = slice(part * WIDTH, (part + 1) * WIDTH)
        xg = project(C_QKV + part * WIDTH, C_QKV + (part + 1) * WIDTH)
        tail_ref[0:8, cols] = tail_ref[tm:tm + 8, cols]
        tail_ref[8:8 + tm, cols] = xg
        cw = convw_ref[:, cols]
        conv = xg * cw[CONV_K - 1:CONV_K, :]
        for j in range(CONV_K - 1):
            start = 8 - (CONV_K - 1 - j)
            conv = conv + tail_ref[start:start + tm, cols] * cw[j:j + 1, :]
        act = _silu(conv)
        if part == 2:
            out_ref[0] = act.astype(out_ref.dtype)
        else:
            scale = DV ** -0.5 if part == 0 else 1.0
            for hd in range(HEADS):
                lo = hd * DV
                a = act[:, lo:lo + DV]
                inv = lax.rsqrt(jnp.sum(a * a, axis=-1, keepdims=True) + EPS) * scale
                out_ref[0, :, lo:lo + DV] = (a * inv).astype(out_ref.dtype)

    gates = project(C_AB, N_COLS)
    ab = gates[:, 0:LANES]
    g = -jnp.exp(alog_ref[...]) * _softplus(ab + dtb_ref[...])
    lane = lax.broadcasted_iota(jnp.int32, ab.shape, 1)
    gb_ref[0] = jnp.where(lane < HEADS, g, _sigmoid(ab))
    zg_ref[0] = gates[:, C_ZGDN - C_AB:].astype(zg_ref.dtype)


def _in_call(layer, x, mod, prew, win, qnw, qup, kvnw, kvup, tab, convw, alog, dtb):
    b, s, d = x.shape
    tm = min(TOKEN_TILE, s)
    const = lambda shape: pl.BlockSpec((None,) + shape, lambda i, j: (layer,) + (0,) * len(shape))
    tok = lambda w: pl.BlockSpec((1, tm, w), lambda i, j: (i, j, 0))
    head = pl.BlockSpec((1, HEADS, tm, DQK), lambda i, j: (i, 0, j, 0))
    head_t = pl.BlockSpec((1, HEADS, DQK, tm), lambda i, j: (i, 0, 0, j))
    head_vt = pl.BlockSpec((1, HEADS, 1, VT_ROWS, tm), lambda i, j: (i, 0, j, 0, 0))
    out_shape = (
        jax.ShapeDtypeStruct((b, HEADS, DQK, s), _BF16),
        jax.ShapeDtypeStruct((b, HEADS, s, DQK), _BF16),
        jax.ShapeDtypeStruct((b, HEADS, s // tm, VT_ROWS, tm), _BF16),
        jax.ShapeDtypeStruct((b, s, WIDTH), _BF16),
        jax.ShapeDtypeStruct((b, s, WIDTH), _BF16),
        jax.ShapeDtypeStruct((b, s, WIDTH), _BF16),
        jax.ShapeDtypeStruct((b, s, WIDTH), _BF16),
        jax.ShapeDtypeStruct((b, s, LANES), _F32),
        jax.ShapeDtypeStruct((b, s, WIDTH), _BF16),
    )
    return pl.pallas_call(
        _in_kernel,
        out_shape=out_shape,
        grid=(b, s // tm),
        in_specs=[
            tok(d),
            pl.BlockSpec((None, 1, 1, 3 * d), lambda i, j: (layer, i, 0, 0)),
            const((1, d)),
            const((d, N_COLS)),
            const((1, Q_RANK)),
            const((Q_RANK, 2 * WIDTH)),
            const((1, KV_RANK)),
            const((KV_RANK, 2 * WIDTH)),
            tok(LANES),
            const((CONV_K, 3 * WIDTH)),
            const((1, LANES)),
            const((1, LANES)),
        ],
        out_specs=(head_t, head, head_vt, tok(WIDTH), tok(WIDTH), tok(WIDTH), tok(WIDTH),
                   tok(LANES), tok(WIDTH)),
        scratch_shapes=[pltpu.VMEM((tm + 8, 3 * WIDTH), _F32)],
        compiler_params=pltpu.CompilerParams(dimension_semantics=("arbitrary", "arbitrary"),
                                             vmem_limit_bytes=VMEM_LIMIT),
        name="in_proj",
    )(x, mod, prew, win, qnw, qup, kvnw, kvup, tab, convw, alog, dtb)


def _attn_kernel(qt_ref, k_ref, vt_ref, o_ref, sa_ref, sb_ref, mxa_ref, mxb_ref, m_ref, acc_ref):
    tq = qt_ref.shape[3]
    uk = vt_ref.shape[4]
    uq = ATTN_SUB
    nq = tq // uq
    nu = tq // uk
    i = pl.program_id(2)
    qts = [qt_ref[0, 0, :, h * uq:(h + 1) * uq] for h in range(nq)]
    bufs = ((sa_ref, mxa_ref), (sb_ref, mxb_ref))

    m_ref[...] = jnp.full_like(m_ref, _NEG)
    acc_ref[...] = jnp.zeros_like(acc_ref)

    def produce(buf, blk, unit, subs, diagonal=False):
        s_ref, mx_ref = bufs[buf]
        kj = k_ref[0, 0, pl.ds(pl.multiple_of(blk * tq + unit * uk, uk), uk), :]
        for h in subs:
            keys = min(uk, (h - unit * (uk // uq) + 1) * uq) if diagonal else uk
            s = jnp.dot(kj[0:keys], qts[h], preferred_element_type=_F32)
            s_ref[h, 0:keys, :] = s
            mx_ref[h] = jnp.max(s, axis=0, keepdims=True)

    def consume(buf, blk, unit, subs, masked=()):
        s_ref, mx_ref = bufs[buf]
        vt = vt_ref[0, 0, blk * nu + unit]
        for h in subs:
            if h in masked:
                own = h - unit * (uk // uq)
                tri = (lax.broadcasted_iota(jnp.int32, (uq, uq), 0)
                       <= lax.broadcasted_iota(jnp.int32, (uq, uq), 1))
                s = jnp.where(tri, s_ref[h, own * uq:(own + 1) * uq, :], _NEG)
                if own > 0:
                    s = jnp.concatenate([s_ref[h, 0:own * uq, :], s], axis=0)
                mx = jnp.max(s, axis=0, keepdims=True)
                vth = vt_ref[0, 0, blk * nu + unit, :, 0:(own + 1) * uq]
            else:
                s = s_ref[h]
                mx = mx_ref[h]
                vth = vt
            m = m_ref[h]
            m_new = jnp.maximum(m, mx)
            p = jnp.exp2(s - m_new).astype(_BF16)
            acc_ref[h] = jnp.exp2(m - m_new) * acc_ref[h] + jnp.dot(vth, p, preferred_element_type=_F32)
            m_ref[h] = m_new

    every = tuple(range(nq))
    produce(0, 0, 0, every)

    def body(j, carry):
        for unit in range(nu):
            if unit + 1 < nu:
                produce((unit + 1) % 2, j, unit + 1, every)
            else:
                produce((unit + 1) % 2, j + 1, 0, every)
            consume(unit % 2, j, unit, every)
        return carry

    lax.fori_loop(0, i, body, 0)
    per = uk // uq
    for unit in range(nu):
        if unit + 1 < nu:
            produce((unit + 1) % 2, i, unit + 1, tuple(range((unit + 1) * per, nq)), diagonal=True)
        consume(unit % 2, i, unit, tuple(range(unit * per, nq)),
                masked=tuple(range(unit * per, (unit + 1) * per)))
    for h in range(nq):
        acc = acc_ref[h]
        o_ref[0, h * uq:(h + 1) * uq, :] = (acc[:DV] / acc[DV:DV + 1]).T.astype(o_ref.dtype)


def _attn_call(qt, k, vt):
    b, _, _, s = qt.shape
    vrows, tv = vt.shape[-2:]
    t = ATTN_Q
    nq = t // ATTN_SUB
    return pl.pallas_call(
        _attn_kernel,
        out_shape=jax.ShapeDtypeStruct((b, s, WIDTH), _BF16),
        grid=(b, HEADS, s // t),
        in_specs=[
            pl.BlockSpec((1, 1, DQK, t), lambda bi, h, i: (bi, h, 0, i)),
            pl.BlockSpec((1, 1, s, DQK), lambda bi, h, i: (bi, h, 0, 0)),
            pl.BlockSpec((1, 1, s // tv, vrows, tv), lambda bi, h, i: (bi, h, 0, 0, 0)),
        ],
        out_specs=pl.BlockSpec((1, t, DV), lambda bi, h, i: (bi, i, h)),
        scratch_shapes=[
            pltpu.VMEM((nq, tv, ATTN_SUB), _F32),
            pltpu.VMEM((nq, tv, ATTN_SUB), _F32),
            pltpu.VMEM((nq, 1, ATTN_SUB), _F32),
            pltpu.VMEM((nq, 1, ATTN_SUB), _F32),
            pltpu.VMEM((nq, 1, ATTN_SUB), _F32),
            pltpu.VMEM((nq, vrows, ATTN_SUB), _F32),
        ],
        compiler_params=pltpu.CompilerParams(
            dimension_semantics=("arbitrary", "arbitrary", "arbitrary"), vmem_limit_bytes=VMEM_LIMIT),
        name="mla_attention",
    )(qt, k, vt)


def _cumsum_rows(tril_bf16, x):
    lane = lax.broadcasted_iota(jnp.int32, x.shape, 1)
    hi = x.astype(_BF16).astype(_F32)
    r1 = x - hi
    mid = r1.astype(_BF16).astype(_F32)
    low = r1 - mid
    packed = jnp.where(lane < 8, hi, jnp.where(lane < 16, pltpu.roll(mid, 8, 1), pltpu.roll(low, 16, 1)))
    res = jnp.dot(tril_bf16, packed.astype(_BF16), preferred_element_type=_F32)
    return res + pltpu.roll(res, LANES - 8, 1) + pltpu.roll(res, LANES - 16, 1)


def _gdn_kernel(q_ref, k_ref, v_ref, gb_ref, o_ref, state_ref, gc_ref, n_ref, t_ref, qk_ref, rhs_ref,
                qe_ref, kd_ref, sol_ref, ou_ref, qw_ref, su_ref, sw_ref):
    c = GDN_CHUNK
    n_chunks = q_ref.shape[1] // c
    n_chains = n_chunks * HEADS

    @pl.when(pl.program_id(1) == 0)
    def _():
        state_ref[...] = jnp.zeros_like(state_ref)

    row = lax.broadcasted_iota(jnp.int32, (c, c), 0)
    col = lax.broadcasted_iota(jnp.int32, (c, c), 1)
    incl = col <= row
    xor = row ^ col
    tril = incl.astype(_BF16)

    for ci in range(n_chunks):
        rows = slice(ci * c, (ci + 1) * c)
        gb = gb_ref[0, rows, :]
        gcum = _cumsum_rows(tril, gb)
        gc_ref[ci] = gcum
        gcum_t = gcum.T
        for hd in range(HEADS):
            ch = ci * HEADS + hd
            lo = hd * DV
            q = q_ref[0, rows, lo:lo + DV].astype(_F32)
            k = k_ref[0, rows, lo:lo + DV].astype(_F32)
            v = v_ref[0, rows, lo:lo + DV].astype(_F32)
            beta = jnp.broadcast_to(gb[:, HEADS + hd:HEADS + hd + 1], (c, DV))
            g_col = jnp.broadcast_to(gcum[:, hd:hd + 1], (c, DV))
            g_row = gcum_t[hd:hd + 1, :]
            g_last = gcum[c - 1:c, hd:hd + 1]
            decay = jnp.exp(jnp.where(incl, jnp.concatenate([g_col, g_col], axis=1) - g_row, _NEG))
            e_col = jnp.exp(g_col)
            kb = k * beta
            n_mat = _dot_nt(kb, k) * decay
            n_ref[ch] = n_mat
            t_ref[ch] = jnp.where(xor == 0, 1.0, jnp.where(xor == 1, -n_mat, 0.0))
            qk_ref[ch] = (_dot_nt(q, k) * decay).astype(qk_ref.dtype)
            rhs_ref[ch] = jnp.concatenate([v * beta, kb * e_col], axis=1).astype(rhs_ref.dtype)
            qe_ref[ch] = (q * e_col).astype(qe_ref.dtype)
            kd_ref[ch] = (k * jnp.exp(g_last - g_col)).astype(kd_ref.dtype)

    m = 2
    while m < c:
        blk = lax.broadcasted_iota(jnp.int32, (c, c), 0) ^ lax.broadcasted_iota(jnp.int32, (c, c), 1)
        mask = (blk >= m) & (blk < 2 * m)
        size = m if m % 8 == 0 else c
        odd = [slice(b * m, (b + 1) * m) for b in range(1, c // m, 2)] if m % 8 == 0 else [slice(0, c)]

        def rows_of(ch, odd=odd):
            return jnp.concatenate([t_ref[ch, r, :] for r in odd], axis=0)

        left = []
        for ch in range(n_chains):
            left.append(_dot(rows_of(ch), jnp.where(mask, n_ref[ch], 0.0)))
        for ch in range(n_chains):
            upd = rows_of(ch) - _dot(left[ch], t_ref[ch])
            for j, r in enumerate(odd):
                t_ref[ch, r, :] = upd[j * size:(j + 1) * size]
        m *= 2

    for ch in range(n_chains):
        sol_ref[ch] = _dot(t_ref[ch], rhs_ref[ch]).astype(sol_ref.dtype)
    for ch in range(n_chains):
        qk_sol = _dot(qk_ref[ch], sol_ref[ch])
        ou_ref[ch] = qk_sol[:, :DV]
        qw_ref[ch] = (qe_ref[ch].astype(_F32) - qk_sol[:, DV:]).astype(qw_ref.dtype)
    for ch in range(n_chains):
        kd_sol = _dot_tn(kd_ref[ch], sol_ref[ch])
        su_ref[ch] = kd_sol[:, :DV]
        sw_ref[ch] = kd_sol[:, DV:].astype(sw_ref.dtype)
    for ci in range(n_chunks):
        rows = slice(ci * c, (ci + 1) * c)
        for hd in range(HEADS):
            ch = ci * HEADS + hd
            lo = hd * DV
            g_last = gc_ref[ci, c - 1:c, hd:hd + 1]
            state = state_ref[hd]
            o_ref[0, rows, lo:lo + DV] = (_dot(qw_ref[ch], state) + ou_ref[ch]).astype(o_ref.dtype)
            state_ref[hd] = state * jnp.exp(g_last) + su_ref[ch] - _dot(sw_ref[ch], state)


def _gdn_call(gq, gk, gv, gb):
    b, s, _ = gq.shape
    t = GDN_TILE
    n_chains = (t // GDN_CHUNK) * HEADS
    tok = lambda w: pl.BlockSpec((1, t, w), lambda i, j: (i, j, 0))
    return pl.pallas_call(
        _gdn_kernel,
        out_shape=jax.ShapeDtypeStruct((b, s, WIDTH), _BF16),
        grid=(b, s // t),
        in_specs=[tok(WIDTH), tok(WIDTH), tok(WIDTH), tok(LANES)],
        out_specs=tok(WIDTH),
        scratch_shapes=[
            pltpu.VMEM((HEADS, DV, DV), _F32),
            pltpu.VMEM((t // GDN_CHUNK, GDN_CHUNK, LANES), _F32),
            pltpu.VMEM((n_chains, GDN_CHUNK, GDN_CHUNK), _F32),
            pltpu.VMEM((n_chains, GDN_CHUNK, GDN_CHUNK), _F32),
            pltpu.VMEM((n_chains, GDN_CHUNK, GDN_CHUNK), _BF16),
            pltpu.VMEM((n_chains, GDN_CHUNK, 2 * DV), _BF16),
            pltpu.VMEM((n_chains, GDN_CHUNK, DV), _BF16),
            pltpu.VMEM((n_chains, GDN_CHUNK, DV), _BF16),
            pltpu.VMEM((n_chains, GDN_CHUNK, 2 * DV), _BF16),
            pltpu.VMEM((n_chains, GDN_CHUNK, DV), _F32),
            pltpu.VMEM((n_chains, GDN_CHUNK, DV), _BF16),
            pltpu.VMEM((n_chains, DV, DV), _F32),
            pltpu.VMEM((n_chains, DV, DV), _BF16),
        ],
        compiler_params=pltpu.CompilerParams(dimension_semantics=("arbitrary", "arbitrary"),
                                             vmem_limit_bytes=VMEM_LIMIT),
        name="gated_delta_rule",
    )(gq, gk, gv, gb)


def _out_kernel(x_ref, mod_ref, om_ref, zm_ref, og_ref, zg_ref, onw_ref, wout_ref, postw_ref, o_ref):
    d = x_ref.shape[-1]
    y_mla = om_ref[0].astype(_F32) * _silu(zm_ref[0].astype(_F32))
    og = og_ref[0].astype(_F32)
    zg = _silu(zg_ref[0].astype(_F32))
    parts = [y_mla]
    for hd in range(HEADS):
        lo = hd * DV
        parts.append(_rms(og[:, lo:lo + DV], onw_ref[...]) * zg[:, lo:lo + DV])
    y = _dot(jnp.concatenate(parts, axis=1), wout_ref[...])
    gate = mod_ref[0][:, 2 * d:3 * d]
    o_ref[0] = x_ref[0] + gate * _rms(y, postw_ref[...])


def _out_call(layer, x, mod, o_mla, z_mla, o_gdn, z_gdn, onw, wout, postw):
    b, s, d = x.shape
    tm = min(TOKEN_TILE, s)
    const = lambda shape: pl.BlockSpec((None,) + shape, lambda i, j: (layer,) + (0,) * len(shape))
    tok = lambda w: pl.BlockSpec((1, tm, w), lambda i, j: (i, j, 0))
    return pl.pallas_call(
        _out_kernel,
        out_shape=jax.ShapeDtypeStruct((b, s, d), _F32),
        grid=(b, s // tm),
        in_specs=[
            tok(d),
            pl.BlockSpec((None, 1, 1, 3 * d), lambda i, j: (layer, i, 0, 0)),
            tok(WIDTH), tok(WIDTH), tok(WIDTH), tok(WIDTH),
            const((1, DV)),
            const((2 * WIDTH, d)),
            const((1, d)),
        ],
        out_specs=tok(d),
        compiler_params=pltpu.CompilerParams(dimension_semantics=("arbitrary", "arbitrary"),
                                             vmem_limit_bytes=VMEM_LIMIT),
        name="out_proj",
    )(x, mod, o_mla, z_mla, o_gdn, z_gdn, onw, wout, postw)


def _rot_cols(w):
    half = w.shape[-1] // 2
    return jnp.concatenate([-w[..., half:], w[..., :half]], axis=-1)


def _pad_cols(w, n):
    return jnp.pad(w, [(0, 0)] * (w.ndim - 1) + [(0, n - w.shape[-1])])


def _prep_w_in(w):
    w = w.astype(_BF16)
    o_kv = Q_RANK
    o_kpe = o_kv + KV_RANK
    o_zm = o_kpe + ROPE
    o_qkv = o_zm + WIDTH
    o_a = o_qkv + 3 * WIDTH
    o_zg = o_a + 2 * HEADS
    k_pe = w[..., o_kpe:o_zm]
    return jnp.concatenate([
        w[..., :o_kpe], k_pe, _rot_cols(k_pe), w[..., o_zm:o_qkv], w[..., o_qkv:o_a],
        _pad_cols(w[..., o_a:o_zg], LANES), w[..., o_zg:],
    ], axis=-1)


def _prep_q_up(w):
    depth = w.shape[0]
    per = w.astype(_BF16).reshape(depth, Q_RANK, HEADS, DQK)
    nope = per[..., :NOPE].reshape(depth, Q_RANK, HEADS * NOPE)
    pe = [jnp.concatenate([per[:, :, h, NOPE:], _rot_cols(per[:, :, h, NOPE:])], axis=-1) for h in range(HEADS)]
    return jnp.concatenate([nope] + pe, axis=-1)


def _prep_kv_up(w):
    depth = w.shape[0]
    per = w.astype(_BF16).reshape(depth, KV_RANK, HEADS, NOPE + DV)
    return jnp.concatenate([per[..., :NOPE].reshape(depth, KV_RANK, HEADS * NOPE),
                            per[..., NOPE:].reshape(depth, KV_RANK, HEADS * DV)], axis=-1)


def _lane_rows(vals):
    return _pad_cols(vals.astype(_F32), LANES)[:, None, :]


def kernel(x, c, positions, w_mod, b_mod, pre_norm_w, post_norm_w, w_in, mla_q_norm_w, mla_q_up,
           mla_kv_norm_w, mla_kv_up, gdn_conv_w, gdn_a_log, gdn_dt_bias, gdn_o_norm_w, w_out):
    depth = w_mod.shape[0]
    b, s, d = x.shape
    assert s % TOKEN_TILE == 0 and s % GDN_TILE == 0 and s % ATTN_Q == 0
    assert TOKEN_TILE % ATTN_SUB == 0 and ATTN_Q % TOKEN_TILE == 0 and (ATTN_Q // ATTN_SUB) % 2 == 0
    mod = _modulation(c, w_mod, b_mod).reshape(depth, b, 1, 3 * d)
    tab = _rope_table(positions)
    in_params = (pre_norm_w[:, None, :], _prep_w_in(w_in), mla_q_norm_w[:, None, :], _prep_q_up(mla_q_up),
                 mla_kv_norm_w[:, None, :], _prep_kv_up(mla_kv_up))
    gdn_params = (gdn_conv_w, _lane_rows(gdn_a_log), _lane_rows(gdn_dt_bias))
    out_params = (gdn_o_norm_w[:, None, :], w_out.astype(_BF16), post_norm_w[:, None, :])
    for l in range(depth):
        q, k, v, z_mla, gq, gk, gv, gb, z_gdn = _in_call(l, x, mod, *in_params, tab, *gdn_params)
        o_mla = _attn_call(q, k, v)
        o_gdn = _gdn_call(gq, gk, gv, gb)
        x = _out_call(l, x, mod, o_mla, z_mla, o_gdn, z_gdn, *out_params)
    return x
```

```python
import functools

import jax
import jax.numpy as jnp
import numpy as np
from jax import lax
from jax.experimental import pallas as pl
from jax.experimental.pallas import tpu as pltpu

HEADS = 4
NOPE = 128
ROPE = 64
DQK = NOPE + ROPE
DV = 128
Q_RANK = 384
KV_RANK = 256
WIDTH = HEADS * DV
CONV_K = 4
ROPE_THETA = 10000.0
EPS = 1e-6

LANES = 128

C_QLAT = 0
C_KVLAT = C_QLAT + Q_RANK
C_KPE = C_KVLAT + KV_RANK
C_ZMLA = C_KPE + LANES
C_QKV = C_ZMLA + WIDTH
C_AB = C_QKV + 3 * WIDTH
C_ZGDN = C_AB + LANES
N_COLS = C_ZGDN + WIDTH

TOKEN_TILE = 512
ATTN_Q = 1024
ATTN_SUB = 256
VT_ROWS = DV + 16
GDN_CHUNK = 256
GDN_TILE = 512
OUT_ROWS = 256
ROPE_TILE = 1024
VMEM_LIMIT = 56 * 1024 * 1024

_F32 = jnp.float32
_BF16 = jnp.bfloat16
_NEG = -1e30
LOG2E = float(np.log2(np.e))


def _sigmoid(x):
    return 1.0 / (1.0 + jnp.exp2(x * -LOG2E))


def _silu(x):
    return x * _sigmoid(x)


def _softplus(x):
    return jnp.maximum(x, 0.0) + jnp.log(1.0 + jnp.exp(-jnp.abs(x)))


def _rms(x, w):
    return x * lax.rsqrt(jnp.mean(x * x, axis=-1, keepdims=True) + EPS) * w


def _dot(a, b):
    return jnp.dot(a.astype(_BF16), b.astype(_BF16), preferred_element_type=_F32)


def _dot_nt(a, b):
    return lax.dot_general(a.astype(_BF16), b.astype(_BF16), (((1,), (1,)), ((), ())),
                           preferred_element_type=_F32)


def _dot_tn(a, b):
    return lax.dot_general(a.astype(_BF16), b.astype(_BF16), (((0,), (0,)), ((), ())),
                           preferred_element_type=_F32)


def _mod_kernel(c_ref, w_ref, b_ref, o_ref):
    c_act = _silu(c_ref[...])
    o_ref[0] = jnp.dot(c_act, w_ref[0], preferred_element_type=_F32,
                       precision=lax.Precision.HIGHEST) + b_ref[0]


def _modulation(c, w_mod, b_mod):
    depth, d, d3 = w_mod.shape
    b = c.shape[0]
    return pl.pallas_call(
        _mod_kernel,
        out_shape=jax.ShapeDtypeStruct((depth, b, d3), _F32),
        grid=(depth, d3 // d),
        in_specs=[
            pl.BlockSpec((b, d), lambda l, j: (0, 0)),
            pl.BlockSpec((1, d, d), lambda l, j: (l, 0, j)),
            pl.BlockSpec((1, 1, d), lambda l, j: (l, 0, j)),
        ],
        out_specs=pl.BlockSpec((1, b, d), lambda l, j: (l, 0, j)),
        compiler_params=pltpu.CompilerParams(dimension_semantics=("arbitrary", "arbitrary")),
        name="modulation",
    )(c, w_mod, b_mod.reshape(depth, 1, d3))


def _rope_kernel(pos_ref, freq_ref, o_ref):
    ang = pos_ref[0].astype(_F32) * freq_ref[...]
    lane = lax.broadcasted_iota(jnp.int32, ang.shape, 1)
    o_ref[0] = jnp.where(lane < ROPE, jnp.cos(ang), jnp.sin(ang))


def _rope_table(positions):
    b, s = positions.shape
    half = ROPE // 2
    inv_freq = jnp.power(ROPE_THETA, -jnp.arange(half, dtype=_F32) * 2.0 / ROPE)
    freq = jnp.tile(inv_freq, LANES // half).reshape(1, LANES)
    t = min(ROPE_TILE, s)
    return pl.pallas_call(
        _rope_kernel,
        out_shape=jax.ShapeDtypeStruct((b, s, LANES), _F32),
        grid=(b, s // t),
        in_specs=[
            pl.BlockSpec((1, t, 1), lambda i, j: (i, j, 0)),
            pl.BlockSpec((1, LANES), lambda i, j: (0, 0)),
        ],
        out_specs=pl.BlockSpec((1, t, LANES), lambda i, j: (i, j, 0)),
        compiler_params=pltpu.CompilerParams(dimension_semantics=("arbitrary", "arbitrary")),
        name="rope_table",
    )(positions.reshape(b, s, 1), freq)


def _in_kernel(x_ref, mod_ref, prew_ref, win_ref, qnw_ref, qup_ref, kvnw_ref, kvup_ref, tab_ref,
               convw_ref, alog_ref, dtb_ref,
               qt_ref, k_ref, vt_ref, zm_ref, gq_ref, gk_ref, gv_ref, gb_ref, zg_ref,
               tail_ref):
    d = x_ref.shape[-1]
    tm = x_ref.shape[1]

    @pl.when(pl.program_id(1) == 0)
    def _():
        tail_ref[tm:tm + 8, :] = jnp.zeros((8, tail_ref.shape[1]), tail_ref.dtype)

    x = x_ref[0]
    mod = mod_ref[0]
    w_eff = prew_ref[...] * (1.0 + mod[:, d:2 * d])
    h = (x * lax.rsqrt(jnp.mean(x * x, axis=-1, keepdims=True) + EPS) * w_eff + mod[:, 0:d]).astype(_BF16)

    def project(lo, hi):
        return jnp.dot(h, win_ref[:, lo:hi], preferred_element_type=_F32)

    proj = project(0, C_ZMLA)

    tab = tab_ref[0]
    q_scale = DQK ** -0.5 * LOG2E

    def rope(block):
        y = block * tab
        return y + pltpu.roll(y, ROPE, 1)

    qf = _dot(_rms(proj[:, C_QLAT:C_QLAT + Q_RANK], qnw_ref[...]), qup_ref[...])
    kvf = _dot(_rms(proj[:, C_KVLAT:C_KVLAT + KV_RANK], kvnw_ref[...]), kvup_ref[...])
    k_pe = rope(proj[:, C_KPE:C_KPE + LANES])[:, :ROPE].astype(k_ref.dtype)
    for hd in range(HEADS):
        lo = hd * NOPE
        q_pe = rope(qf[:, WIDTH + hd * LANES:WIDTH + (hd + 1) * LANES])
        qt_ref[0, hd, 0:NOPE, :] = (qf[:, lo:lo + NOPE] * q_scale).T.astype(qt_ref.dtype)
        qt_ref[0, hd, NOPE:DQK, :] = (q_pe * q_scale).T[:ROPE, :].astype(qt_ref.dtype)
        k_ref[0, hd, :, 0:NOPE] = kvf[:, lo:lo + NOPE].astype(k_ref.dtype)
        k_ref[0, hd, :, NOPE:DQK] = k_pe
        vt_ref[0, hd, 0, 0:DV, :] = kvf[:, WIDTH + lo:WIDTH + lo + DV].T.astype(vt_ref.dtype)
        vt_ref[0, hd, 0, DV:, :] = jnp.ones((VT_ROWS - DV, tm), vt_ref.dtype)
    zm_ref[0] = project(C_ZMLA, C_QKV).astype(zm_ref.dtype)

    for part, out_ref in enumerate((gq_ref, gk_ref, gv_ref)):
        cols = slice(part * WIDTH, (part + 1) * WIDTH)
        xg = project(C_QKV + part * WIDTH, C_QKV + (part + 1) * WIDTH)
        tail_ref[0:8, cols] = tail_ref[tm:tm + 8, cols]
        tail_ref[8:8 + tm, cols] = xg
        cw = convw_ref[:, cols]
        conv = xg * cw[CONV_K - 1:CONV_K, :]
        for j in range(CONV_K - 1):
            start = 8 - (CONV_K - 1 - j)
            conv = conv + tail_ref[start:start + tm, cols] * cw[j:j + 1, :]
        act = _silu(conv)
        if part == 2:
            out_ref[0] = act.astype(out_ref.dtype)
        else:
            scale = DV ** -0.5 if part == 0 else 1.0
            for hd in range(HEADS):
                lo = hd * DV
                a = act[:, lo:lo + DV]
                inv = lax.rsqrt(jnp.sum(a * a, axis=-1, keepdims=True) + EPS) * scale
                out_ref[0, :, lo:lo + DV] = (a * inv).astype(out_ref.dtype)

    gates = project(C_AB, N_COLS)
    ab = gates[:, 0:LANES]
    g = -jnp.exp(alog_ref[...]) * _softplus(ab + dtb_ref[...])
    lane = lax.broadcasted_iota(jnp.int32, ab.shape, 1)
    gb_ref[0] = jnp.where(lane < HEADS, g, _sigmoid(ab))
    zg_ref[0] = gates[:, C_ZGDN - C_AB:].astype(zg_ref.dtype)


def _in_call(layer, x, mod, prew, win, qnw, qup, kvnw, kvup, tab, convw, alog, dtb):
    b, s, d = x.shape
    tm = min(TOKEN_TILE, s)
    const = lambda shape: pl.BlockSpec((None,) + shape, lambda i, j: (layer,) + (0,) * len(shape))
    tok = lambda w: pl.BlockSpec((1, tm, w), lambda i, j: (i, j, 0))
    head = pl.BlockSpec((1, HEADS, tm, DQK), lambda i, j: (i, 0, j, 0))
    head_t = pl.BlockSpec((1, HEADS, DQK, tm), lambda i, j: (i, 0, 0, j))
    head_vt = pl.BlockSpec((1, HEADS, 1, VT_ROWS, tm), lambda i, j: (i, 0, j, 0, 0))
    out_shape = (
        jax.ShapeDtypeStruct((b, HEADS, DQK, s), _BF16),
        jax.ShapeDtypeStruct((b, HEADS, s, DQK), _BF16),
        jax.ShapeDtypeStruct((b, HEADS, s // tm, VT_ROWS, tm), _BF16),
        jax.ShapeDtypeStruct((b, s, WIDTH), _BF16),
        jax.ShapeDtypeStruct((b, s, WIDTH), _BF16),
        jax.ShapeDtypeStruct((b, s, WIDTH), _BF16),
        jax.ShapeDtypeStruct((b, s, WIDTH), _BF16),
        jax.ShapeDtypeStruct((b, s, LANES), _F32),
        jax.ShapeDtypeStruct((b, s, WIDTH), _BF16),
    )
    return pl.pallas_call(
        _in_kernel,
        out_shape=out_shape,
        grid=(b, s // tm),
        in_specs=[
            tok(d),
            pl.BlockSpec((None, 1, 1, 3 * d), lambda i, j: (layer, i, 0, 0)),
            const((1, d)),
            const((d, N_COLS)),
            const((1, Q_RANK)),
            const((Q_RANK, 2 * WIDTH)),
            const((1, KV_RANK)),
            const((KV_RANK, 2 * WIDTH)),
            tok(LANES),
            const((CONV_K, 3 * WIDTH)),
            const((1, LANES)),
            const((1, LANES)),
        ],
        out_specs=(head_t, head, head_vt, tok(WIDTH), tok(WIDTH), tok(WIDTH), tok(WIDTH),
                   tok(LANES), tok(WIDTH)),
        scratch_shapes=[pltpu.VMEM((tm + 8, 3 * WIDTH), _F32)],
        compiler_params=pltpu.CompilerParams(dimension_semantics=("arbitrary", "arbitrary"),
                                             vmem_limit_bytes=VMEM_LIMIT),
        name="in_proj",
    )(x, mod, prew, win, qnw, qup, kvnw, kvup, tab, convw, alog, dtb)


def _attn_kernel(qt_ref, k_ref, vt_ref, o_ref, sa_ref, sb_ref, mxa_ref, mxb_ref, m_ref, acc_ref):
    tq = qt_ref.shape[3]
    uk = vt_ref.shape[4]
    uq = ATTN_SUB
    nq = tq // uq
    nu = tq // uk
    i = pl.program_id(2)
    qts = [qt_ref[0, 0, :, h * uq:(h + 1) * uq] for h in range(nq)]
    bufs = ((sa_ref, mxa_ref), (sb_ref, mxb_ref))

    m_ref[...] = jnp.full_like(m_ref, _NEG)
    acc_ref[...] = jnp.zeros_like(acc_ref)

    def produce(buf, blk, unit, subs, diagonal=False):
        s_ref, mx_ref = bufs[buf]
        kj = k_ref[0, 0, pl.ds(pl.multiple_of(blk * tq + unit * uk, uk), uk), :]
        for h in subs:
            keys = min(uk, (h - unit * (uk // uq) + 1) * uq) if diagonal else uk
            s = jnp.dot(kj[0:keys], qts[h], preferred_element_type=_F32)
            s_ref[h, 0:keys, :] = s
            mx_ref[h] = jnp.max(s, axis=0, keepdims=True)

    def consume(buf, blk, unit, subs, masked=()):
        s_ref, mx_ref = bufs[buf]
        vt = vt_ref[0, 0, blk * nu + unit]
        for h in subs:
            if h in masked:
                own = h - unit * (uk // uq)
                tri = (lax.broadcasted_iota(jnp.int32, (uq, uq), 0)
                       <= lax.broadcasted_iota(jnp.int32, (uq, uq), 1))
                s = jnp.where(tri, s_ref[h, own * uq:(own + 1) * uq, :], _NEG)
                if own > 0:
                    s = jnp.concatenate([s_ref[h, 0:own * uq, :], s], axis=0)
                mx = jnp.max(s, axis=0, keepdims=True)
                vth = vt_ref[0, 0, blk * nu + unit, :, 0:(own + 1) * uq]
            else:
                s = s_ref[h]
                mx = mx_ref[h]
                vth = vt
            m = m_ref[h]
            m_new = jnp.maximum(m, mx)
            p = jnp.exp2(s - m_new).astype(_BF16)
            acc_ref[h] = jnp.exp2(m - m_new) * acc_ref[h] + jnp.dot(vth, p, preferred_element_type=_F32)
            m_ref[h] = m_new

    every = tuple(range(nq))
    produce(0, 0, 0, every)

    def block(j):
        for unit in range(nu):
            if unit + 1 < nu:
                produce((unit + 1) % 2, j, unit + 1, every)
            else:
                produce((unit + 1) % 2, j + 1, 0, every)
            consume(unit % 2, j, unit, every)

    def pair(p, carry):
        block(2 * p)
        block(2 * p + 1)
        return carry

    lax.fori_loop(0, i // 2, pair, 0)

    @pl.when(i % 2 == 1)
    def _():
        block(i - 1)
    per = uk // uq
    for unit in range(nu):
        if unit + 1 < nu:
            produce((unit + 1) % 2, i, unit + 1, tuple(range((unit + 1) * per, nq)), diagonal=True)
        consume(unit % 2, i, unit, tuple(range(unit * per, nq)),
                masked=tuple(range(unit * per, (unit + 1) * per)))
    for h in range(nq):
        acc = acc_ref[h]
        o_ref[0, h * uq:(h + 1) * uq, :] = (acc[:DV] / acc[DV:DV + 1]).T.astype(o_ref.dtype)


def _attn_call(qt, k, vt):
    b, _, _, s = qt.shape
    vrows, tv = vt.shape[-2:]
    t = ATTN_Q
    nq = t // ATTN_SUB
    return pl.pallas_call(
        _attn_kernel,
        out_shape=jax.ShapeDtypeStruct((b, s, WIDTH), _BF16),
        grid=(b, HEADS, s // t),
        in_specs=[
            pl.BlockSpec((1, 1, DQK, t), lambda bi, h, i: (bi, h, 0, i)),
            pl.BlockSpec((1, 1, s, DQK), lambda bi, h, i: (bi, h, 0, 0)),
            pl.BlockSpec((1, 1, s // tv, vrows, tv), lambda bi, h, i: (bi, h, 0, 0, 0)),
        ],
        out_specs=pl.BlockSpec((1, t, DV), lambda bi, h, i: (bi, i, h)),
        scratch_shapes=[
            pltpu.VMEM((nq, tv, ATTN_SUB), _F32),
            pltpu.VMEM((nq, tv, ATTN_SUB), _F32),
            pltpu.VMEM((nq, 1, ATTN_SUB), _F32),
            pltpu.VMEM((nq, 1, ATTN_SUB), _F32),
            pltpu.VMEM((nq, 1, ATTN_SUB), _F32),
            pltpu.VMEM((nq, vrows, ATTN_SUB), _F32),
        ],
        compiler_params=pltpu.CompilerParams(
            dimension_semantics=("arbitrary", "arbitrary", "arbitrary"), vmem_limit_bytes=VMEM_LIMIT),
        name="mla_attention",
    )(qt, k, vt)


def _cumsum_rows(tril_bf16, x):
    lane = lax.broadcasted_iota(jnp.int32, x.shape, 1)
    hi = x.astype(_BF16).astype(_F32)
    r1 = x - hi
    mid = r1.astype(_BF16).astype(_F32)
    low = r1 - mid
    packed = jnp.where(lane < 8, hi, jnp.where(lane < 16, pltpu.roll(mid, 8, 1), pltpu.roll(low, 16, 1)))
    res = jnp.dot(tril_bf16, packed.astype(_BF16), preferred_element_type=_F32)
    return res + pltpu.roll(res, LANES - 8, 1) + pltpu.roll(res, LANES - 16, 1)


def _gdn_kernel(q_ref, k_ref, v_ref, gb_ref, o_ref, state_ref, gc_ref, n_ref, t_ref, qk_ref, rhs_ref,
                qe_ref, kd_ref, sol_ref, ou_ref, qw_ref, su_ref, sw_ref):
    c = GDN_CHUNK
    n_chunks = q_ref.shape[1] // c
    n_chains = n_chunks * HEADS

    @pl.when(pl.program_id(1) == 0)
    def _():
        state_ref[...] = jnp.zeros_like(state_ref)

    row = lax.broadcasted_iota(jnp.int32, (c, c), 0)
    col = lax.broadcasted_iota(jnp.int32, (c, c), 1)
    incl = col <= row
    xor = row ^ col
    eye = (xor == 0).astype(_F32)
    tril = incl.astype(_BF16)

    for ci in range(n_chunks):
        rows = slice(ci * c, (ci + 1) * c)
        gb = gb_ref[0, rows, :]
        gcum = _cumsum_rows(tril, gb)
        gc_ref[ci] = gcum
        gcum_t = gcum.T
        for hd in range(HEADS):
            ch = ci * HEADS + hd
            lo = hd * DV
            q = q_ref[0, rows, lo:lo + DV].astype(_F32)
            k = k_ref[0, rows, lo:lo + DV].astype(_F32)
            v = v_ref[0, rows, lo:lo + DV].astype(_F32)
            beta = jnp.broadcast_to(gb[:, HEADS + hd:HEADS + hd + 1], (c, DV))
            g_col = jnp.broadcast_to(gcum[:, hd:hd + 1], (c, DV))
            g_row = gcum_t[hd:hd + 1, :]
            g_last = gcum[c - 1:c, hd:hd + 1]
            decay = jnp.exp(jnp.where(incl, jnp.concatenate([g_col, g_col], axis=1) - g_row, _NEG))
            e_col = jnp.exp(g_col)
            kb = k * beta
            n_neg = _dot_nt(-kb, k) * decay
            n_ref[ch] = n_neg
            t_ref[ch] = jnp.where(xor == 1, n_neg, eye)
            qk_ref[ch] = (_dot_nt(q, k) * decay).astype(qk_ref.dtype)
            rhs_ref[ch] = jnp.concatenate([v * beta, kb * e_col], axis=1).astype(rhs_ref.dtype)
            qe_ref[ch] = (q * e_col).astype(qe_ref.dtype)
            kd_ref[ch] = (k * jnp.exp(g_last - g_col)).astype(kd_ref.dtype)

    m = 2
    while m < c:
        blk = lax.broadcasted_iota(jnp.int32, (c, c), 0) ^ lax.broadcasted_iota(jnp.int32, (c, c), 1)
        mask = (blk >= m) & (blk < 2 * m)
        size = m if m % 8 == 0 else c
        odd = [slice(b * m, (b + 1) * m) for b in range(1, c // m, 2)] if m % 8 == 0 else [slice(0, c)]

        def rows_of(ch, odd=odd):
            return jnp.concatenate([t_ref[ch, r, :] for r in odd], axis=0)

        left = []
        for ch in range(n_chains):
            left.append(_dot(rows_of(ch), jnp.where(mask, n_ref[ch], 0.0)))
        for ch in range(n_chains):
            upd = rows_of(ch) + _dot(left[ch], t_ref[ch])
            for j, r in enumerate(odd):
                t_ref[ch, r, :] = upd[j * size:(j + 1) * size]
        m *= 2

    for ch in range(n_chains):
        sol_ref[ch] = _dot(t_ref[ch], rhs_ref[ch]).astype(sol_ref.dtype)
    for ch in range(n_chains):
        qk_sol = _dot(qk_ref[ch], sol_ref[ch])
        ou_ref[ch] = qk_sol[:, :DV]
        qw_ref[ch] = (qe_ref[ch].astype(_F32) - qk_sol[:, DV:]).astype(qw_ref.dtype)
    for ch in range(n_chains):
        kd_sol = _dot_tn(kd_ref[ch], sol_ref[ch])
        su_ref[ch] = kd_sol[:, :DV]
        sw_ref[ch] = kd_sol[:, DV:].astype(sw_ref.dtype)
    for ci in range(n_chunks):
        rows = slice(ci * c, (ci + 1) * c)
        for hd in range(HEADS):
            ch = ci * HEADS + hd
            lo = hd * DV
            g_last = gc_ref[ci, c - 1:c, hd:hd + 1]
            state = state_ref[hd]
            o_ref[0, rows, lo:lo + DV] = (_dot(qw_ref[ch], state) + ou_ref[ch]).astype(o_ref.dtype)
            state_ref[hd] = state * jnp.exp(g_last) + su_ref[ch] - _dot(sw_ref[ch], state)


def _gdn_call(gq, gk, gv, gb):
    b, s, _ = gq.shape
    t = GDN_TILE
    n_chains = (t // GDN_CHUNK) * HEADS
    tok = lambda w: pl.BlockSpec((1, t, w), lambda i, j: (i, j, 0))
    return pl.pallas_call(
        _gdn_kernel,
        out_shape=jax.ShapeDtypeStruct((b, s, WIDTH), _BF16),
        grid=(b, s // t),
        in_specs=[tok(WIDTH), tok(WIDTH), tok(WIDTH), tok(LANES)],
        out_specs=tok(WIDTH),
        scratch_shapes=[
            pltpu.VMEM((HEADS, DV, DV), _F32),
            pltpu.VMEM((t // GDN_CHUNK, GDN_CHUNK, LANES), _F32),
            pltpu.VMEM((n_chains, GDN_CHUNK, GDN_CHUNK), _F32),
            pltpu.VMEM((n_chains, GDN_CHUNK, GDN_CHUNK), _F32),
            pltpu.VMEM((n_chains, GDN_CHUNK, GDN_CHUNK), _BF16),
            pltpu.VMEM((n_chains, GDN_CHUNK, 2 * DV), _BF16),
            pltpu.VMEM((n_chains, GDN_CHUNK, DV), _BF16),
            pltpu.VMEM((n_chains, GDN_CHUNK, DV), _BF16),
            pltpu.VMEM((n_chains, GDN_CHUNK, 2 * DV), _BF16),
            pltpu.VMEM((n_chains, GDN_CHUNK, DV), _F32),
            pltpu.VMEM((n_chains, GDN_CHUNK, DV), _BF16),
            pltpu.VMEM((n_chains, DV, DV), _F32),
            pltpu.VMEM((n_chains, DV, DV), _BF16),
        ],
        compiler_params=pltpu.CompilerParams(dimension_semantics=("arbitrary", "arbitrary"),
                                             vmem_limit_bytes=VMEM_LIMIT),
        name="gated_delta_rule",
    )(gq, gk, gv, gb)


def _out_kernel(x_ref, mod_ref, om_ref, zm_ref, og_ref, zg_ref, onw_ref, wout_ref, postw_ref, o_ref):
    d = x_ref.shape[-1]
    tm = x_ref.shape[1]
    w_post = mod_ref[0][:, 2 * d:3 * d] * postw_ref[...]
    for r in range(0, tm, OUT_ROWS):
        rows = slice(r, r + OUT_ROWS)
        y_mla = om_ref[0, rows, :].astype(_F32) * _silu(zm_ref[0, rows, :].astype(_F32))
        og = og_ref[0, rows, :].astype(_F32)
        zg = _silu(zg_ref[0, rows, :].astype(_F32))
        parts = [y_mla]
        for hd in range(HEADS):
            lo = hd * DV
            parts.append(_rms(og[:, lo:lo + DV], onw_ref[...]) * zg[:, lo:lo + DV])
        y = _dot(jnp.concatenate(parts, axis=1), wout_ref[...])
        y_n = y * lax.rsqrt(jnp.mean(y * y, axis=-1, keepdims=True) + EPS)
        o_ref[0, rows, :] = x_ref[0, rows, :] + y_n * w_post


def _out_call(layer, x, mod, o_mla, z_mla, o_gdn, z_gdn, onw, wout, postw):
    b, s, d = x.shape
    tm = min(TOKEN_TILE, s)
    const = lambda shape: pl.BlockSpec((None,) + shape, lambda i, j: (layer,) + (0,) * len(shape))
    tok = lambda w: pl.BlockSpec((1, tm, w), lambda i, j: (i, j, 0))
    return pl.pallas_call(
        _out_kernel,
        out_shape=jax.ShapeDtypeStruct((b, s, d), _F32),
        grid=(b, s // tm),
        in_specs=[
            tok(d),
            pl.BlockSpec((None, 1, 1, 3 * d), lambda i, j: (layer, i, 0, 0)),
            tok(WIDTH), tok(WIDTH), tok(WIDTH), tok(WIDTH),
            const((1, DV)),
            const((2 * WIDTH, d)),
            const((1, d)),
        ],
        out_specs=tok(d),
        compiler_params=pltpu.CompilerParams(dimension_semantics=("arbitrary", "arbitrary"),
                                             vmem_limit_bytes=VMEM_LIMIT),
        name="out_proj",
    )(x, mod, o_mla, z_mla, o_gdn, z_gdn, onw, wout, postw)


def _rot_cols(w):
    half = w.shape[-1] // 2
    return jnp.concatenate([-w[..., half:], w[..., :half]], axis=-1)


def _pad_cols(w, n):
    return jnp.pad(w, [(0, 0)] * (w.ndim - 1) + [(0, n - w.shape[-1])])


def _prep_w_in(w):
    w = w.astype(_BF16)
    o_kv = Q_RANK
    o_kpe = o_kv + KV_RANK
    o_zm = o_kpe + ROPE
    o_qkv = o_zm + WIDTH
    o_a = o_qkv + 3 * WIDTH
    o_zg = o_a + 2 * HEADS
    k_pe = w[..., o_kpe:o_zm]
    return jnp.concatenate([
        w[..., :o_kpe], k_pe, _rot_cols(k_pe), w[..., o_zm:o_qkv], w[..., o_qkv:o_a],
        _pad_cols(w[..., o_a:o_zg], LANES), w[..., o_zg:],
    ], axis=-1)


def _prep_q_up(w):
    depth = w.shape[0]
    per = w.astype(_BF16).reshape(depth, Q_RANK, HEADS, DQK)
    nope = per[..., :NOPE].reshape(depth, Q_RANK, HEADS * NOPE)
    pe = [jnp.concatenate([per[:, :, h, NOPE:], _rot_cols(per[:, :, h, NOPE:])], axis=-1) for h in range(HEADS)]
    return jnp.concatenate([nope] + pe, axis=-1)


def _prep_kv_up(w):
    depth = w.shape[0]
    per = w.astype(_BF16).reshape(depth, KV_RANK, HEADS, NOPE + DV)
    return jnp.concatenate([per[..., :NOPE].reshape(depth, KV_RANK, HEADS * NOPE),
                            per[..., NOPE:].reshape(depth, KV_RANK, HEADS * DV)], axis=-1)


def _lane_rows(vals):
    return _pad_cols(vals.astype(_F32), LANES)[:, None, :]


def kernel(x, c, positions, w_mod, b_mod, pre_norm_w, post_norm_w, w_in, mla_q_norm_w, mla_q_up,
           mla_kv_norm_w, mla_kv_up, gdn_conv_w, gdn_a_log, gdn_dt_bias, gdn_o_norm_w, w_out):
    depth = w_mod.shape[0]
    b, s, d = x.shape
    assert s % TOKEN_TILE == 0 and s % GDN_TILE == 0 and s % ATTN_Q == 0
    assert TOKEN_TILE % ATTN_SUB == 0 and ATTN_Q % TOKEN_TILE == 0 and (ATTN_Q // ATTN_SUB) % 2 == 0
    mod = _modulation(c, w_mod, b_mod).reshape(depth, b, 1, 3 * d)
    tab = _rope_table(positions)
    in_params = (pre_norm_w[:, None, :], _prep_w_in(w_in), mla_q_norm_w[:, None, :], _prep_q_up(mla_q_up),
                 mla_kv_norm_w[:, None, :], _prep_kv_up(mla_kv_up))
    gdn_params = (gdn_conv_w, _lane_rows(gdn_a_log), _lane_rows(gdn_dt_bias))
    out_params = (gdn_o_norm_w[:, None, :], w_out.astype(_BF16), post_norm_w[:, None, :])
    for l in range(depth):
        q, k, v, z_mla, gq, gk, gv, gb, z_gdn = _in_call(l, x, mod, *in_params, tab, *gdn_params)
        o_mla = _attn_call(q, k, v)
        o_gdn = _gdn_call(gq, gk, gv, gb)
        x = _out_call(l, x, mod, o_mla, z_mla, o_gdn, z_gdn, *out_params)
    return x
```

```python
import functools

import jax
import jax.numpy as jnp
import numpy as np
from jax import lax
from jax.experimental import pallas as pl
from jax.experimental.pallas import tpu as pltpu

HEADS = 4
NOPE = 128
ROPE = 64
DQK = NOPE + ROPE
DV = 128
Q_RANK = 384
KV_RANK = 256
WIDTH = HEADS * DV
CONV_K = 4
ROPE_THETA = 10000.0
EPS = 1e-6

LANES = 128

C_QLAT = 0
C_KVLAT = C_QLAT + Q_RANK
C_KPE = C_KVLAT + KV_RANK
C_ZMLA = C_KPE + LANES
C_QKV = C_ZMLA + WIDTH
C_AB = C_QKV + 3 * WIDTH
C_ZGDN = C_AB + LANES
N_COLS = C_ZGDN + WIDTH

TOKEN_TILE = 512
ATTN_Q = 1024
ATTN_SUB = 256
ATTN_HEADS = 1
VT_ROWS = DV + 16
GDN_CHUNK = 256
GDN_TILE = 512
OUT_ROWS = 256
ROPE_TILE = 1024
VMEM_LIMIT = 56 * 1024 * 1024

_F32 = jnp.float32
_BF16 = jnp.bfloat16
_NEG = -1e30
LOG2E = float(np.log2(np.e))


def _sigmoid(x):
    return 1.0 / (1.0 + jnp.exp2(x * -LOG2E))


def _silu(x):
    return x * _sigmoid(x)


def _softplus(x):
    return jnp.maximum(x, 0.0) + jnp.log(1.0 + jnp.exp(-jnp.abs(x)))


def _rms(x, w):
    return x * lax.rsqrt(jnp.mean(x * x, axis=-1, keepdims=True) + EPS) * w


def _dot(a, b):
    return jnp.dot(a.astype(_BF16), b.astype(_BF16), preferred_element_type=_F32)


def _dot_nt(a, b):
    return lax.dot_general(a.astype(_BF16), b.astype(_BF16), (((1,), (1,)), ((), ())),
                           preferred_element_type=_F32)


def _dot_tn(a, b):
    return lax.dot_general(a.astype(_BF16), b.astype(_BF16), (((0,), (0,)), ((), ())),
                           preferred_element_type=_F32)


def _mod_kernel(c_ref, w_ref, b_ref, o_ref):
    c_act = _silu(c_ref[...])
    o_ref[0] = jnp.dot(c_act, w_ref[0], preferred_element_type=_F32,
                       precision=lax.Precision.HIGHEST) + b_ref[0]


def _modulation(c, w_mod, b_mod):
    depth, d, d3 = w_mod.shape
    b = c.shape[0]
    return pl.pallas_call(
        _mod_kernel,
        out_shape=jax.ShapeDtypeStruct((depth, b, d3), _F32),
        grid=(depth, d3 // d),
        in_specs=[
            pl.BlockSpec((b, d), lambda l, j: (0, 0)),
            pl.BlockSpec((1, d, d), lambda l, j: (l, 0, j)),
            pl.BlockSpec((1, 1, d), lambda l, j: (l, 0, j)),
        ],
        out_specs=pl.BlockSpec((1, b, d), lambda l, j: (l, 0, j)),
        compiler_params=pltpu.CompilerParams(dimension_semantics=("arbitrary", "arbitrary")),
        name="modulation",
    )(c, w_mod, b_mod.reshape(depth, 1, d3))


def _rope_kernel(pos_ref, freq_ref, o_ref):
    ang = pos_ref[0].astype(_F32) * freq_ref[...]
    lane = lax.broadcasted_iota(jnp.int32, ang.shape, 1)
    o_ref[0] = jnp.where(lane < ROPE, jnp.cos(ang), jnp.sin(ang))


def _rope_table(positions):
    b, s = positions.shape
    half = ROPE // 2
    inv_freq = jnp.power(ROPE_THETA, -jnp.arange(half, dtype=_F32) * 2.0 / ROPE)
    freq = jnp.tile(inv_freq, LANES // half).reshape(1, LANES)
    t = min(ROPE_TILE, s)
    return pl.pallas_call(
        _rope_kernel,
        out_shape=jax.ShapeDtypeStruct((b, s, LANES), _F32),
        grid=(b, s // t),
        in_specs=[
            pl.BlockSpec((1, t, 1), lambda i, j: (i, j, 0)),
            pl.BlockSpec((1, LANES), lambda i, j: (0, 0)),
        ],
        out_specs=pl.BlockSpec((1, t, LANES), lambda i, j: (i, j, 0)),
        compiler_params=pltpu.CompilerParams(dimension_semantics=("arbitrary", "arbitrary")),
        name="rope_table",
    )(positions.reshape(b, s, 1), freq)


def _in_kernel(x_ref, mod_ref, prew_ref, win_ref, qnw_ref, qup_ref, kvnw_ref, kvup_ref, tab_ref,
               convw_ref, alog_ref, dtb_ref,
               qt_ref, k_ref, vt_ref, zm_ref, gq_ref, gk_ref, gv_ref, gb_ref, zg_ref,
               tail_ref):
    d = x_ref.shape[-1]
    tm = x_ref.shape[1]

    @pl.when(pl.program_id(1) == 0)
    def _():
        tail_ref[tm:tm + 8, :] = jnp.zeros((8, tail_ref.shape[1]), tail_ref.dtype)

    x = x_ref[0]
    mod = mod_ref[0]
    w_eff = prew_ref[...] * (1.0 + mod[:, d:2 * d])
    h = (x * lax.rsqrt(jnp.mean(x * x, axis=-1, keepdims=True) + EPS) * w_eff + mod[:, 0:d]).astype(_BF16)

    def project(lo, hi):
        return jnp.dot(h, win_ref[:, lo:hi], preferred_element_type=_F32)

    proj = project(0, C_ZMLA)

    tab = tab_ref[0]
    q_scale = DQK ** -0.5 * LOG2E

    def rope(block):
        y = block * tab
        return y + pltpu.roll(y, ROPE, 1)

    qf = _dot(_rms(proj[:, C_QLAT:C_QLAT + Q_RANK], qnw_ref[...]), qup_ref[...])
    kvf = _dot(_rms(proj[:, C_KVLAT:C_KVLAT + KV_RANK], kvnw_ref[...]), kvup_ref[...])
    k_pe = rope(proj[:, C_KPE:C_KPE + LANES])[:, :ROPE].astype(k_ref.dtype)
    for hd in range(HEADS):
        lo = hd * NOPE
        q_pe = rope(qf[:, WIDTH + hd * LANES:WIDTH + (hd + 1) * LANES])
        qt_ref[0, hd, 0:NOPE, :] = (qf[:, lo:lo + NOPE] * q_scale).T.astype(qt_ref.dtype)
        qt_ref[0, hd, NOPE:DQK, :] = (q_pe * q_scale).T[:ROPE, :].astype(qt_ref.dtype)
        k_ref[0, hd, :, 0:NOPE] = kvf[:, lo:lo + NOPE].astype(k_ref.dtype)
        k_ref[0, hd, :, NOPE:DQK] = k_pe
        vt_ref[0, hd, 0, 0:DV, :] = kvf[:, WIDTH + lo:WIDTH + lo + DV].T.astype(vt_ref.dtype)
        vt_ref[0, hd, 0, DV:, :] = jnp.ones((VT_ROWS - DV, tm), vt_ref.dtype)
    zm_ref[0] = project(C_ZMLA, C_QKV).astype(zm_ref.dtype)

    for part, out_ref in enumerate((gq_ref, gk_ref, gv_ref)):
        cols = slice(part * WIDTH, (part + 1) * WIDTH)
        xg = project(C_QKV + part * WIDTH, C_QKV + (part + 1) * WIDTH)
        tail_ref[0:8, cols] = tail_ref[tm:tm + 8, cols]
        tail_ref[8:8 + tm, cols] = xg
        cw = convw_ref[:, cols]
        conv = xg * cw[CONV_K - 1:CONV_K, :]
        for j in range(CONV_K - 1):
            start = 8 - (CONV_K - 1 - j)
            conv = conv + tail_ref[start:start + tm, cols] * cw[j:j + 1, :]
        act = _silu(conv)
        if part == 2:
            out_ref[0] = act.astype(out_ref.dtype)
        else:
            scale = DV ** -0.5 if part == 0 else 1.0
            for hd in range(HEADS):
                lo = hd * DV
                a = act[:, lo:lo + DV]
                inv = lax.rsqrt(jnp.sum(a * a, axis=-1, keepdims=True) + EPS) * scale
                out_ref[0, :, lo:lo + DV] = (a * inv).astype(out_ref.dtype)

    gates = project(C_AB, N_COLS)
    ab = gates[:, 0:LANES]
    g = -jnp.exp(alog_ref[...]) * _softplus(ab + dtb_ref[...])
    lane = lax.broadcasted_iota(jnp.int32, ab.shape, 1)
    gb_ref[0] = jnp.where(lane < HEADS, g, _sigmoid(ab))
    zg_ref[0] = gates[:, C_ZGDN - C_AB:].astype(zg_ref.dtype)


def _in_call(layer, x, mod, prew, win, qnw, qup, kvnw, kvup, tab, convw, alog, dtb):
    b, s, d = x.shape
    tm = min(TOKEN_TILE, s)
    const = lambda shape: pl.BlockSpec((None,) + shape, lambda i, j: (layer,) + (0,) * len(shape))
    tok = lambda w: pl.BlockSpec((1, tm, w), lambda i, j: (i, j, 0))
    head = pl.BlockSpec((1, HEADS, tm, DQK), lambda i, j: (i, 0, j, 0))
    head_t = pl.BlockSpec((1, HEADS, DQK, tm), lambda i, j: (i, 0, 0, j))
    head_vt = pl.BlockSpec((1, HEADS, 1, VT_ROWS, tm), lambda i, j: (i, 0, j, 0, 0))
    out_shape = (
        jax.ShapeDtypeStruct((b, HEADS, DQK, s), _BF16),
        jax.ShapeDtypeStruct((b, HEADS, s, DQK), _BF16),
        jax.ShapeDtypeStruct((b, HEADS, s // tm, VT_ROWS, tm), _BF16),
        jax.ShapeDtypeStruct((b, s, WIDTH), _BF16),
        jax.ShapeDtypeStruct((b, s, WIDTH), _BF16),
        jax.ShapeDtypeStruct((b, s, WIDTH), _BF16),
        jax.ShapeDtypeStruct((b, s, WIDTH), _BF16),
        jax.ShapeDtypeStruct((b, s, LANES), _F32),
        jax.ShapeDtypeStruct((b, s, WIDTH), _BF16),
    )
    return pl.pallas_call(
        _in_kernel,
        out_shape=out_shape,
        grid=(b, s // tm),
        in_specs=[
            tok(d),
            pl.BlockSpec((None, 1, 1, 3 * d), lambda i, j: (layer, i, 0, 0)),
            const((1, d)),
            const((d, N_COLS)),
            const((1, Q_RANK)),
            const((Q_RANK, 2 * WIDTH)),
            const((1, KV_RANK)),
            const((KV_RANK, 2 * WIDTH)),
            tok(LANES),
            const((CONV_K, 3 * WIDTH)),
            const((1, LANES)),
            const((1, LANES)),
        ],
        out_specs=(head_t, head, head_vt, tok(WIDTH), tok(WIDTH), tok(WIDTH), tok(WIDTH),
                   tok(LANES), tok(WIDTH)),
        scratch_shapes=[pltpu.VMEM((tm + 8, 3 * WIDTH), _F32)],
        compiler_params=pltpu.CompilerParams(dimension_semantics=("arbitrary", "arbitrary"),
                                             vmem_limit_bytes=VMEM_LIMIT),
        name="in_proj",
    )(x, mod, prew, win, qnw, qup, kvnw, kvup, tab, convw, alog, dtb)


def _attn_kernel(qt_ref, k_ref, vt_ref, z_ref, o_ref, sa_ref, sb_ref, mxa_ref, mxb_ref, m_ref, acc_ref):
    tq = qt_ref.shape[3]
    uk = vt_ref.shape[4]
    uq = ATTN_SUB
    nq = tq // uq
    nu = tq // uk
    i = pl.program_id(2)
    heads = range(qt_ref.shape[1])
    qts = [[qt_ref[0, g, :, h * uq:(h + 1) * uq] for h in range(nq)] for g in heads]
    bufs = ((sa_ref, mxa_ref), (sb_ref, mxb_ref))

    m_ref[...] = jnp.full_like(m_ref, _NEG)
    acc_ref[...] = jnp.zeros_like(acc_ref)

    def produce(buf, blk, unit, subs, diagonal=False):
        s_ref, mx_ref = bufs[buf]
        for g in heads:
            kj = k_ref[0, g, pl.ds(pl.multiple_of(blk * tq + unit * uk, uk), uk), :]
            for h in subs:
                keys = min(uk, (h - unit * (uk // uq) + 1) * uq) if diagonal else uk
                s = jnp.dot(kj[0:keys], qts[g][h], preferred_element_type=_F32)
                s_ref[g * nq + h, 0:keys, :] = s
                mx_ref[g * nq + h] = jnp.max(s, axis=0, keepdims=True)

    def consume(buf, blk, unit, subs, masked=()):
        s_ref, mx_ref = bufs[buf]
        for g in heads:
            vt = vt_ref[0, g, blk * nu + unit]
            for h in subs:
                f = g * nq + h
                if h in masked:
                    own = h - unit * (uk // uq)
                    tri = (lax.broadcasted_iota(jnp.int32, (uq, uq), 0)
                           <= lax.broadcasted_iota(jnp.int32, (uq, uq), 1))
                    s = jnp.where(tri, s_ref[f, own * uq:(own + 1) * uq, :], _NEG)
                    if own > 0:
                        s = jnp.concatenate([s_ref[f, 0:own * uq, :], s], axis=0)
                    mx = jnp.max(s, axis=0, keepdims=True)
                    vth = vt_ref[0, g, blk * nu + unit, :, 0:(own + 1) * uq]
                else:
                    s = s_ref[f]
                    mx = mx_ref[f]
                    vth = vt
                m = m_ref[f]
                m_new = jnp.maximum(m, mx)
                p = jnp.exp2(s - m_new).astype(_BF16)
                acc_ref[f] = jnp.exp2(m - m_new) * acc_ref[f] + jnp.dot(vth, p, preferred_element_type=_F32)
                m_ref[f] = m_new

    every = tuple(range(nq))
    produce(0, 0, 0, every)

    def block(j):
        for unit in range(nu):
            if unit + 1 < nu:
                produce((unit + 1) % 2, j, unit + 1, every)
            else:
                produce((unit + 1) % 2, j + 1, 0, every)
            consume(unit % 2, j, unit, every)

    def pair(p, carry):
        block(2 * p)
        block(2 * p + 1)
        return carry

    lax.fori_loop(0, i // 2, pair, 0)

    @pl.when(i % 2 == 1)
    def _():
        block(i - 1)
    per = uk // uq
    for unit in range(nu):
        if unit + 1 < nu:
            produce((unit + 1) % 2, i, unit + 1, tuple(range((unit + 1) * per, nq)), diagonal=True)
        consume(unit % 2, i, unit, tuple(range(unit * per, nq)),
                masked=tuple(range(unit * per, (unit + 1) * per)))
    for g in heads:
        for h in range(nq):
            acc = acc_ref[g * nq + h]
            rows, cols = slice(h * uq, (h + 1) * uq), slice(g * DV, (g + 1) * DV)
            gate = _silu(z_ref[0, rows, cols].astype(_F32))
            o_ref[0, rows, cols] = ((acc[:DV] / acc[DV:DV + 1]).T * gate).astype(o_ref.dtype)


def _attn_call(qt, k, vt, z):
    b, _, _, s = qt.shape
    vrows, tv = vt.shape[-2:]
    t = ATTN_Q
    g = ATTN_HEADS
    nq = g * (t // ATTN_SUB)
    return pl.pallas_call(
        _attn_kernel,
        out_shape=jax.ShapeDtypeStruct((b, s, WIDTH), _BF16),
        grid=(b, HEADS // g, s // t),
        in_specs=[
            pl.BlockSpec((1, g, DQK, t), lambda bi, h, i: (bi, h, 0, i)),
            pl.BlockSpec((1, g, s, DQK), lambda bi, h, i: (bi, h, 0, 0)),
            pl.BlockSpec((1, g, s // tv, vrows, tv), lambda bi, h, i: (bi, h, 0, 0, 0)),
            pl.BlockSpec((1, t, g * DV), lambda bi, h, i: (bi, i, h)),
        ],
        out_specs=pl.BlockSpec((1, t, g * DV), lambda bi, h, i: (bi, i, h)),
        scratch_shapes=[
            pltpu.VMEM((nq, tv, ATTN_SUB), _F32),
            pltpu.VMEM((nq, tv, ATTN_SUB), _F32),
            pltpu.VMEM((nq, 1, ATTN_SUB), _F32),
            pltpu.VMEM((nq, 1, ATTN_SUB), _F32),
            pltpu.VMEM((nq, 1, ATTN_SUB), _F32),
            pltpu.VMEM((nq, vrows, ATTN_SUB), _F32),
        ],
        compiler_params=pltpu.CompilerParams(
            dimension_semantics=("arbitrary", "arbitrary", "arbitrary"), vmem_limit_bytes=VMEM_LIMIT),
        name="mla_attention",
    )(qt, k, vt, z)


def _cumsum_rows(tril_bf16, x):
    lane = lax.broadcasted_iota(jnp.int32, x.shape, 1)
    hi = x.astype(_BF16).astype(_F32)
    r1 = x - hi
    mid = r1.astype(_BF16).astype(_F32)
    low = r1 - mid
    packed = jnp.where(lane < 8, hi, jnp.where(lane < 16, pltpu.roll(mid, 8, 1), pltpu.roll(low, 16, 1)))
    res = jnp.dot(tril_bf16, packed.astype(_BF16), preferred_element_type=_F32)
    return res + pltpu.roll(res, LANES - 8, 1) + pltpu.roll(res, LANES - 16, 1)


def _gdn_kernel(q_ref, k_ref, v_ref, gb_ref, z_ref, onw_ref, o_ref, state_ref, gc_ref, n_ref, t_ref, qk_ref,
                rhs_ref, qe_ref, kd_ref, sol_ref, ou_ref, qw_ref, su_ref, sw_ref):
    c = GDN_CHUNK
    n_chunks = q_ref.shape[1] // c
    n_chains = n_chunks * HEADS

    @pl.when(pl.program_id(1) == 0)
    def _():
        state_ref[...] = jnp.zeros_like(state_ref)

    row = lax.broadcasted_iota(jnp.int32, (c, c), 0)
    col = lax.broadcasted_iota(jnp.int32, (c, c), 1)
    incl = col <= row
    xor = row ^ col
    eye = (xor == 0).astype(_F32)
    tril = incl.astype(_BF16)

    for ci in range(n_chunks):
        rows = slice(ci * c, (ci + 1) * c)
        gb = gb_ref[0, rows, :]
        gcum = _cumsum_rows(tril, gb)
        gc_ref[ci] = gcum
        gcum_t = gcum.T
        for hd in range(HEADS):
            ch = ci * HEADS + hd
            lo = hd * DV
            q = q_ref[0, rows, lo:lo + DV].astype(_F32)
            k = k_ref[0, rows, lo:lo + DV].astype(_F32)
            v = v_ref[0, rows, lo:lo + DV].astype(_F32)
            beta = jnp.broadcast_to(gb[:, HEADS + hd:HEADS + hd + 1], (c, DV))
            g_col = jnp.broadcast_to(gcum[:, hd:hd + 1], (c, DV))
            g_row = gcum_t[hd:hd + 1, :]
            g_last = gcum[c - 1:c, hd:hd + 1]
            decay = jnp.exp(jnp.where(incl, jnp.concatenate([g_col, g_col], axis=1) - g_row, _NEG))
            e_col = jnp.exp(g_col)
            kb = k * beta
            n_neg = _dot_nt(-kb, k) * decay
            n_ref[ch] = n_neg
            t_ref[ch] = jnp.where(xor == 1, n_neg, eye)
            qk_ref[ch] = (_dot_nt(q, k) * decay).astype(qk_ref.dtype)
            rhs_ref[ch] = jnp.concatenate([v * beta, kb * e_col], axis=1).astype(rhs_ref.dtype)
            qe_ref[ch] = (q * e_col).astype(qe_ref.dtype)
            kd_ref[ch] = (k * jnp.exp(g_last - g_col)).astype(kd_ref.dtype)

    m = 2
    while m < c:
        blk = lax.broadcasted_iota(jnp.int32, (c, c), 0) ^ lax.broadcasted_iota(jnp.int32, (c, c), 1)
        mask = (blk >= m) & (blk < 2 * m)
        size = m if m % 8 == 0 else c
        odd = [slice(b * m, (b + 1) * m) for b in range(1, c // m, 2)] if m % 8 == 0 else [slice(0, c)]

        def rows_of(ch, odd=odd):
            return jnp.concatenate([t_ref[ch, r, :] for r in odd], axis=0)

        left = []
        for ch in range(n_chains):
            left.append(_dot(rows_of(ch), jnp.where(mask, n_ref[ch], 0.0)))
        for ch in range(n_chains):
            upd = rows_of(ch) + _dot(left[ch], t_ref[ch])
            for j, r in enumerate(odd):
                t_ref[ch, r, :] = upd[j * size:(j + 1) * size]
        m *= 2

    for ch in range(n_chains):
        sol_ref[ch] = _dot(t_ref[ch], rhs_ref[ch]).astype(sol_ref.dtype)
    for ch in range(n_chains):
        qk_sol = _dot(qk_ref[ch], sol_ref[ch])
        ou_ref[ch] = qk_sol[:, :DV]
        qw_ref[ch] = (qe_ref[ch].astype(_F32) - qk_sol[:, DV:]).astype(qw_ref.dtype)
    for ch in range(n_chains):
        kd_sol = _dot_tn(kd_ref[ch], sol_ref[ch])
        su_ref[ch] = kd_sol[:, :DV]
        sw_ref[ch] = kd_sol[:, DV:].astype(sw_ref.dtype)
    for ci in range(n_chunks):
        rows = slice(ci * c, (ci + 1) * c)
        for hd in range(HEADS):
            ch = ci * HEADS + hd
            lo = hd * DV
            g_last = gc_ref[ci, c - 1:c, hd:hd + 1]
            state = state_ref[hd]
            o = _dot(qw_ref[ch], state) + ou_ref[ch]
            gate = _silu(z_ref[0, rows, lo:lo + DV].astype(_F32))
            o_ref[0, rows, lo:lo + DV] = (_rms(o, onw_ref[...]) * gate).astype(o_ref.dtype)
            state_ref[hd] = state * jnp.exp(g_last) + su_ref[ch] - _dot(sw_ref[ch], state)


def _gdn_call(layer, gq, gk, gv, gb, z, onw):
    b, s, _ = gq.shape
    t = GDN_TILE
    n_chains = (t // GDN_CHUNK) * HEADS
    tok = lambda w: pl.BlockSpec((1, t, w), lambda i, j: (i, j, 0))
    return pl.pallas_call(
        _gdn_kernel,
        out_shape=jax.ShapeDtypeStruct((b, s, WIDTH), _BF16),
        grid=(b, s // t),
        in_specs=[tok(WIDTH), tok(WIDTH), tok(WIDTH), tok(LANES), tok(WIDTH),
                  pl.BlockSpec((None, 1, DV), lambda i, j: (layer, 0, 0))],
        out_specs=tok(WIDTH),
        scratch_shapes=[
            pltpu.VMEM((HEADS, DV, DV), _F32),
            pltpu.VMEM((t // GDN_CHUNK, GDN_CHUNK, LANES), _F32),
            pltpu.VMEM((n_chains, GDN_CHUNK, GDN_CHUNK), _F32),
            pltpu.VMEM((n_chains, GDN_CHUNK, GDN_CHUNK), _F32),
            pltpu.VMEM((n_chains, GDN_CHUNK, GDN_CHUNK), _BF16),
            pltpu.VMEM((n_chains, GDN_CHUNK, 2 * DV), _BF16),
            pltpu.VMEM((n_chains, GDN_CHUNK, DV), _BF16),
            pltpu.VMEM((n_chains, GDN_CHUNK, DV), _BF16),
            pltpu.VMEM((n_chains, GDN_CHUNK, 2 * DV), _BF16),
            pltpu.VMEM((n_chains, GDN_CHUNK, DV), _F32),
            pltpu.VMEM((n_chains, GDN_CHUNK, DV), _BF16),
            pltpu.VMEM((n_chains, DV, DV), _F32),
            pltpu.VMEM((n_chains, DV, DV), _BF16),
        ],
        compiler_params=pltpu.CompilerParams(dimension_semantics=("arbitrary", "arbitrary"),
                                             vmem_limit_bytes=VMEM_LIMIT),
        name="gated_delta_rule",
    )(gq, gk, gv, gb, z, onw)


def _out_kernel(x_ref, mod_ref, om_ref, og_ref, wout_ref, postw_ref, o_ref):
    d = x_ref.shape[-1]
    tm = x_ref.shape[1]
    w_post = mod_ref[0][:, 2 * d:3 * d] * postw_ref[...]
    for r in range(0, tm, OUT_ROWS):
        rows = slice(r, r + OUT_ROWS)
        y = (jnp.dot(om_ref[0, rows, :], wout_ref[0:WIDTH, :], preferred_element_type=_F32)
             + jnp.dot(og_ref[0, rows, :], wout_ref[WIDTH:2 * WIDTH, :], preferred_element_type=_F32))
        y_n = y * lax.rsqrt(jnp.mean(y * y, axis=-1, keepdims=True) + EPS)
        o_ref[0, rows, :] = x_ref[0, rows, :] + y_n * w_post


def _out_call(layer, x, mod, o_mla, o_gdn, wout, postw):
    b, s, d = x.shape
    tm = min(TOKEN_TILE, s)
    const = lambda shape: pl.BlockSpec((None,) + shape, lambda i, j: (layer,) + (0,) * len(shape))
    tok = lambda w: pl.BlockSpec((1, tm, w), lambda i, j: (i, j, 0))
    return pl.pallas_call(
        _out_kernel,
        out_shape=jax.ShapeDtypeStruct((b, s, d), _F32),
        grid=(b, s // tm),
        in_specs=[
            tok(d),
            pl.BlockSpec((None, 1, 1, 3 * d), lambda i, j: (layer, i, 0, 0)),
            tok(WIDTH), tok(WIDTH),
            const((2 * WIDTH, d)),
            const((1, d)),
        ],
        out_specs=tok(d),
        compiler_params=pltpu.CompilerParams(dimension_semantics=("arbitrary", "arbitrary"),
                                             vmem_limit_bytes=VMEM_LIMIT),
        name="out_proj",
    )(x, mod, o_mla, o_gdn, wout, postw)


def _rot_cols(w):
    half = w.shape[-1] // 2
    return jnp.concatenate([-w[..., half:], w[..., :half]], axis=-1)


def _pad_cols(w, n):
    return jnp.pad(w, [(0, 0)] * (w.ndim - 1) + [(0, n - w.shape[-1])])


def _prep_w_in(w):
    w = w.astype(_BF16)
    o_kv = Q_RANK
    o_kpe = o_kv + KV_RANK
    o_zm = o_kpe + ROPE
    o_qkv = o_zm + WIDTH
    o_a = o_qkv + 3 * WIDTH
    o_zg = o_a + 2 * HEADS
    k_pe = w[..., o_kpe:o_zm]
    return jnp.concatenate([
        w[..., :o_kpe], k_pe, _rot_cols(k_pe), w[..., o_zm:o_qkv], w[..., o_qkv:o_a],
        _pad_cols(w[..., o_a:o_zg], LANES), w[..., o_zg:],
    ], axis=-1)


def _prep_q_up(w):
    depth = w.shape[0]
    per = w.astype(_BF16).reshape(depth, Q_RANK, HEADS, DQK)
    nope = per[..., :NOPE].reshape(depth, Q_RANK, HEADS * NOPE)
    pe = [jnp.concatenate([per[:, :, h, NOPE:], _rot_cols(per[:, :, h, NOPE:])], axis=-1) for h in range(HEADS)]
    return jnp.concatenate([nope] + pe, axis=-1)


def _prep_kv_up(w):
    depth = w.shape[0]
    per = w.astype(_BF16).reshape(depth, KV_RANK, HEADS, NOPE + DV)
    return jnp.concatenate([per[..., :NOPE].reshape(depth, KV_RANK, HEADS * NOPE),
                            per[..., NOPE:].reshape(depth, KV_RANK, HEADS * DV)], axis=-1)


def _lane_rows(vals):
    return _pad_cols(vals.astype(_F32), LANES)[:, None, :]


def kernel(x, c, positions, w_mod, b_mod, pre_norm_w, post_norm_w, w_in, mla_q_norm_w, mla_q_up,
           mla_kv_norm_w, mla_kv_up, gdn_conv_w, gdn_a_log, gdn_dt_bias, gdn_o_norm_w, w_out):
    depth = w_mod.shape[0]
    b, s, d = x.shape
    assert s % TOKEN_TILE == 0 and s % GDN_TILE == 0 and s % ATTN_Q == 0
    assert TOKEN_TILE % ATTN_SUB == 0 and ATTN_Q % TOKEN_TILE == 0 and (ATTN_Q // ATTN_SUB) % 2 == 0
    mod = _modulation(c, w_mod, b_mod).reshape(depth, b, 1, 3 * d)
    tab = _rope_table(positions)
    in_params = (pre_norm_w[:, None, :], _prep_w_in(w_in), mla_q_norm_w[:, None, :], _prep_q_up(mla_q_up),
                 mla_kv_norm_w[:, None, :], _prep_kv_up(mla_kv_up))
    gdn_params = (gdn_conv_w, _lane_rows(gdn_a_log), _lane_rows(gdn_dt_bias))
    out_params = (w_out.astype(_BF16), post_norm_w[:, None, :])
    for l in range(depth):
        q, k, v, z_mla, gq, gk, gv, gb, z_gdn = _in_call(l, x, mod, *in_params, tab, *gdn_params)
        o_mla = _attn_call(q, k, v, z_mla)
        o_gdn = _gdn_call(l, gq, gk, gv, gb, z_gdn, gdn_o_norm_w[:, None, :])
        x = _out_call(l, x, mod, o_mla, o_gdn, *out_params)
    return x
```

```python
import functools

import jax
import jax.numpy as jnp
import numpy as np
from jax import lax
from jax.experimental import pallas as pl
from jax.experimental.pallas import tpu as pltpu

HEADS = 4
NOPE = 128
ROPE = 64
DQK = NOPE + ROPE
DV = 128
Q_RANK = 384
KV_RANK = 256
WIDTH = HEADS * DV
CONV_K = 4
ROPE_THETA = 10000.0
EPS = 1e-6

LANES = 128

C_QLAT = 0
C_KVLAT = C_QLAT + Q_RANK
C_KPE = C_KVLAT + KV_RANK
C_ZMLA = C_KPE + LANES
C_QKV = C_ZMLA + WIDTH
C_AB = C_QKV + 3 * WIDTH
C_ZGDN = C_AB + LANES
N_COLS = C_ZGDN + WIDTH

TOKEN_TILE = 512
ATTN_Q = 1024
ATTN_SUB = 256
ATTN_HEADS = 1
VT_ROWS = DV + 16
GDN_CHUNK = 256
GDN_TILE = 512
OUT_ROWS = 256
ROPE_TILE = 1024
VMEM_LIMIT = 56 * 1024 * 1024

_F32 = jnp.float32
_BF16 = jnp.bfloat16
_NEG = -1e30
LOG2E = float(np.log2(np.e))


def _sigmoid(x):
    return 1.0 / (1.0 + jnp.exp2(x * -LOG2E))


def _silu(x):
    h = 0.5 * x
    return h + h * jnp.tanh(h)


def _softplus(x):
    return jnp.maximum(x, 0.0) + jnp.log(1.0 + jnp.exp(-jnp.abs(x)))


def _rms(x, w):
    return x * lax.rsqrt(jnp.mean(x * x, axis=-1, keepdims=True) + EPS) * w


def _dot(a, b):
    return jnp.dot(a.astype(_BF16), b.astype(_BF16), preferred_element_type=_F32)


def _dot_nt(a, b):
    return lax.dot_general(a.astype(_BF16), b.astype(_BF16), (((1,), (1,)), ((), ())),
                           preferred_element_type=_F32)


def _dot_tn(a, b):
    return lax.dot_general(a.astype(_BF16), b.astype(_BF16), (((0,), (0,)), ((), ())),
                           preferred_element_type=_F32)


def _mod_kernel(c_ref, w_ref, b_ref, o_ref):
    c_act = _silu(c_ref[...])
    o_ref[0] = jnp.dot(c_act, w_ref[0], preferred_element_type=_F32,
                       precision=lax.Precision.HIGHEST) + b_ref[0]


def _modulation(c, w_mod, b_mod):
    depth, d, d3 = w_mod.shape
    b = c.shape[0]
    return pl.pallas_call(
        _mod_kernel,
        out_shape=jax.ShapeDtypeStruct((depth, b, d3), _F32),
        grid=(depth, d3 // d),
        in_specs=[
            pl.BlockSpec((b, d), lambda l, j: (0, 0)),
            pl.BlockSpec((1, d, d), lambda l, j: (l, 0, j)),
            pl.BlockSpec((1, 1, d), lambda l, j: (l, 0, j)),
        ],
        out_specs=pl.BlockSpec((1, b, d), lambda l, j: (l, 0, j)),
        compiler_params=pltpu.CompilerParams(dimension_semantics=("arbitrary", "arbitrary")),
        name="modulation",
    )(c, w_mod, b_mod.reshape(depth, 1, d3))


def _rope_kernel(pos_ref, freq_ref, o_ref):
    ang = pos_ref[0].astype(_F32) * freq_ref[...]
    lane = lax.broadcasted_iota(jnp.int32, ang.shape, 1)
    o_ref[0] = jnp.where(lane < ROPE, jnp.cos(ang), jnp.sin(ang))


def _rope_table(positions):
    b, s = positions.shape
    half = ROPE // 2
    inv_freq = jnp.power(ROPE_THETA, -jnp.arange(half, dtype=_F32) * 2.0 / ROPE)
    freq = jnp.tile(inv_freq, LANES // half).reshape(1, LANES)
    t = min(ROPE_TILE, s)
    return pl.pallas_call(
        _rope_kernel,
        out_shape=jax.ShapeDtypeStruct((b, s, LANES), _F32),
        grid=(b, s // t),
        in_specs=[
            pl.BlockSpec((1, t, 1), lambda i, j: (i, j, 0)),
            pl.BlockSpec((1, LANES), lambda i, j: (0, 0)),
        ],
        out_specs=pl.BlockSpec((1, t, LANES), lambda i, j: (i, j, 0)),
        compiler_params=pltpu.CompilerParams(dimension_semantics=("arbitrary", "arbitrary")),
        name="rope_table",
    )(positions.reshape(b, s, 1), freq)


def _in_kernel(x_ref, mod_ref, prew_ref, win_ref, qnw_ref, qup_ref, kvnw_ref, kvup_ref, tab_ref,
               convw_ref, alog_ref, dtb_ref,
               qt_ref, k_ref, vt_ref, zm_ref, gq_ref, gk_ref, gv_ref, gb_ref, zg_ref,
               tail_ref):
    d = x_ref.shape[-1]
    tm = x_ref.shape[1]

    @pl.when(pl.program_id(1) == 0)
    def _():
        tail_ref[tm:tm + 8, :] = jnp.zeros((8, tail_ref.shape[1]), tail_ref.dtype)

    x = x_ref[0]
    mod = mod_ref[0]
    w_eff = prew_ref[...] * (1.0 + mod[:, d:2 * d])
    h = (x * lax.rsqrt(jnp.mean(x * x, axis=-1, keepdims=True) + EPS) * w_eff + mod[:, 0:d]).astype(_BF16)

    def project(lo, hi):
        return jnp.dot(h, win_ref[:, lo:hi], preferred_element_type=_F32)

    proj = project(0, C_ZMLA)

    tab = tab_ref[0]
    q_scale = DQK ** -0.5 * LOG2E

    def rope(block):
        y = block * tab
        return y + pltpu.roll(y, ROPE, 1)

    qf = _dot(_rms(proj[:, C_QLAT:C_QLAT + Q_RANK], qnw_ref[...]), qup_ref[...])
    kvf = _dot(_rms(proj[:, C_KVLAT:C_KVLAT + KV_RANK], kvnw_ref[...]), kvup_ref[...])
    k_pe = rope(proj[:, C_KPE:C_KPE + LANES])[:, :ROPE].astype(k_ref.dtype)
    for hd in range(HEADS):
        lo = hd * NOPE
        q_pe = rope(qf[:, WIDTH + hd * LANES:WIDTH + (hd + 1) * LANES])
        qt_ref[0, hd, 0:NOPE, :] = (qf[:, lo:lo + NOPE] * q_scale).T.astype(qt_ref.dtype)
        qt_ref[0, hd, NOPE:DQK, :] = (q_pe * q_scale).T[:ROPE, :].astype(qt_ref.dtype)
        k_ref[0, hd, :, 0:NOPE] = kvf[:, lo:lo + NOPE].astype(k_ref.dtype)
        k_ref[0, hd, :, NOPE:DQK] = k_pe
        vt_ref[0, hd, 0, 0:DV, :] = kvf[:, WIDTH + lo:WIDTH + lo + DV].T.astype(vt_ref.dtype)
        vt_ref[0, hd, 0, DV:, :] = jnp.ones((VT_ROWS - DV, tm), vt_ref.dtype)
    zm_ref[0] = project(C_ZMLA, C_QKV).astype(zm_ref.dtype)

    for part, out_ref in enumerate((gq_ref, gk_ref, gv_ref)):
        cols = slice(part * WIDTH, (part + 1) * WIDTH)
        xg = project(C_QKV + part * WIDTH, C_QKV + (part + 1) * WIDTH)
        tail_ref[0:8, cols] = tail_ref[tm:tm + 8, cols]
        tail_ref[8:8 + tm, cols] = xg
        cw = convw_ref[:, cols]
        conv = xg * cw[CONV_K - 1:CONV_K, :]
        for j in range(CONV_K - 1):
            start = 8 - (CONV_K - 1 - j)
            conv = conv + tail_ref[start:start + tm, cols] * cw[j:j + 1, :]
        act = _silu(conv)
        if part == 2:
            out_ref[0] = act.astype(out_ref.dtype)
        else:
            scale = DV ** -0.5 if part == 0 else 1.0
            for hd in range(HEADS):
                lo = hd * DV
                a = act[:, lo:lo + DV]
                inv = lax.rsqrt(jnp.sum(a * a, axis=-1, keepdims=True) + EPS) * scale
                out_ref[0, :, lo:lo + DV] = (a * inv).astype(out_ref.dtype)

    gates = project(C_AB, N_COLS)
    ab = gates[:, 0:LANES]
    g = -jnp.exp(alog_ref[...]) * _softplus(ab + dtb_ref[...])
    lane = lax.broadcasted_iota(jnp.int32, ab.shape, 1)
    gb_ref[0] = jnp.where(lane < HEADS, g, _sigmoid(ab))
    zg_ref[0] = gates[:, C_ZGDN - C_AB:].astype(zg_ref.dtype)


def _in_call(layer, x, mod, prew, win, qnw, qup, kvnw, kvup, tab, convw, alog, dtb):
    b, s, d = x.shape
    tm = min(TOKEN_TILE, s)
    const = lambda shape: pl.BlockSpec((None,) + shape, lambda i, j: (layer,) + (0,) * len(shape))
    tok = lambda w: pl.BlockSpec((1, tm, w), lambda i, j: (i, j, 0))
    head = pl.BlockSpec((1, HEADS, tm, DQK), lambda i, j: (i, 0, j, 0))
    head_t = pl.BlockSpec((1, HEADS, DQK, tm), lambda i, j: (i, 0, 0, j))
    head_vt = pl.BlockSpec((1, HEADS, 1, VT_ROWS, tm), lambda i, j: (i, 0, j, 0, 0))
    out_shape = (
        jax.ShapeDtypeStruct((b, HEADS, DQK, s), _BF16),
        jax.ShapeDtypeStruct((b, HEADS, s, DQK), _BF16),
        jax.ShapeDtypeStruct((b, HEADS, s // tm, VT_ROWS, tm), _BF16),
        jax.ShapeDtypeStruct((b, s, WIDTH), _BF16),
        jax.ShapeDtypeStruct((b, s, WIDTH), _BF16),
        jax.ShapeDtypeStruct((b, s, WIDTH), _BF16),
        jax.ShapeDtypeStruct((b, s, WIDTH), _BF16),
        jax.ShapeDtypeStruct((b, s, LANES), _F32),
        jax.ShapeDtypeStruct((b, s, WIDTH), _BF16),
    )
    return pl.pallas_call(
        _in_kernel,
        out_shape=out_shape,
        grid=(b, s // tm),
        in_specs=[
            tok(d),
            pl.BlockSpec((None, 1, 1, 3 * d), lambda i, j: (layer, i, 0, 0)),
            const((1, d)),
            const((d, N_COLS)),
            const((1, Q_RANK)),
            const((Q_RANK, 2 * WIDTH)),
            const((1, KV_RANK)),
            const((KV_RANK, 2 * WIDTH)),
            tok(LANES),
            const((CONV_K, 3 * WIDTH)),
            const((1, LANES)),
            const((1, LANES)),
        ],
        out_specs=(head_t, head, head_vt, tok(WIDTH), tok(WIDTH), tok(WIDTH), tok(WIDTH),
                   tok(LANES), tok(WIDTH)),
        scratch_shapes=[pltpu.VMEM((tm + 8, 3 * WIDTH), _F32)],
        compiler_params=pltpu.CompilerParams(dimension_semantics=("arbitrary", "arbitrary"),
                                             vmem_limit_bytes=VMEM_LIMIT),
        name="in_proj",
    )(x, mod, prew, win, qnw, qup, kvnw, kvup, tab, convw, alog, dtb)


def _attn_kernel(qt_ref, k_ref, vt_ref, z_ref, o_ref, sa_ref, sb_ref, mxa_ref, mxb_ref, m_ref, acc_ref):
    tq = qt_ref.shape[3]
    uk = vt_ref.shape[4]
    uq = ATTN_SUB
    nq = tq // uq
    nu = tq // uk
    i = pl.program_id(2)
    heads = range(qt_ref.shape[1])
    qts = [[qt_ref[0, g, :, h * uq:(h + 1) * uq] for h in range(nq)] for g in heads]
    bufs = ((sa_ref, mxa_ref), (sb_ref, mxb_ref))

    m_ref[...] = jnp.full_like(m_ref, _NEG)
    acc_ref[...] = jnp.zeros_like(acc_ref)

    def produce(buf, blk, unit, subs, diagonal=False):
        s_ref, mx_ref = bufs[buf]
        for g in heads:
            kj = k_ref[0, g, pl.ds(pl.multiple_of(blk * tq + unit * uk, uk), uk), :]
            for h in subs:
                keys = min(uk, (h - unit * (uk // uq) + 1) * uq) if diagonal else uk
                s = jnp.dot(kj[0:keys], qts[g][h], preferred_element_type=_F32)
                s_ref[g * nq + h, 0:keys, :] = s
                mx_ref[g * nq + h] = jnp.max(s, axis=0, keepdims=True)

    def consume(buf, blk, unit, subs, masked=()):
        s_ref, mx_ref = bufs[buf]
        for g in heads:
            vt = vt_ref[0, g, blk * nu + unit]
            for h in subs:
                f = g * nq + h
                if h in masked:
                    own = h - unit * (uk // uq)
                    tri = (lax.broadcasted_iota(jnp.int32, (uq, uq), 0)
                           <= lax.broadcasted_iota(jnp.int32, (uq, uq), 1))
                    s = jnp.where(tri, s_ref[f, own * uq:(own + 1) * uq, :], _NEG)
                    if own > 0:
                        s = jnp.concatenate([s_ref[f, 0:own * uq, :], s], axis=0)
                    mx = jnp.max(s, axis=0, keepdims=True)
                    vth = vt_ref[0, g, blk * nu + unit, :, 0:(own + 1) * uq]
                else:
                    s = s_ref[f]
                    mx = mx_ref[f]
                    vth = vt
                m = m_ref[f]
                m_new = jnp.maximum(m, mx)
                p = jnp.exp2(s - m_new).astype(_BF16)
                acc_ref[f] = jnp.exp2(m - m_new) * acc_ref[f] + jnp.dot(vth, p, preferred_element_type=_F32)
                m_ref[f] = m_new

    every = tuple(range(nq))
    produce(0, 0, 0, every)

    def block(j):
        for unit in range(nu):
            if unit + 1 < nu:
                produce((unit + 1) % 2, j, unit + 1, every)
            else:
                produce((unit + 1) % 2, j + 1, 0, every)
            consume(unit % 2, j, unit, every)

    def pair(p, carry):
        block(2 * p)
        block(2 * p + 1)
        return carry

    lax.fori_loop(0, i // 2, pair, 0)

    @pl.when(i % 2 == 1)
    def _():
        block(i - 1)
    per = uk // uq
    for unit in range(nu):
        if unit + 1 < nu:
            produce((unit + 1) % 2, i, unit + 1, tuple(range((unit + 1) * per, nq)), diagonal=True)
        consume(unit % 2, i, unit, tuple(range(unit * per, nq)),
                masked=tuple(range(unit * per, (unit + 1) * per)))
    for g in heads:
        for h in range(nq):
            acc = acc_ref[g * nq + h]
            rows, cols = slice(h * uq, (h + 1) * uq), slice(g * DV, (g + 1) * DV)
            gate = _silu(z_ref[0, rows, cols].astype(_F32))
            o_ref[0, rows, cols] = ((acc[:DV] / acc[DV:DV + 1]).T * gate).astype(o_ref.dtype)


def _attn_call(qt, k, vt, z):
    b, _, _, s = qt.shape
    vrows, tv = vt.shape[-2:]
    t = ATTN_Q
    g = ATTN_HEADS
    nq = g * (t // ATTN_SUB)
    return pl.pallas_call(
        _attn_kernel,
        out_shape=jax.ShapeDtypeStruct((b, s, WIDTH), _BF16),
        grid=(b, HEADS // g, s // t),
        in_specs=[
            pl.BlockSpec((1, g, DQK, t), lambda bi, h, i: (bi, h, 0, i)),
            pl.BlockSpec((1, g, s, DQK), lambda bi, h, i: (bi, h, 0, 0)),
            pl.BlockSpec((1, g, s // tv, vrows, tv), lambda bi, h, i: (bi, h, 0, 0, 0)),
            pl.BlockSpec((1, t, g * DV), lambda bi, h, i: (bi, i, h)),
        ],
        out_specs=pl.BlockSpec((1, t, g * DV), lambda bi, h, i: (bi, i, h)),
        scratch_shapes=[
            pltpu.VMEM((nq, tv, ATTN_SUB), _F32),
            pltpu.VMEM((nq, tv, ATTN_SUB), _F32),
            pltpu.VMEM((nq, 1, ATTN_SUB), _F32),
            pltpu.VMEM((nq, 1, ATTN_SUB), _F32),
            pltpu.VMEM((nq, 1, ATTN_SUB), _F32),
            pltpu.VMEM((nq, vrows, ATTN_SUB), _F32),
        ],
        compiler_params=pltpu.CompilerParams(
            dimension_semantics=("arbitrary", "arbitrary", "arbitrary"), vmem_limit_bytes=VMEM_LIMIT),
        name="mla_attention",
    )(qt, k, vt, z)


def _cumsum_rows(tril_bf16, x):
    lane = lax.broadcasted_iota(jnp.int32, x.shape, 1)
    hi = x.astype(_BF16).astype(_F32)
    r1 = x - hi
    mid = r1.astype(_BF16).astype(_F32)
    low = r1 - mid
    packed = jnp.where(lane < 8, hi, jnp.where(lane < 16, pltpu.roll(mid, 8, 1), pltpu.roll(low, 16, 1)))
    res = jnp.dot(tril_bf16, packed.astype(_BF16), preferred_element_type=_F32)
    return res + pltpu.roll(res, LANES - 8, 1) + pltpu.roll(res, LANES - 16, 1)


def _gdn_kernel(q_ref, k_ref, v_ref, gb_ref, o_ref, state_ref, gc_ref, n_ref, t_ref, qk_ref, rhs_ref,
                qe_ref, kd_ref, sol_ref, ou_ref, qw_ref, su_ref, sw_ref):
    c = GDN_CHUNK
    n_chunks = q_ref.shape[1] // c
    n_chains = n_chunks * HEADS

    @pl.when(pl.program_id(1) == 0)
    def _():
        state_ref[...] = jnp.zeros_like(state_ref)

    row = lax.broadcasted_iota(jnp.int32, (c, c), 0)
    col = lax.broadcasted_iota(jnp.int32, (c, c), 1)
    incl = col <= row
    xor = row ^ col
    eye = (xor == 0).astype(_F32)
    tril = incl.astype(_BF16)

    for ci in range(n_chunks):
        rows = slice(ci * c, (ci + 1) * c)
        gb = gb_ref[0, rows, :]
        gcum = _cumsum_rows(tril, gb)
        gc_ref[ci] = gcum
        gcum_t = gcum.T
        for hd in range(HEADS):
            ch = ci * HEADS + hd
            lo = hd * DV
            q = q_ref[0, rows, lo:lo + DV].astype(_F32)
            k = k_ref[0, rows, lo:lo + DV].astype(_F32)
            v = v_ref[0, rows, lo:lo + DV].astype(_F32)
            beta = jnp.broadcast_to(gb[:, HEADS + hd:HEADS + hd + 1], (c, DV))
            g_col = jnp.broadcast_to(gcum[:, hd:hd + 1], (c, DV))
            g_row = gcum_t[hd:hd + 1, :]
            g_last = gcum[c - 1:c, hd:hd + 1]
            decay = jnp.exp(jnp.where(incl, jnp.concatenate([g_col, g_col], axis=1) - g_row, _NEG))
            e_col = jnp.exp(g_col)
            kb = k * beta
            n_neg = _dot_nt(-kb, k) * decay
            n_ref[ch] = n_neg
            t_ref[ch] = jnp.where(xor == 1, n_neg, eye)
            qk_ref[ch] = (_dot_nt(q, k) * decay).astype(qk_ref.dtype)
            rhs_ref[ch] = jnp.concatenate([v * beta, kb * e_col], axis=1).astype(rhs_ref.dtype)
            qe_ref[ch] = (q * e_col).astype(qe_ref.dtype)
            kd_ref[ch] = (k * jnp.exp(g_last - g_col)).astype(kd_ref.dtype)

    m = 2
    while m < c:
        blk = lax.broadcasted_iota(jnp.int32, (c, c), 0) ^ lax.broadcasted_iota(jnp.int32, (c, c), 1)
        mask = (blk >= m) & (blk < 2 * m)
        size = m if m % 8 == 0 else c
        odd = [slice(b * m, (b + 1) * m) for b in range(1, c // m, 2)] if m % 8 == 0 else [slice(0, c)]

        def rows_of(ch, odd=odd):
            return jnp.concatenate([t_ref[ch, r, :] for r in odd], axis=0)

        left = []
        for ch in range(n_chains):
            left.append(_dot(rows_of(ch), jnp.where(mask, n_ref[ch], 0.0)))
        for ch in range(n_chains):
            upd = rows_of(ch) + _dot(left[ch], t_ref[ch])
            for j, r in enumerate(odd):
                t_ref[ch, r, :] = upd[j * size:(j + 1) * size]
        m *= 2

    for ch in range(n_chains):
        sol_ref[ch] = _dot(t_ref[ch], rhs_ref[ch]).astype(sol_ref.dtype)
    for ch in range(n_chains):
        qk_sol = _dot(qk_ref[ch], sol_ref[ch])
        ou_ref[ch] = qk_sol[:, :DV]
        qw_ref[ch] = (qe_ref[ch].astype(_F32) - qk_sol[:, DV:]).astype(qw_ref.dtype)
    for ch in range(n_chains):
        kd_sol = _dot_tn(kd_ref[ch], sol_ref[ch])
        su_ref[ch] = kd_sol[:, :DV]
        sw_ref[ch] = kd_sol[:, DV:].astype(sw_ref.dtype)
    for ci in range(n_chunks):
        rows = slice(ci * c, (ci + 1) * c)
        for hd in range(HEADS):
            ch = ci * HEADS + hd
            lo = hd * DV
            g_last = gc_ref[ci, c - 1:c, hd:hd + 1]
            state = state_ref[hd]
            o_ref[0, rows, lo:lo + DV] = (_dot(qw_ref[ch], state) + ou_ref[ch]).astype(o_ref.dtype)
            state_ref[hd] = state * jnp.exp(g_last) + su_ref[ch] - _dot(sw_ref[ch], state)


def _gdn_call(gq, gk, gv, gb):
    b, s, _ = gq.shape
    t = GDN_TILE
    n_chains = (t // GDN_CHUNK) * HEADS
    tok = lambda w: pl.BlockSpec((1, t, w), lambda i, j: (i, j, 0))
    return pl.pallas_call(
        _gdn_kernel,
        out_shape=jax.ShapeDtypeStruct((b, s, WIDTH), _BF16),
        grid=(b, s // t),
        in_specs=[tok(WIDTH), tok(WIDTH), tok(WIDTH), tok(LANES)],
        out_specs=tok(WIDTH),
        scratch_shapes=[
            pltpu.VMEM((HEADS, DV, DV), _F32),
            pltpu.VMEM((t // GDN_CHUNK, GDN_CHUNK, LANES), _F32),
            pltpu.VMEM((n_chains, GDN_CHUNK, GDN_CHUNK), _F32),
            pltpu.VMEM((n_chains, GDN_CHUNK, GDN_CHUNK), _F32),
            pltpu.VMEM((n_chains, GDN_CHUNK, GDN_CHUNK), _BF16),
            pltpu.VMEM((n_chains, GDN_CHUNK, 2 * DV), _BF16),
            pltpu.VMEM((n_chains, GDN_CHUNK, DV), _BF16),
            pltpu.VMEM((n_chains, GDN_CHUNK, DV), _BF16),
            pltpu.VMEM((n_chains, GDN_CHUNK, 2 * DV), _BF16),
            pltpu.VMEM((n_chains, GDN_CHUNK, DV), _F32),
            pltpu.VMEM((n_chains, GDN_CHUNK, DV), _BF16),
            pltpu.VMEM((n_chains, DV, DV), _F32),
            pltpu.VMEM((n_chains, DV, DV), _BF16),
        ],
        compiler_params=pltpu.CompilerParams(dimension_semantics=("arbitrary", "arbitrary"),
                                             vmem_limit_bytes=VMEM_LIMIT),
        name="gated_delta_rule",
    )(gq, gk, gv, gb)


def _out_kernel(x_ref, mod_ref, om_ref, og_ref, zg_ref, onw_ref, wout_ref, postw_ref, o_ref):
    d = x_ref.shape[-1]
    tm = x_ref.shape[1]
    w_post = mod_ref[0][:, 2 * d:3 * d] * postw_ref[...]
    for r in range(0, tm, OUT_ROWS):
        rows = slice(r, r + OUT_ROWS)
        og = og_ref[0, rows, :].astype(_F32)
        zg = _silu(zg_ref[0, rows, :].astype(_F32))
        y_gdn = jnp.concatenate([_rms(og[:, hd * DV:(hd + 1) * DV], onw_ref[...]) * zg[:, hd * DV:(hd + 1) * DV]
                                 for hd in range(HEADS)], axis=1)
        y = (jnp.dot(om_ref[0, rows, :], wout_ref[0:WIDTH, :], preferred_element_type=_F32)
             + _dot(y_gdn, wout_ref[WIDTH:2 * WIDTH, :]))
        y_n = y * lax.rsqrt(jnp.mean(y * y, axis=-1, keepdims=True) + EPS)
        o_ref[0, rows, :] = x_ref[0, rows, :] + y_n * w_post


def _out_call(layer, x, mod, o_mla, o_gdn, z_gdn, onw, wout, postw):
    b, s, d = x.shape
    tm = min(TOKEN_TILE, s)
    const = lambda shape: pl.BlockSpec((None,) + shape, lambda i, j: (layer,) + (0,) * len(shape))
    tok = lambda w: pl.BlockSpec((1, tm, w), lambda i, j: (i, j, 0))
    return pl.pallas_call(
        _out_kernel,
        out_shape=jax.ShapeDtypeStruct((b, s, d), _F32),
        grid=(b, s // tm),
        in_specs=[
            tok(d),
            pl.BlockSpec((None, 1, 1, 3 * d), lambda i, j: (layer, i, 0, 0)),
            tok(WIDTH), tok(WIDTH), tok(WIDTH),
            const((1, DV)),
            const((2 * WIDTH, d)),
            const((1, d)),
        ],
        out_specs=tok(d),
        compiler_params=pltpu.CompilerParams(dimension_semantics=("arbitrary", "arbitrary"),
                                             vmem_limit_bytes=VMEM_LIMIT),
        name="out_proj",
    )(x, mod, o_mla, o_gdn, z_gdn, onw, wout, postw)


def _rot_cols(w):
    half = w.shape[-1] // 2
    return jnp.concatenate([-w[..., half:], w[..., :half]], axis=-1)


def _pad_cols(w, n):
    return jnp.pad(w, [(0, 0)] * (w.ndim - 1) + [(0, n - w.shape[-1])])


def _prep_w_in(w):
    w = w.astype(_BF16)
    o_kv = Q_RANK
    o_kpe = o_kv + KV_RANK
    o_zm = o_kpe + ROPE
    o_qkv = o_zm + WIDTH
    o_a = o_qkv + 3 * WIDTH
    o_zg = o_a + 2 * HEADS
    k_pe = w[..., o_kpe:o_zm]
    return jnp.concatenate([
        w[..., :o_kpe], k_pe, _rot_cols(k_pe), w[..., o_zm:o_qkv], w[..., o_qkv:o_a],
        _pad_cols(w[..., o_a:o_zg], LANES), w[..., o_zg:],
    ], axis=-1)


def _prep_q_up(w):
    depth = w.shape[0]
    per = w.astype(_BF16).reshape(depth, Q_RANK, HEADS, DQK)
    nope = per[..., :NOPE].reshape(depth, Q_RANK, HEADS * NOPE)
    pe = [jnp.concatenate([per[:, :, h, NOPE:], _rot_cols(per[:, :, h, NOPE:])], axis=-1) for h in range(HEADS)]
    return jnp.concatenate([nope] + pe, axis=-1)


def _prep_kv_up(w):
    depth = w.shape[0]
    per = w.astype(_BF16).reshape(depth, KV_RANK, HEADS, NOPE + DV)
    return jnp.concatenate([per[..., :NOPE].reshape(depth, KV_RANK, HEADS * NOPE),
                            per[..., NOPE:].reshape(depth, KV_RANK, HEADS * DV)], axis=-1)


def _lane_rows(vals):
    return _pad_cols(vals.astype(_F32), LANES)[:, None, :]


def kernel(x, c, positions, w_mod, b_mod, pre_norm_w, post_norm_w, w_in, mla_q_norm_w, mla_q_up,
           mla_kv_norm_w, mla_kv_up, gdn_conv_w, gdn_a_log, gdn_dt_bias, gdn_o_norm_w, w_out):
    depth = w_mod.shape[0]
    b, s, d = x.shape
    assert s % TOKEN_TILE == 0 and s % GDN_TILE == 0 and s % ATTN_Q == 0
    assert TOKEN_TILE % ATTN_SUB == 0 and ATTN_Q % TOKEN_TILE == 0 and (ATTN_Q // ATTN_SUB) % 2 == 0
    mod = _modulation(c, w_mod, b_mod).reshape(depth, b, 1, 3 * d)
    tab = _rope_table(positions)
    in_params = (pre_norm_w[:, None, :], _prep_w_in(w_in), mla_q_norm_w[:, None, :], _prep_q_up(mla_q_up),
                 mla_kv_norm_w[:, None, :], _prep_kv_up(mla_kv_up))
    gdn_params = (gdn_conv_w, _lane_rows(gdn_a_log), _lane_rows(gdn_dt_bias))
    out_params = (gdn_o_norm_w[:, None, :], w_out.astype(_BF16), post_norm_w[:, None, :])
    for l in range(depth):
        q, k, v, z_mla, gq, gk, gv, gb, z_gdn = _in_call(l, x, mod, *in_params, tab, *gdn_params)
        o_mla = _attn_call(q, k, v, z_mla)
        o_gdn = _gdn_call(gq, gk, gv, gb)
        x = _out_call(l, x, mod, o_mla, o_gdn, z_gdn, *out_params)
    return x
```

```python
import functools

import jax
import jax.numpy as jnp
import numpy as np
from jax import lax
from jax.experimental import pallas as pl
from jax.experimental.pallas import tpu as pltpu

HEADS = 4
NOPE = 128
ROPE = 64
DQK = NOPE + ROPE
DV = 128
Q_RANK = 384
KV_RANK = 256
WIDTH = HEADS * DV
CONV_K = 4
ROPE_THETA = 10000.0
EPS = 1e-6

LANES = 128

C_QLAT = 0
C_KVLAT = C_QLAT + Q_RANK
C_KPE = C_KVLAT + KV_RANK
C_ZMLA = C_KPE + LANES
C_QKV = C_ZMLA + WIDTH
C_AB = C_QKV + 3 * WIDTH
C_ZGDN = C_AB + LANES
N_COLS = C_ZGDN + WIDTH

TOKEN_TILE = 512
ATTN_Q = 1024
ATTN_SUB = 256
ATTN_HEADS = 1
VT_ROWS = DV + 16
GDN_CHUNK = 256
GDN_TILE = 512
OUT_TILE = 1024
OUT_ROWS = 256
ROPE_TILE = 1024
VMEM_LIMIT = 56 * 1024 * 1024

_F32 = jnp.float32
_BF16 = jnp.bfloat16
_NEG = -1e30
LOG2E = float(np.log2(np.e))


def _sigmoid(x):
    return 1.0 / (1.0 + jnp.exp2(x * -LOG2E))


def _silu(x):
    h = 0.5 * x
    return h + h * jnp.tanh(h)


def _softplus(x):
    return jnp.maximum(x, 0.0) + jnp.log(1.0 + jnp.exp(-jnp.abs(x)))


def _rms(x, w):
    return x * lax.rsqrt(jnp.mean(x * x, axis=-1, keepdims=True) + EPS) * w


def _dot(a, b):
    return jnp.dot(a.astype(_BF16), b.astype(_BF16), preferred_element_type=_F32)


def _dot_nt(a, b):
    return lax.dot_general(a.astype(_BF16), b.astype(_BF16), (((1,), (1,)), ((), ())),
                           preferred_element_type=_F32)


def _dot_tn(a, b):
    return lax.dot_general(a.astype(_BF16), b.astype(_BF16), (((0,), (0,)), ((), ())),
                           preferred_element_type=_F32)


def _mod_kernel(c_ref, w_ref, b_ref, o_ref):
    c_act = _silu(c_ref[...])
    o_ref[0] = jnp.dot(c_act, w_ref[0], preferred_element_type=_F32,
                       precision=lax.Precision.HIGHEST) + b_ref[0]


def _modulation(c, w_mod, b_mod):
    depth, d, d3 = w_mod.shape
    b = c.shape[0]
    return pl.pallas_call(
        _mod_kernel,
        out_shape=jax.ShapeDtypeStruct((depth, b, d3), _F32),
        grid=(depth, d3 // d),
        in_specs=[
            pl.BlockSpec((b, d), lambda l, j: (0, 0)),
            pl.BlockSpec((1, d, d), lambda l, j: (l, 0, j)),
            pl.BlockSpec((1, 1, d), lambda l, j: (l, 0, j)),
        ],
        out_specs=pl.BlockSpec((1, b, d), lambda l, j: (l, 0, j)),
        compiler_params=pltpu.CompilerParams(dimension_semantics=("arbitrary", "arbitrary")),
        name="modulation",
    )(c, w_mod, b_mod.reshape(depth, 1, d3))


def _rope_kernel(pos_ref, freq_ref, o_ref):
    ang = pos_ref[0].astype(_F32) * freq_ref[...]
    lane = lax.broadcasted_iota(jnp.int32, ang.shape, 1)
    o_ref[0] = jnp.where(lane < ROPE, jnp.cos(ang), jnp.sin(ang))


def _rope_table(positions):
    b, s = positions.shape
    half = ROPE // 2
    inv_freq = jnp.power(ROPE_THETA, -jnp.arange(half, dtype=_F32) * 2.0 / ROPE)
    freq = jnp.tile(inv_freq, LANES // half).reshape(1, LANES)
    t = min(ROPE_TILE, s)
    return pl.pallas_call(
        _rope_kernel,
        out_shape=jax.ShapeDtypeStruct((b, s, LANES), _F32),
        grid=(b, s // t),
        in_specs=[
            pl.BlockSpec((1, t, 1), lambda i, j: (i, j, 0)),
            pl.BlockSpec((1, LANES), lambda i, j: (0, 0)),
        ],
        out_specs=pl.BlockSpec((1, t, LANES), lambda i, j: (i, j, 0)),
        compiler_params=pltpu.CompilerParams(dimension_semantics=("arbitrary", "arbitrary")),
        name="rope_table",
    )(positions.reshape(b, s, 1), freq)


def _in_kernel(x_ref, mod_ref, prew_ref, win_ref, qnw_ref, qup_ref, kvnw_ref, kvup_ref, tab_ref,
               convw_ref, alog_ref, dtb_ref,
               qt_ref, k_ref, vt_ref, zm_ref, gq_ref, gk_ref, gv_ref, gb_ref, zg_ref,
               tail_ref):
    d = x_ref.shape[-1]
    tm = x_ref.shape[1]

    @pl.when(pl.program_id(1) == 0)
    def _():
        tail_ref[tm:tm + 8, :] = jnp.zeros((8, tail_ref.shape[1]), tail_ref.dtype)

    x = x_ref[0]
    mod = mod_ref[0]
    w_eff = prew_ref[...] * (1.0 + mod[:, d:2 * d])
    h = (x * lax.rsqrt(jnp.mean(x * x, axis=-1, keepdims=True) + EPS) * w_eff + mod[:, 0:d]).astype(_BF16)

    def project(lo, hi):
        return jnp.dot(h, win_ref[:, lo:hi], preferred_element_type=_F32)

    proj = project(0, C_ZMLA)

    tab = tab_ref[0]
    q_scale = DQK ** -0.5 * LOG2E

    def rope(block):
        y = block * tab
        return y + pltpu.roll(y, ROPE, 1)

    qf = _dot(_rms(proj[:, C_QLAT:C_QLAT + Q_RANK], qnw_ref[...]), qup_ref[...])
    kvf = _dot(_rms(proj[:, C_KVLAT:C_KVLAT + KV_RANK], kvnw_ref[...]), kvup_ref[...])
    k_pe = rope(proj[:, C_KPE:C_KPE + LANES])[:, :ROPE].astype(k_ref.dtype)
    for hd in range(HEADS):
        lo = hd * NOPE
        q_pe = rope(qf[:, WIDTH + hd * LANES:WIDTH + (hd + 1) * LANES])
        qt_ref[0, hd, 0:NOPE, :] = (qf[:, lo:lo + NOPE] * q_scale).T.astype(qt_ref.dtype)
        qt_ref[0, hd, NOPE:DQK, :] = (q_pe * q_scale).T[:ROPE, :].astype(qt_ref.dtype)
        k_ref[0, hd, :, 0:NOPE] = kvf[:, lo:lo + NOPE].astype(k_ref.dtype)
        k_ref[0, hd, :, NOPE:DQK] = k_pe
        vt_ref[0, hd, 0, 0:DV, :] = kvf[:, WIDTH + lo:WIDTH + lo + DV].T.astype(vt_ref.dtype)
        vt_ref[0, hd, 0, DV:, :] = jnp.ones((VT_ROWS - DV, tm), vt_ref.dtype)
    zm_ref[0] = project(C_ZMLA, C_QKV).astype(zm_ref.dtype)

    for part, out_ref in enumerate((gq_ref, gk_ref, gv_ref)):
        cols = slice(part * WIDTH, (part + 1) * WIDTH)
        xg = project(C_QKV + part * WIDTH, C_QKV + (part + 1) * WIDTH)
        tail_ref[0:8, cols] = tail_ref[tm:tm + 8, cols]
        tail_ref[8:8 + tm, cols] = xg
        cw = convw_ref[:, cols]
        conv = xg * cw[CONV_K - 1:CONV_K, :]
        for j in range(CONV_K - 1):
            start = 8 - (CONV_K - 1 - j)
            conv = conv + tail_ref[start:start + tm, cols] * cw[j:j + 1, :]
        act = _silu(conv)
        if part == 2:
            out_ref[0] = act.astype(out_ref.dtype)
        else:
            scale = DV ** -0.5 if part == 0 else 1.0
            for hd in range(HEADS):
                lo = hd * DV
                a = act[:, lo:lo + DV]
                inv = lax.rsqrt(jnp.sum(a * a, axis=-1, keepdims=True) + EPS) * scale
                out_ref[0, :, lo:lo + DV] = (a * inv).astype(out_ref.dtype)

    gates = project(C_AB, N_COLS)
    ab = gates[:, 0:LANES]
    g = -jnp.exp(alog_ref[...]) * _softplus(ab + dtb_ref[...])
    lane = lax.broadcasted_iota(jnp.int32, ab.shape, 1)
    gb_ref[0] = jnp.where(lane < HEADS, g, _sigmoid(ab))
    zg_ref[0] = gates[:, C_ZGDN - C_AB:].astype(zg_ref.dtype)


def _in_call(layer, x, mod, prew, win, qnw, qup, kvnw, kvup, tab, convw, alog, dtb):
    b, s, d = x.shape
    tm = min(TOKEN_TILE, s)
    const = lambda shape: pl.BlockSpec((None,) + shape, lambda i, j: (layer,) + (0,) * len(shape))
    tok = lambda w: pl.BlockSpec((1, tm, w), lambda i, j: (i, j, 0))
    head = pl.BlockSpec((1, HEADS, tm, DQK), lambda i, j: (i, 0, j, 0))
    head_t = pl.BlockSpec((1, HEADS, DQK, tm), lambda i, j: (i, 0, 0, j))
    head_vt = pl.BlockSpec((1, HEADS, 1, VT_ROWS, tm), lambda i, j: (i, 0, j, 0, 0))
    out_shape = (
        jax.ShapeDtypeStruct((b, HEADS, DQK, s), _BF16),
        jax.ShapeDtypeStruct((b, HEADS, s, DQK), _BF16),
        jax.ShapeDtypeStruct((b, HEADS, s // tm, VT_ROWS, tm), _BF16),
        jax.ShapeDtypeStruct((b, s, WIDTH), _BF16),
        jax.ShapeDtypeStruct((b, s, WIDTH), _BF16),
        jax.ShapeDtypeStruct((b, s, WIDTH), _BF16),
        jax.ShapeDtypeStruct((b, s, WIDTH), _BF16),
        jax.ShapeDtypeStruct((b, s, LANES), _F32),
        jax.ShapeDtypeStruct((b, s, WIDTH), _BF16),
    )
    return pl.pallas_call(
        _in_kernel,
        out_shape=out_shape,
        grid=(b, s // tm),
        in_specs=[
            tok(d),
            pl.BlockSpec((None, 1, 1, 3 * d), lambda i, j: (layer, i, 0, 0)),
            const((1, d)),
            const((d, N_COLS)),
            const((1, Q_RANK)),
            const((Q_RANK, 2 * WIDTH)),
            const((1, KV_RANK)),
            const((KV_RANK, 2 * WIDTH)),
            tok(LANES),
            const((CONV_K, 3 * WIDTH)),
            const((1, LANES)),
            const((1, LANES)),
        ],
        out_specs=(head_t, head, head_vt, tok(WIDTH), tok(WIDTH), tok(WIDTH), tok(WIDTH),
                   tok(LANES), tok(WIDTH)),
        scratch_shapes=[pltpu.VMEM((tm + 8, 3 * WIDTH), _F32)],
        compiler_params=pltpu.CompilerParams(dimension_semantics=("arbitrary", "arbitrary"),
                                             vmem_limit_bytes=VMEM_LIMIT),
        name="in_proj",
    )(x, mod, prew, win, qnw, qup, kvnw, kvup, tab, convw, alog, dtb)


def _attn_kernel(qt_ref, k_ref, vt_ref, z_ref, o_ref, sa_ref, sb_ref, mxa_ref, mxb_ref, m_ref, acc_ref):
    tq = qt_ref.shape[3]
    uk = vt_ref.shape[4]
    uq = ATTN_SUB
    nq = tq // uq
    nu = tq // uk
    i = pl.program_id(2)
    heads = range(qt_ref.shape[1])
    qts = [[qt_ref[0, g, :, h * uq:(h + 1) * uq] for h in range(nq)] for g in heads]
    bufs = ((sa_ref, mxa_ref), (sb_ref, mxb_ref))

    m_ref[...] = jnp.full_like(m_ref, _NEG)
    acc_ref[...] = jnp.zeros_like(acc_ref)

    def produce(buf, blk, unit, subs, diagonal=False):
        s_ref, mx_ref = bufs[buf]
        for g in heads:
            kj = k_ref[0, g, pl.ds(pl.multiple_of(blk * tq + unit * uk, uk), uk), :]
            for h in subs:
                keys = min(uk, (h - unit * (uk // uq) + 1) * uq) if diagonal else uk
                s = jnp.dot(kj[0:keys], qts[g][h], preferred_element_type=_F32)
                s_ref[g * nq + h, 0:keys, :] = s
                mx_ref[g * nq + h] = jnp.max(s, axis=0, keepdims=True)

    def consume(buf, blk, unit, subs, masked=()):
        s_ref, mx_ref = bufs[buf]
        for g in heads:
            vt = vt_ref[0, g, blk * nu + unit]
            for h in subs:
                f = g * nq + h
                if h in masked:
                    own = h - unit * (uk // uq)
                    tri = (lax.broadcasted_iota(jnp.int32, (uq, uq), 0)
                           <= lax.broadcasted_iota(jnp.int32, (uq, uq), 1))
                    s = jnp.where(tri, s_ref[f, own * uq:(own + 1) * uq, :], _NEG)
                    if own > 0:
                        s = jnp.concatenate([s_ref[f, 0:own * uq, :], s], axis=0)
                    mx = jnp.max(s, axis=0, keepdims=True)
                    vth = vt_ref[0, g, blk * nu + unit, :, 0:(own + 1) * uq]
                else:
                    s = s_ref[f]
                    mx = mx_ref[f]
                    vth = vt
                m = m_ref[f]
                m_new = jnp.maximum(m, mx)
                p = jnp.exp2(s - m_new).astype(_BF16)
                acc_ref[f] = jnp.exp2(m - m_new) * acc_ref[f] + jnp.dot(vth, p, preferred_element_type=_F32)
                m_ref[f] = m_new

    every = tuple(range(nq))
    produce(0, 0, 0, every)

    def block(j):
        for unit in range(nu):
            if unit + 1 < nu:
                produce((unit + 1) % 2, j, unit + 1, every)
            else:
                produce((unit + 1) % 2, j + 1, 0, every)
            consume(unit % 2, j, unit, every)

    def pair(p, carry):
        block(2 * p)
        block(2 * p + 1)
        return carry

    lax.fori_loop(0, i // 2, pair, 0)

    @pl.when(i % 2 == 1)
    def _():
        block(i - 1)
    per = uk // uq
    for unit in range(nu):
        if unit + 1 < nu:
            produce((unit + 1) % 2, i, unit + 1, tuple(range((unit + 1) * per, nq)), diagonal=True)
        consume(unit % 2, i, unit, tuple(range(unit * per, nq)),
                masked=tuple(range(unit * per, (unit + 1) * per)))
    for g in heads:
        for h in range(nq):
            acc = acc_ref[g * nq + h]
            rows, cols = slice(h * uq, (h + 1) * uq), slice(g * DV, (g + 1) * DV)
            gate = _silu(z_ref[0, rows, cols].astype(_F32))
            o_ref[0, rows, cols] = ((acc[:DV] / acc[DV:DV + 1]).T * gate).astype(o_ref.dtype)


def _attn_call(qt, k, vt, z):
    b, _, _, s = qt.shape
    vrows, tv = vt.shape[-2:]
    t = ATTN_Q
    g = ATTN_HEADS
    nq = g * (t // ATTN_SUB)
    return pl.pallas_call(
        _attn_kernel,
        out_shape=jax.ShapeDtypeStruct((b, s, WIDTH), _BF16),
        grid=(b, HEADS // g, s // t),
        in_specs=[
            pl.BlockSpec((1, g, DQK, t), lambda bi, h, i: (bi, h, 0, i)),
            pl.BlockSpec((1, g, s, DQK), lambda bi, h, i: (bi, h, 0, 0)),
            pl.BlockSpec((1, g, s // tv, vrows, tv), lambda bi, h, i: (bi, h, 0, 0, 0)),
            pl.BlockSpec((1, t, g * DV), lambda bi, h, i: (bi, i, h)),
        ],
        out_specs=pl.BlockSpec((1, t, g * DV), lambda bi, h, i: (bi, i, h)),
        scratch_shapes=[
            pltpu.VMEM((nq, tv, ATTN_SUB), _F32),
            pltpu.VMEM((nq, tv, ATTN_SUB), _F32),
            pltpu.VMEM((nq, 1, ATTN_SUB), _F32),
            pltpu.VMEM((nq, 1, ATTN_SUB), _F32),
            pltpu.VMEM((nq, 1, ATTN_SUB), _F32),
            pltpu.VMEM((nq, vrows, ATTN_SUB), _F32),
        ],
        compiler_params=pltpu.CompilerParams(
            dimension_semantics=("arbitrary", "arbitrary", "arbitrary"), vmem_limit_bytes=VMEM_LIMIT),
        name="mla_attention",
    )(qt, k, vt, z)


def _cumsum_rows(tril_bf16, x):
    lane = lax.broadcasted_iota(jnp.int32, x.shape, 1)
    hi = x.astype(_BF16).astype(_F32)
    r1 = x - hi
    mid = r1.astype(_BF16).astype(_F32)
    low = r1 - mid
    packed = jnp.where(lane < 8, hi, jnp.where(lane < 16, pltpu.roll(mid, 8, 1), pltpu.roll(low, 16, 1)))
    res = jnp.dot(tril_bf16, packed.astype(_BF16), preferred_element_type=_F32)
    return res + pltpu.roll(res, LANES - 8, 1) + pltpu.roll(res, LANES - 16, 1)


def _gdn_kernel(q_ref, k_ref, v_ref, gb_ref, o_ref, state_ref, gc_ref, n_ref, t_ref, qk_ref, rhs_ref,
                qe_ref, kd_ref, sol_ref, ou_ref, qw_ref, su_ref, sw_ref):
    c = GDN_CHUNK
    n_chunks = q_ref.shape[1] // c
    n_chains = n_chunks * HEADS

    @pl.when(pl.program_id(1) == 0)
    def _():
        state_ref[...] = jnp.zeros_like(state_ref)

    row = lax.broadcasted_iota(jnp.int32, (c, c), 0)
    col = lax.broadcasted_iota(jnp.int32, (c, c), 1)
    incl = col <= row
    xor = row ^ col
    eye = (xor == 0).astype(_F32)
    tril = incl.astype(_BF16)

    for ci in range(n_chunks):
        rows = slice(ci * c, (ci + 1) * c)
        gb = gb_ref[0, rows, :]
        gcum = _cumsum_rows(tril, gb)
        gc_ref[ci] = gcum
        gcum_t = gcum.T
        for hd in range(HEADS):
            ch = ci * HEADS + hd
            lo = hd * DV
            q = q_ref[0, rows, lo:lo + DV].astype(_F32)
            k = k_ref[0, rows, lo:lo + DV].astype(_F32)
            v = v_ref[0, rows, lo:lo + DV].astype(_F32)
            beta = jnp.broadcast_to(gb[:, HEADS + hd:HEADS + hd + 1], (c, DV))
            g_col = jnp.broadcast_to(gcum[:, hd:hd + 1], (c, DV))
            g_row = gcum_t[hd:hd + 1, :]
            g_last = gcum[c - 1:c, hd:hd + 1]
            decay = jnp.exp(jnp.where(incl, jnp.concatenate([g_col, g_col], axis=1) - g_row, _NEG))
            e_col = jnp.exp(g_col)
            kb = k * beta
            n_neg = _dot_nt(-kb, k) * decay
            n_ref[ch] = n_neg
            t_ref[ch] = jnp.where(xor == 1, n_neg, eye)
            qk_ref[ch] = (_dot_nt(q, k) * decay).astype(qk_ref.dtype)
            rhs_ref[ch] = jnp.concatenate([v * beta, kb * e_col], axis=1).astype(rhs_ref.dtype)
            qe_ref[ch] = (q * e_col).astype(qe_ref.dtype)
            kd_ref[ch] = (k * jnp.exp(g_last - g_col)).astype(kd_ref.dtype)

    m = 2
    while m < c:
        blk = lax.broadcasted_iota(jnp.int32, (c, c), 0) ^ lax.broadcasted_iota(jnp.int32, (c, c), 1)
        mask = (blk >= m) & (blk < 2 * m)
        size = m if m % 8 == 0 else c
        odd = [slice(b * m, (b + 1) * m) for b in range(1, c // m, 2)] if m % 8 == 0 else [slice(0, c)]

        def rows_of(ch, odd=odd):
            return jnp.concatenate([t_ref[ch, r, :] for r in odd], axis=0)

        left = []
        for ch in range(n_chains):
            left.append(_dot(rows_of(ch), jnp.where(mask, n_ref[ch], 0.0)))
        for ch in range(n_chains):
            upd = rows_of(ch) + _dot(left[ch], t_ref[ch])
            for j, r in enumerate(odd):
                t_ref[ch, r, :] = upd[j * size:(j + 1) * size]
        m *= 2

    for ch in range(n_chains):
        sol_ref[ch] = _dot(t_ref[ch], rhs_ref[ch]).astype(sol_ref.dtype)
    for ch in range(n_chains):
        qk_sol = _dot(qk_ref[ch], sol_ref[ch])
        ou_ref[ch] = qk_sol[:, :DV]
        qw_ref[ch] = (qe_ref[ch].astype(_F32) - qk_sol[:, DV:]).astype(qw_ref.dtype)
    for ch in range(n_chains):
        kd_sol = _dot_tn(kd_ref[ch], sol_ref[ch])
        su_ref[ch] = kd_sol[:, :DV]
        sw_ref[ch] = kd_sol[:, DV:].astype(sw_ref.dtype)
    for ci in range(n_chunks):
        rows = slice(ci * c, (ci + 1) * c)
        for hd in range(HEADS):
            ch = ci * HEADS + hd
            lo = hd * DV
            g_last = gc_ref[ci, c - 1:c, hd:hd + 1]
            state = state_ref[hd]
            o_ref[0, rows, lo:lo + DV] = (_dot(qw_ref[ch], state) + ou_ref[ch]).astype(o_ref.dtype)
            state_ref[hd] = state * jnp.exp(g_last) + su_ref[ch] - _dot(sw_ref[ch], state)


def _gdn_call(gq, gk, gv, gb):
    b, s, _ = gq.shape
    t = GDN_TILE
    n_chains = (t // GDN_CHUNK) * HEADS
    tok = lambda w: pl.BlockSpec((1, t, w), lambda i, j: (i, j, 0))
    return pl.pallas_call(
        _gdn_kernel,
        out_shape=jax.ShapeDtypeStruct((b, s, WIDTH), _BF16),
        grid=(b, s // t),
        in_specs=[tok(WIDTH), tok(WIDTH), tok(WIDTH), tok(LANES)],
        out_specs=tok(WIDTH),
        scratch_shapes=[
            pltpu.VMEM((HEADS, DV, DV), _F32),
            pltpu.VMEM((t // GDN_CHUNK, GDN_CHUNK, LANES), _F32),
            pltpu.VMEM((n_chains, GDN_CHUNK, GDN_CHUNK), _F32),
            pltpu.VMEM((n_chains, GDN_CHUNK, GDN_CHUNK), _F32),
            pltpu.VMEM((n_chains, GDN_CHUNK, GDN_CHUNK), _BF16),
            pltpu.VMEM((n_chains, GDN_CHUNK, 2 * DV), _BF16),
            pltpu.VMEM((n_chains, GDN_CHUNK, DV), _BF16),
            pltpu.VMEM((n_chains, GDN_CHUNK, DV), _BF16),
            pltpu.VMEM((n_chains, GDN_CHUNK, 2 * DV), _BF16),
            pltpu.VMEM((n_chains, GDN_CHUNK, DV), _F32),
            pltpu.VMEM((n_chains, GDN_CHUNK, DV), _BF16),
            pltpu.VMEM((n_chains, DV, DV), _F32),
            pltpu.VMEM((n_chains, DV, DV), _BF16),
        ],
        compiler_params=pltpu.CompilerParams(dimension_semantics=("arbitrary", "arbitrary"),
                                             vmem_limit_bytes=VMEM_LIMIT),
        name="gated_delta_rule",
    )(gq, gk, gv, gb)


def _out_kernel(x_ref, mod_ref, om_ref, og_ref, zg_ref, onw_ref, wout_ref, postw_ref, o_ref):
    d = x_ref.shape[-1]
    tm = x_ref.shape[1]
    w_post = mod_ref[0][:, 2 * d:3 * d] * postw_ref[...]
    for r in range(0, tm, OUT_ROWS):
        rows = slice(r, r + OUT_ROWS)
        og = og_ref[0, rows, :].astype(_F32)
        zg = _silu(zg_ref[0, rows, :].astype(_F32))
        y_gdn = jnp.concatenate([_rms(og[:, hd * DV:(hd + 1) * DV], onw_ref[...]) * zg[:, hd * DV:(hd + 1) * DV]
                                 for hd in range(HEADS)], axis=1)
        y = (jnp.dot(om_ref[0, rows, :], wout_ref[0:WIDTH, :], preferred_element_type=_F32)
             + _dot(y_gdn, wout_ref[WIDTH:2 * WIDTH, :]))
        y_n = y * lax.rsqrt(jnp.mean(y * y, axis=-1, keepdims=True) + EPS)
        o_ref[0, rows, :] = x_ref[0, rows, :] + y_n * w_post


def _out_call(layer, x, mod, o_mla, o_gdn, z_gdn, onw, wout, postw):
    b, s, d = x.shape
    tm = min(OUT_TILE, s)
    const = lambda shape: pl.BlockSpec((None,) + shape, lambda i, j: (layer,) + (0,) * len(shape))
    tok = lambda w: pl.BlockSpec((1, tm, w), lambda i, j: (i, j, 0))
    return pl.pallas_call(
        _out_kernel,
        out_shape=jax.ShapeDtypeStruct((b, s, d), _F32),
        grid=(b, s // tm),
        in_specs=[
            tok(d),
            pl.BlockSpec((None, 1, 1, 3 * d), lambda i, j: (layer, i, 0, 0)),
            tok(WIDTH), tok(WIDTH), tok(WIDTH),
            const((1, DV)),
            const((2 * WIDTH, d)),
            const((1, d)),
        ],
        out_specs=tok(d),
        compiler_params=pltpu.CompilerParams(dimension_semantics=("arbitrary", "arbitrary"),
                                             vmem_limit_bytes=VMEM_LIMIT),
        name="out_proj",
    )(x, mod, o_mla, o_gdn, z_gdn, onw, wout, postw)


def _rot_cols(w):
    half = w.shape[-1] // 2
    return jnp.concatenate([-w[..., half:], w[..., :half]], axis=-1)


def _pad_cols(w, n):
    return jnp.pad(w, [(0, 0)] * (w.ndim - 1) + [(0, n - w.shape[-1])])


def _prep_w_in(w):
    w = w.astype(_BF16)
    o_kv = Q_RANK
    o_kpe = o_kv + KV_RANK
    o_zm = o_kpe + ROPE
    o_qkv = o_zm + WIDTH
    o_a = o_qkv + 3 * WIDTH
    o_zg = o_a + 2 * HEADS
    k_pe = w[..., o_kpe:o_zm]
    return jnp.concatenate([
        w[..., :o_kpe], k_pe, _rot_cols(k_pe), w[..., o_zm:o_qkv], w[..., o_qkv:o_a],
        _pad_cols(w[..., o_a:o_zg], LANES), w[..., o_zg:],
    ], axis=-1)


def _prep_q_up(w):
    depth = w.shape[0]
    per = w.astype(_BF16).reshape(depth, Q_RANK, HEADS, DQK)
    nope = per[..., :NOPE].reshape(depth, Q_RANK, HEADS * NOPE)
    pe = [jnp.concatenate([per[:, :, h, NOPE:], _rot_cols(per[:, :, h, NOPE:])], axis=-1) for h in range(HEADS)]
    return jnp.concatenate([nope] + pe, axis=-1)


def _prep_kv_up(w):
    depth = w.shape[0]
    per = w.astype(_BF16).reshape(depth, KV_RANK, HEADS, NOPE + DV)
    return jnp.concatenate([per[..., :NOPE].reshape(depth, KV_RANK, HEADS * NOPE),
                            per[..., NOPE:].reshape(depth, KV_RANK, HEADS * DV)], axis=-1)


def _lane_rows(vals):
    return _pad_cols(vals.astype(_F32), LANES)[:, None, :]


def kernel(x, c, positions, w_mod, b_mod, pre_norm_w, post_norm_w, w_in, mla_q_norm_w, mla_q_up,
           mla_kv_norm_w, mla_kv_up, gdn_conv_w, gdn_a_log, gdn_dt_bias, gdn_o_norm_w, w_out):
    depth = w_mod.shape[0]
    b, s, d = x.shape
    assert s % TOKEN_TILE == 0 and s % GDN_TILE == 0 and s % ATTN_Q == 0 and s % OUT_TILE == 0
    assert TOKEN_TILE % ATTN_SUB == 0 and ATTN_Q % TOKEN_TILE == 0 and (ATTN_Q // ATTN_SUB) % 2 == 0
    mod = _modulation(c, w_mod, b_mod).reshape(depth, b, 1, 3 * d)
    tab = _rope_table(positions)
    in_params = (pre_norm_w[:, None, :], _prep_w_in(w_in), mla_q_norm_w[:, None, :], _prep_q_up(mla_q_up),
                 mla_kv_norm_w[:, None, :], _prep_kv_up(mla_kv_up))
    gdn_params = (gdn_conv_w, _lane_rows(gdn_a_log), _lane_rows(gdn_dt_bias))
    out_params = (gdn_o_norm_w[:, None, :], w_out.astype(_BF16), post_norm_w[:, None, :])
    for l in range(depth):
        q, k, v, z_mla, gq, gk, gv, gb, z_gdn = _in_call(l, x, mod, *in_params, tab, *gdn_params)
        o_mla = _attn_call(q, k, v, z_mla)
        o_gdn = _gdn_call(gq, gk, gv, gb)
        x = _out_call(l, x, mod, o_mla, o_gdn, z_gdn, *out_params)
    return x
```

```python
import functools

import jax
import jax.numpy as jnp
import numpy as np
from jax import lax
from jax.experimental import pallas as pl
from jax.experimental.pallas import tpu as pltpu

HEADS = 4
NOPE = 128
ROPE = 64
DQK = NOPE + ROPE
DV = 128
Q_RANK = 384
KV_RANK = 256
WIDTH = HEADS * DV
CONV_K = 4
ROPE_THETA = 10000.0
EPS = 1e-6

LANES = 128

C_QLAT = 0
C_KVLAT = C_QLAT + Q_RANK
C_KPE = C_KVLAT + KV_RANK
C_ZMLA = C_KPE + LANES
C_QKV = C_ZMLA + WIDTH
C_AB = C_QKV + 3 * WIDTH
C_ZGDN = C_AB + LANES
N_COLS = C_ZGDN + WIDTH

TOKEN_TILE = 512
ATTN_Q = 2048
ATTN_SUB = 256
ATTN_HEADS = 1
VT_ROWS = DV + 16
GDN_CHUNK = 256
GDN_TILE = 512
OUT_TILE = 2048
OUT_ROWS = 256
ROPE_TILE = 1024
VMEM_LIMIT = 56 * 1024 * 1024

_F32 = jnp.float32
_BF16 = jnp.bfloat16
_NEG = -1e30
LOG2E = float(np.log2(np.e))


def _sigmoid(x):
    return 1.0 / (1.0 + jnp.exp2(x * -LOG2E))


def _silu(x):
    h = 0.5 * x
    return h + h * jnp.tanh(h)


def _softplus(x):
    return jnp.maximum(x, 0.0) + jnp.log(1.0 + jnp.exp(-jnp.abs(x)))


def _rms(x, w):
    return x * lax.rsqrt(jnp.mean(x * x, axis=-1, keepdims=True) + EPS) * w


def _dot(a, b):
    return jnp.dot(a.astype(_BF16), b.astype(_BF16), preferred_element_type=_F32)


def _dot_nt(a, b):
    return lax.dot_general(a.astype(_BF16), b.astype(_BF16), (((1,), (1,)), ((), ())),
                           preferred_element_type=_F32)


def _dot_tn(a, b):
    return lax.dot_general(a.astype(_BF16), b.astype(_BF16), (((0,), (0,)), ((), ())),
                           preferred_element_type=_F32)


def _mod_kernel(c_ref, w_ref, b_ref, o_ref):
    c_act = _silu(c_ref[...])
    o_ref[0] = jnp.dot(c_act, w_ref[0], preferred_element_type=_F32,
                       precision=lax.Precision.HIGHEST) + b_ref[0]


def _modulation(c, w_mod, b_mod):
    depth, d, d3 = w_mod.shape
    b = c.shape[0]
    return pl.pallas_call(
        _mod_kernel,
        out_shape=jax.ShapeDtypeStruct((depth, b, d3), _F32),
        grid=(depth, d3 // d),
        in_specs=[
            pl.BlockSpec((b, d), lambda l, j: (0, 0)),
            pl.BlockSpec((1, d, d), lambda l, j: (l, 0, j)),
            pl.BlockSpec((1, 1, d), lambda l, j: (l, 0, j)),
        ],
        out_specs=pl.BlockSpec((1, b, d), lambda l, j: (l, 0, j)),
        compiler_params=pltpu.CompilerParams(dimension_semantics=("arbitrary", "arbitrary")),
        name="modulation",
    )(c, w_mod, b_mod.reshape(depth, 1, d3))


def _rope_kernel(pos_ref, freq_ref, o_ref):
    ang = pos_ref[0].astype(_F32) * freq_ref[...]
    lane = lax.broadcasted_iota(jnp.int32, ang.shape, 1)
    o_ref[0] = jnp.where(lane < ROPE, jnp.cos(ang), jnp.sin(ang))


def _rope_table(positions):
    b, s = positions.shape
    half = ROPE // 2
    inv_freq = jnp.power(ROPE_THETA, -jnp.arange(half, dtype=_F32) * 2.0 / ROPE)
    freq = jnp.tile(inv_freq, LANES // half).reshape(1, LANES)
    t = min(ROPE_TILE, s)
    return pl.pallas_call(
        _rope_kernel,
        out_shape=jax.ShapeDtypeStruct((b, s, LANES), _F32),
        grid=(b, s // t),
        in_specs=[
            pl.BlockSpec((1, t, 1), lambda i, j: (i, j, 0)),
            pl.BlockSpec((1, LANES), lambda i, j: (0, 0)),
        ],
        out_specs=pl.BlockSpec((1, t, LANES), lambda i, j: (i, j, 0)),
        compiler_params=pltpu.CompilerParams(dimension_semantics=("arbitrary", "arbitrary")),
        name="rope_table",
    )(positions.reshape(b, s, 1), freq)


def _in_kernel(x_ref, mod_ref, prew_ref, win_ref, qnw_ref, qup_ref, kvnw_ref, kvup_ref, tab_ref,
               convw_ref, alog_ref, dtb_ref,
               qt_ref, k_ref, vt_ref, zm_ref, gq_ref, gk_ref, gv_ref, gb_ref, zg_ref,
               tail_ref):
    d = x_ref.shape[-1]
    tm = x_ref.shape[1]

    @pl.when(pl.program_id(1) == 0)
    def _():
        tail_ref[tm:tm + 8, :] = jnp.zeros((8, tail_ref.shape[1]), tail_ref.dtype)

    x = x_ref[0]
    mod = mod_ref[0]
    w_eff = prew_ref[...] * (1.0 + mod[:, d:2 * d])
    h = (x * lax.rsqrt(jnp.mean(x * x, axis=-1, keepdims=True) + EPS) * w_eff + mod[:, 0:d]).astype(_BF16)

    def project(lo, hi):
        return jnp.dot(h, win_ref[:, lo:hi], preferred_element_type=_F32)

    proj = project(0, C_ZMLA)

    tab = tab_ref[0]
    q_scale = DQK ** -0.5 * LOG2E

    def rope(block):
        y = block * tab
        return y + pltpu.roll(y, ROPE, 1)

    qf = _dot(_rms(proj[:, C_QLAT:C_QLAT + Q_RANK], qnw_ref[...]), qup_ref[...])
    kvf = _dot(_rms(proj[:, C_KVLAT:C_KVLAT + KV_RANK], kvnw_ref[...]), kvup_ref[...])
    k_pe = rope(proj[:, C_KPE:C_KPE + LANES])[:, :ROPE].astype(k_ref.dtype)
    for hd in range(HEADS):
        lo = hd * NOPE
        q_pe = rope(qf[:, WIDTH + hd * LANES:WIDTH + (hd + 1) * LANES])
        qt_ref[0, hd, 0:NOPE, :] = (qf[:, lo:lo + NOPE] * q_scale).T.astype(qt_ref.dtype)
        qt_ref[0, hd, NOPE:DQK, :] = (q_pe * q_scale).T[:ROPE, :].astype(qt_ref.dtype)
        k_ref[0, hd, :, 0:NOPE] = kvf[:, lo:lo + NOPE].astype(k_ref.dtype)
        k_ref[0, hd, :, NOPE:DQK] = k_pe
        vt_ref[0, hd, 0, 0:DV, :] = kvf[:, WIDTH + lo:WIDTH + lo + DV].T.astype(vt_ref.dtype)
        vt_ref[0, hd, 0, DV:, :] = jnp.ones((VT_ROWS - DV, tm), vt_ref.dtype)
    zm_ref[0] = project(C_ZMLA, C_QKV).astype(zm_ref.dtype)

    for part, out_ref in enumerate((gq_ref, gk_ref, gv_ref)):
        cols = slice(part * WIDTH, (part + 1) * WIDTH)
        xg = project(C_QKV + part * WIDTH, C_QKV + (part + 1) * WIDTH)
        tail_ref[0:8, cols] = tail_ref[tm:tm + 8, cols]
        tail_ref[8:8 + tm, cols] = xg
        cw = convw_ref[:, cols]
        conv = xg * cw[CONV_K - 1:CONV_K, :]
        for j in range(CONV_K - 1):
            start = 8 - (CONV_K - 1 - j)
            conv = conv + tail_ref[start:start + tm, cols] * cw[j:j + 1, :]
        act = _silu(conv)
        if part == 2:
            out_ref[0] = act.astype(out_ref.dtype)
        else:
            scale = DV ** -0.5 if part == 0 else 1.0
            for hd in range(HEADS):
                lo = hd * DV
                a = act[:, lo:lo + DV]
                inv = lax.rsqrt(jnp.sum(a * a, axis=-1, keepdims=True) + EPS) * scale
                out_ref[0, :, lo:lo + DV] = (a * inv).astype(out_ref.dtype)

    gates = project(C_AB, N_COLS)
    ab = gates[:, 0:LANES]
    g = -jnp.exp(alog_ref[...]) * _softplus(ab + dtb_ref[...])
    lane = lax.broadcasted_iota(jnp.int32, ab.shape, 1)
    gb_ref[0] = jnp.where(lane < HEADS, g, _sigmoid(ab))
    zg_ref[0] = gates[:, C_ZGDN - C_AB:].astype(zg_ref.dtype)


def _in_call(layer, x, mod, prew, win, qnw, qup, kvnw, kvup, tab, convw, alog, dtb):
    b, s, d = x.shape
    tm = min(TOKEN_TILE, s)
    const = lambda shape: pl.BlockSpec((None,) + shape, lambda i, j: (layer,) + (0,) * len(shape))
    tok = lambda w: pl.BlockSpec((1, tm, w), lambda i, j: (i, j, 0))
    head = pl.BlockSpec((1, HEADS, tm, DQK), lambda i, j: (i, 0, j, 0))
    head_t = pl.BlockSpec((1, HEADS, DQK, tm), lambda i, j: (i, 0, 0, j))
    head_vt = pl.BlockSpec((1, HEADS, 1, VT_ROWS, tm), lambda i, j: (i, 0, j, 0, 0))
    out_shape = (
        jax.ShapeDtypeStruct((b, HEADS, DQK, s), _BF16),
        jax.ShapeDtypeStruct((b, HEADS, s, DQK), _BF16),
        jax.ShapeDtypeStruct((b, HEADS, s // tm, VT_ROWS, tm), _BF16),
        jax.ShapeDtypeStruct((b, s, WIDTH), _BF16),
        jax.ShapeDtypeStruct((b, s, WIDTH), _BF16),
        jax.ShapeDtypeStruct((b, s, WIDTH), _BF16),
        jax.ShapeDtypeStruct((b, s, WIDTH), _BF16),
        jax.ShapeDtypeStruct((b, s, LANES), _F32),
        jax.ShapeDtypeStruct((b, s, WIDTH), _BF16),
    )
    return pl.pallas_call(
        _in_kernel,
        out_shape=out_shape,
        grid=(b, s // tm),
        in_specs=[
            tok(d),
            pl.BlockSpec((None, 1, 1, 3 * d), lambda i, j: (layer, i, 0, 0)),
            const((1, d)),
            const((d, N_COLS)),
            const((1, Q_RANK)),
            const((Q_RANK, 2 * WIDTH)),
            const((1, KV_RANK)),
            const((KV_RANK, 2 * WIDTH)),
            tok(LANES),
            const((CONV_K, 3 * WIDTH)),
            const((1, LANES)),
            const((1, LANES)),
        ],
        out_specs=(head_t, head, head_vt, tok(WIDTH), tok(WIDTH), tok(WIDTH), tok(WIDTH),
                   tok(LANES), tok(WIDTH)),
        scratch_shapes=[pltpu.VMEM((tm + 8, 3 * WIDTH), _F32)],
        compiler_params=pltpu.CompilerParams(dimension_semantics=("arbitrary", "arbitrary"),
                                             vmem_limit_bytes=VMEM_LIMIT),
        name="in_proj",
    )(x, mod, prew, win, qnw, qup, kvnw, kvup, tab, convw, alog, dtb)


def _attn_kernel(qt_ref, k_ref, vt_ref, z_ref, o_ref, sa_ref, sb_ref, mxa_ref, mxb_ref, m_ref, acc_ref):
    tq = qt_ref.shape[3]
    uk = vt_ref.shape[4]
    uq = ATTN_SUB
    nq = tq // uq
    nu = tq // uk
    i = pl.program_id(2)
    heads = range(qt_ref.shape[1])
    qts = [[qt_ref[0, g, :, h * uq:(h + 1) * uq] for h in range(nq)] for g in heads]
    bufs = ((sa_ref, mxa_ref), (sb_ref, mxb_ref))

    m_ref[...] = jnp.full_like(m_ref, _NEG)
    acc_ref[...] = jnp.zeros_like(acc_ref)

    def produce(buf, blk, unit, subs, diagonal=False):
        s_ref, mx_ref = bufs[buf]
        for g in heads:
            kj = k_ref[0, g, pl.ds(pl.multiple_of(blk * tq + unit * uk, uk), uk), :]
            for h in subs:
                keys = min(uk, (h - unit * (uk // uq) + 1) * uq) if diagonal else uk
                s = jnp.dot(kj[0:keys], qts[g][h], preferred_element_type=_F32)
                s_ref[g * nq + h, 0:keys, :] = s
                mx_ref[g * nq + h] = jnp.max(s, axis=0, keepdims=True)

    def consume(buf, blk, unit, subs, masked=()):
        s_ref, mx_ref = bufs[buf]
        for g in heads:
            vt = vt_ref[0, g, blk * nu + unit]
            for h in subs:
                f = g * nq + h
                if h in masked:
                    own = h - unit * (uk // uq)
                    tri = (lax.broadcasted_iota(jnp.int32, (uq, uq), 0)
                           <= lax.broadcasted_iota(jnp.int32, (uq, uq), 1))
                    s = jnp.where(tri, s_ref[f, own * uq:(own + 1) * uq, :], _NEG)
                    if own > 0:
                        s = jnp.concatenate([s_ref[f, 0:own * uq, :], s], axis=0)
                    mx = jnp.max(s, axis=0, keepdims=True)
                    vth = vt_ref[0, g, blk * nu + unit, :, 0:(own + 1) * uq]
                else:
                    s = s_ref[f]
                    mx = mx_ref[f]
                    vth = vt
                m = m_ref[f]
                m_new = jnp.maximum(m, mx)
                p = jnp.exp2(s - m_new).astype(_BF16)
                acc_ref[f] = jnp.exp2(m - m_new) * acc_ref[f] + jnp.dot(vth, p, preferred_element_type=_F32)
                m_ref[f] = m_new

    every = tuple(range(nq))
    produce(0, 0, 0, every)

    def block(j):
        for unit in range(nu):
            if unit + 1 < nu:
                produce((unit + 1) % 2, j, unit + 1, every)
            else:
                produce((unit + 1) % 2, j + 1, 0, every)
            consume(unit % 2, j, unit, every)

    def pair(p, carry):
        block(2 * p)
        block(2 * p + 1)
        return carry

    lax.fori_loop(0, i // 2, pair, 0)

    @pl.when(i % 2 == 1)
    def _():
        block(i - 1)
    per = uk // uq
    for unit in range(nu):
        if unit + 1 < nu:
            produce((unit + 1) % 2, i, unit + 1, tuple(range((unit + 1) * per, nq)), diagonal=True)
        consume(unit % 2, i, unit, tuple(range(unit * per, nq)),
                masked=tuple(range(unit * per, (unit + 1) * per)))
    for g in heads:
        for h in range(nq):
            acc = acc_ref[g * nq + h]
            rows, cols = slice(h * uq, (h + 1) * uq), slice(g * DV, (g + 1) * DV)
            gate = _silu(z_ref[0, rows, cols].astype(_F32))
            o_ref[0, rows, cols] = ((acc[:DV] / acc[DV:DV + 1]).T * gate).astype(o_ref.dtype)


def _attn_call(qt, k, vt, z):
    b, _, _, s = qt.shape
    vrows, tv = vt.shape[-2:]
    t = ATTN_Q
    g = ATTN_HEADS
    nq = g * (t // ATTN_SUB)
    return pl.pallas_call(
        _attn_kernel,
        out_shape=jax.ShapeDtypeStruct((b, s, WIDTH), _BF16),
        grid=(b, HEADS // g, s // t),
        in_specs=[
            pl.BlockSpec((1, g, DQK, t), lambda bi, h, i: (bi, h, 0, i)),
            pl.BlockSpec((1, g, s, DQK), lambda bi, h, i: (bi, h, 0, 0)),
            pl.BlockSpec((1, g, s // tv, vrows, tv), lambda bi, h, i: (bi, h, 0, 0, 0)),
            pl.BlockSpec((1, t, g * DV), lambda bi, h, i: (bi, i, h)),
        ],
        out_specs=pl.BlockSpec((1, t, g * DV), lambda bi, h, i: (bi, i, h)),
        scratch_shapes=[
            pltpu.VMEM((nq, tv, ATTN_SUB), _F32),
            pltpu.VMEM((nq, tv, ATTN_SUB), _F32),
            pltpu.VMEM((nq, 1, ATTN_SUB), _F32),
            pltpu.VMEM((nq, 1, ATTN_SUB), _F32),
            pltpu.VMEM((nq, 1, ATTN_SUB), _F32),
            pltpu.VMEM((nq, vrows, ATTN_SUB), _F32),
        ],
        compiler_params=pltpu.CompilerParams(
            dimension_semantics=("arbitrary", "arbitrary", "arbitrary"), vmem_limit_bytes=VMEM_LIMIT),
        name="mla_attention",
    )(qt, k, vt, z)


def _cumsum_rows(tril_bf16, x):
    lane = lax.broadcasted_iota(jnp.int32, x.shape, 1)
    hi = x.astype(_BF16).astype(_F32)
    r1 = x - hi
    mid = r1.astype(_BF16).astype(_F32)
    low = r1 - mid
    packed = jnp.where(lane < 8, hi, jnp.where(lane < 16, pltpu.roll(mid, 8, 1), pltpu.roll(low, 16, 1)))
    res = jnp.dot(tril_bf16, packed.astype(_BF16), preferred_element_type=_F32)
    return res + pltpu.roll(res, LANES - 8, 1) + pltpu.roll(res, LANES - 16, 1)


def _gdn_kernel(q_ref, k_ref, v_ref, gb_ref, o_ref, state_ref, gc_ref, n_ref, t_ref, qk_ref, rhs_ref,
                qe_ref, kd_ref, sol_ref, ou_ref, qw_ref, su_ref, sw_ref):
    c = GDN_CHUNK
    n_chunks = q_ref.shape[1] // c
    n_chains = n_chunks * HEADS

    @pl.when(pl.program_id(1) == 0)
    def _():
        state_ref[...] = jnp.zeros_like(state_ref)

    row = lax.broadcasted_iota(jnp.int32, (c, c), 0)
    col = lax.broadcasted_iota(jnp.int32, (c, c), 1)
    incl = col <= row
    xor = row ^ col
    eye = (xor == 0).astype(_F32)
    tril = incl.astype(_BF16)

    for ci in range(n_chunks):
        rows = slice(ci * c, (ci + 1) * c)
        gb = gb_ref[0, rows, :]
        gcum = _cumsum_rows(tril, gb)
        gc_ref[ci] = gcum
        gcum_t = gcum.T
        for hd in range(HEADS):
            ch = ci * HEADS + hd
            lo = hd * DV
            q = q_ref[0, rows, lo:lo + DV].astype(_F32)
            k = k_ref[0, rows, lo:lo + DV].astype(_F32)
            v = v_ref[0, rows, lo:lo + DV].astype(_F32)
            beta = jnp.broadcast_to(gb[:, HEADS + hd:HEADS + hd + 1], (c, DV))
            g_col = jnp.broadcast_to(gcum[:, hd:hd + 1], (c, DV))
            g_row = gcum_t[hd:hd + 1, :]
            g_last = gcum[c - 1:c, hd:hd + 1]
            decay = jnp.exp(jnp.where(incl, jnp.concatenate([g_col, g_col], axis=1) - g_row, _NEG))
            e_col = jnp.exp(g_col)
            kb = k * beta
            n_neg = _dot_nt(-kb, k) * decay
            n_ref[ch] = n_neg
            t_ref[ch] = jnp.where(xor == 1, n_neg, eye)
            qk_ref[ch] = (_dot_nt(q, k) * decay).astype(qk_ref.dtype)
            rhs_ref[ch] = jnp.concatenate([v * beta, kb * e_col], axis=1).astype(rhs_ref.dtype)
            qe_ref[ch] = (q * e_col).astype(qe_ref.dtype)
            kd_ref[ch] = (k * jnp.exp(g_last - g_col)).astype(kd_ref.dtype)

    m = 2
    while m < c:
        blk = lax.broadcasted_iota(jnp.int32, (c, c), 0) ^ lax.broadcasted_iota(jnp.int32, (c, c), 1)
        mask = (blk >= m) & (blk < 2 * m)
        size = m if m % 8 == 0 else c
        odd = [slice(b * m, (b + 1) * m) for b in range(1, c // m, 2)] if m % 8 == 0 else [slice(0, c)]

        def rows_of(ch, odd=odd):
            return jnp.concatenate([t_ref[ch, r, :] for r in odd], axis=0)

        left = []
        for ch in range(n_chains):
            left.append(_dot(rows_of(ch), jnp.where(mask, n_ref[ch], 0.0)))
        for ch in range(n_chains):
            upd = rows_of(ch) + _dot(left[ch], t_ref[ch])
            for j, r in enumerate(odd):
                t_ref[ch, r, :] = upd[j * size:(j + 1) * size]
        m *= 2

    for ch in range(n_chains):
        sol_ref[ch] = _dot(t_ref[ch], rhs_ref[ch]).astype(sol_ref.dtype)
    for ch in range(n_chains):
        qk_sol = _dot(qk_ref[ch], sol_ref[ch])
        ou_ref[ch] = qk_sol[:, :DV]
        qw_ref[ch] = (qe_ref[ch].astype(_F32) - qk_sol[:, DV:]).astype(qw_ref.dtype)
    for ch in range(n_chains):
        kd_sol = _dot_tn(kd_ref[ch], sol_ref[ch])
        su_ref[ch] = kd_sol[:, :DV]
        sw_ref[ch] = kd_sol[:, DV:].astype(sw_ref.dtype)
    for ci in range(n_chunks):
        rows = slice(ci * c, (ci + 1) * c)
        for hd in range(HEADS):
            ch = ci * HEADS + hd
            lo = hd * DV
            g_last = gc_ref[ci, c - 1:c, hd:hd + 1]
            state = state_ref[hd]
            o_ref[0, rows, lo:lo + DV] = (_dot(qw_ref[ch], state) + ou_ref[ch]).astype(o_ref.dtype)
            state_ref[hd] = state * jnp.exp(g_last) + su_ref[ch] - _dot(sw_ref[ch], state)


def _gdn_call(gq, gk, gv, gb):
    b, s, _ = gq.shape
    t = GDN_TILE
    n_chains = (t // GDN_CHUNK) * HEADS
    tok = lambda w: pl.BlockSpec((1, t, w), lambda i, j: (i, j, 0))
    return pl.pallas_call(
        _gdn_kernel,
        out_shape=jax.ShapeDtypeStruct((b, s, WIDTH), _BF16),
        grid=(b, s // t),
        in_specs=[tok(WIDTH), tok(WIDTH), tok(WIDTH), tok(LANES)],
        out_specs=tok(WIDTH),
        scratch_shapes=[
            pltpu.VMEM((HEADS, DV, DV), _F32),
            pltpu.VMEM((t // GDN_CHUNK, GDN_CHUNK, LANES), _F32),
            pltpu.VMEM((n_chains, GDN_CHUNK, GDN_CHUNK), _F32),
            pltpu.VMEM((n_chains, GDN_CHUNK, GDN_CHUNK), _F32),
            pltpu.VMEM((n_chains, GDN_CHUNK, GDN_CHUNK), _BF16),
            pltpu.VMEM((n_chains, GDN_CHUNK, 2 * DV), _BF16),
            pltpu.VMEM((n_chains, GDN_CHUNK, DV), _BF16),
            pltpu.VMEM((n_chains, GDN_CHUNK, DV), _BF16),
            pltpu.VMEM((n_chains, GDN_CHUNK, 2 * DV), _BF16),
            pltpu.VMEM((n_chains, GDN_CHUNK, DV), _F32),
            pltpu.VMEM((n_chains, GDN_CHUNK, DV), _BF16),
            pltpu.VMEM((n_chains, DV, DV), _F32),
            pltpu.VMEM((n_chains, DV, DV), _BF16),
        ],
        compiler_params=pltpu.CompilerParams(dimension_semantics=("arbitrary", "arbitrary"),
                                             vmem_limit_bytes=VMEM_LIMIT),
        name="gated_delta_rule",
    )(gq, gk, gv, gb)


def _out_kernel(x_ref, mod_ref, om_ref, og_ref, zg_ref, onw_ref, wout_ref, postw_ref, o_ref):
    d = x_ref.shape[-1]
    tm = x_ref.shape[1]
    w_post = mod_ref[0][:, 2 * d:3 * d] * postw_ref[...]
    for r in range(0, tm, OUT_ROWS):
        rows = slice(r, r + OUT_ROWS)
        og = og_ref[0, rows, :].astype(_F32)
        zg = _silu(zg_ref[0, rows, :].astype(_F32))
        y_gdn = jnp.concatenate([_rms(og[:, hd * DV:(hd + 1) * DV], onw_ref[...]) * zg[:, hd * DV:(hd + 1) * DV]
                                 for hd in range(HEADS)], axis=1)
        y = (jnp.dot(om_ref[0, rows, :], wout_ref[0:WIDTH, :], preferred_element_type=_F32)
             + _dot(y_gdn, wout_ref[WIDTH:2 * WIDTH, :]))
        y_n = y * lax.rsqrt(jnp.mean(y * y, axis=-1, keepdims=True) + EPS)
        o_ref[0, rows, :] = x_ref[0, rows, :] + y_n * w_post


def _out_call(layer, x, mod, o_mla, o_gdn, z_gdn, onw, wout, postw):
    b, s, d = x.shape
    tm = min(OUT_TILE, s)
    const = lambda shape: pl.BlockSpec((None,) + shape, lambda i, j: (layer,) + (0,) * len(shape))
    tok = lambda w: pl.BlockSpec((1, tm, w), lambda i, j: (i, j, 0))
    return pl.pallas_call(
        _out_kernel,
        out_shape=jax.ShapeDtypeStruct((b, s, d), _F32),
        grid=(b, s // tm),
        in_specs=[
            tok(d),
            pl.BlockSpec((None, 1, 1, 3 * d), lambda i, j: (layer, i, 0, 0)),
            tok(WIDTH), tok(WIDTH), tok(WIDTH),
            const((1, DV)),
            const((2 * WIDTH, d)),
            const((1, d)),
        ],
        out_specs=tok(d),
        compiler_params=pltpu.CompilerParams(dimension_semantics=("arbitrary", "arbitrary"),
                                             vmem_limit_bytes=VMEM_LIMIT),
        name="out_proj",
    )(x, mod, o_mla, o_gdn, z_gdn, onw, wout, postw)


def _rot_cols(w):
    half = w.shape[-1] // 2
    return jnp.concatenate([-w[..., half:], w[..., :half]], axis=-1)


def _pad_cols(w, n):
    return jnp.pad(w, [(0, 0)] * (w.ndim - 1) + [(0, n - w.shape[-1])])


def _prep_w_in(w):
    w = w.astype(_BF16)
    o_kv = Q_RANK
    o_kpe = o_kv + KV_RANK
    o_zm = o_kpe + ROPE
    o_qkv = o_zm + WIDTH
    o_a = o_qkv + 3 * WIDTH
    o_zg = o_a + 2 * HEADS
    k_pe = w[..., o_kpe:o_zm]
    return jnp.concatenate([
        w[..., :o_kpe], k_pe, _rot_cols(k_pe), w[..., o_zm:o_qkv], w[..., o_qkv:o_a],
        _pad_cols(w[..., o_a:o_zg], LANES), w[..., o_zg:],
    ], axis=-1)


def _prep_q_up(w):
    depth = w.shape[0]
    per = w.astype(_BF16).reshape(depth, Q_RANK, HEADS, DQK)
    nope = per[..., :NOPE].reshape(depth, Q_RANK, HEADS * NOPE)
    pe = [jnp.concatenate([per[:, :, h, NOPE:], _rot_cols(per[:, :, h, NOPE:])], axis=-1) for h in range(HEADS)]
    return jnp.concatenate([nope] + pe, axis=-1)


def _prep_kv_up(w):
    depth = w.shape[0]
    per = w.astype(_BF16).reshape(depth, KV_RANK, HEADS, NOPE + DV)
    return jnp.concatenate([per[..., :NOPE].reshape(depth, KV_RANK, HEADS * NOPE),
                            per[..., NOPE:].reshape(depth, KV_RANK, HEADS * DV)], axis=-1)


def _lane_rows(vals):
    return _pad_cols(vals.astype(_F32), LANES)[:, None, :]


def kernel(x, c, positions, w_mod, b_mod, pre_norm_w, post_norm_w, w_in, mla_q_norm_w, mla_q_up,
           mla_kv_norm_w, mla_kv_up, gdn_conv_w, gdn_a_log, gdn_dt_bias, gdn_o_norm_w, w_out):
    depth = w_mod.shape[0]
    b, s, d = x.shape
    assert s % TOKEN_TILE == 0 and s % GDN_TILE == 0 and s % ATTN_Q == 0 and s % OUT_TILE == 0
    assert TOKEN_TILE % ATTN_SUB == 0 and ATTN_Q % TOKEN_TILE == 0 and (ATTN_Q // ATTN_SUB) % 2 == 0
    mod = _modulation(c, w_mod, b_mod).reshape(depth, b, 1, 3 * d)
    tab = _rope_table(positions)
    in_params = (pre_norm_w[:, None, :], _prep_w_in(w_in), mla_q_norm_w[:, None, :], _prep_q_up(mla_q_up),
                 mla_kv_norm_w[:, None, :], _prep_kv_up(mla_kv_up))
    gdn_params = (gdn_conv_w, _lane_rows(gdn_a_log), _lane_rows(gdn_dt_bias))
    out_params = (gdn_o_norm_w[:, None, :], w_out.astype(_BF16), post_norm_w[:, None, :])
    for l in range(depth):
        q, k, v, z_mla, gq, gk, gv, gb, z_gdn = _in_call(l, x, mod, *in_params, tab, *gdn_params)
        o_mla = _attn_call(q, k, v, z_mla)
        o_gdn = _gdn_call(gq, gk, gv, gb)
        x = _out_call(l, x, mod, o_mla, o_gdn, z_gdn, *out_params)
    return x
```

```python
import functools

import jax
import jax.numpy as jnp
import numpy as np
from jax import lax
from jax.experimental import pallas as pl
from jax.experimental.pallas import tpu as pltpu

HEADS = 4
NOPE = 128
ROPE = 64
DQK = NOPE + ROPE
DV = 128
Q_RANK = 384
KV_RANK = 256
WIDTH = HEADS * DV
CONV_K = 4
ROPE_THETA = 10000.0
EPS = 1e-6

LANES = 128

C_QLAT = 0
C_KVLAT = C_QLAT + Q_RANK
C_KPE = C_KVLAT + KV_RANK
C_ZMLA = C_KPE + LANES
C_QKV = C_ZMLA + WIDTH
C_AB = C_QKV + 3 * WIDTH
C_ZGDN = C_AB + LANES
N_COLS = C_ZGDN + WIDTH

TOKEN_TILE = 512
ATTN_Q = 2048
ATTN_SUB = 256
ATTN_HEADS = 1
VT_ROWS = DV + 16
GDN_CHUNK = 256
GDN_TILE = 1024
OUT_TILE = 2048
OUT_ROWS = 256
ROPE_TILE = 1024
VMEM_LIMIT = 56 * 1024 * 1024

_F32 = jnp.float32
_BF16 = jnp.bfloat16
_NEG = -1e30
LOG2E = float(np.log2(np.e))


def _sigmoid(x):
    return 1.0 / (1.0 + jnp.exp2(x * -LOG2E))


def _silu(x):
    h = 0.5 * x
    return h + h * jnp.tanh(h)


def _softplus(x):
    return jnp.maximum(x, 0.0) + jnp.log(1.0 + jnp.exp(-jnp.abs(x)))


def _rms(x, w):
    return x * lax.rsqrt(jnp.mean(x * x, axis=-1, keepdims=True) + EPS) * w


def _dot(a, b):
    return jnp.dot(a.astype(_BF16), b.astype(_BF16), preferred_element_type=_F32)


def _dot_nt(a, b):
    return lax.dot_general(a.astype(_BF16), b.astype(_BF16), (((1,), (1,)), ((), ())),
                           preferred_element_type=_F32)


def _dot_tn(a, b):
    return lax.dot_general(a.astype(_BF16), b.astype(_BF16), (((0,), (0,)), ((), ())),
                           preferred_element_type=_F32)


def _mod_kernel(c_ref, w_ref, b_ref, o_ref):
    c_act = _silu(c_ref[...])
    o_ref[0] = jnp.dot(c_act, w_ref[0], preferred_element_type=_F32,
                       precision=lax.Precision.HIGHEST) + b_ref[0]


def _modulation(c, w_mod, b_mod):
    depth, d, d3 = w_mod.shape
    b = c.shape[0]
    return pl.pallas_call(
        _mod_kernel,
        out_shape=jax.ShapeDtypeStruct((depth, b, d3), _F32),
        grid=(depth, d3 // d),
        in_specs=[
            pl.BlockSpec((b, d), lambda l, j: (0, 0)),
            pl.BlockSpec((1, d, d), lambda l, j: (l, 0, j)),
            pl.BlockSpec((1, 1, d), lambda l, j: (l, 0, j)),
        ],
        out_specs=pl.BlockSpec((1, b, d), lambda l, j: (l, 0, j)),
        compiler_params=pltpu.CompilerParams(dimension_semantics=("arbitrary", "arbitrary")),
        name="modulation",
    )(c, w_mod, b_mod.reshape(depth, 1, d3))


def _rope_kernel(pos_ref, freq_ref, o_ref):
    ang = pos_ref[0].astype(_F32) * freq_ref[...]
    lane = lax.broadcasted_iota(jnp.int32, ang.shape, 1)
    o_ref[0] = jnp.where(lane < ROPE, jnp.cos(ang), jnp.sin(ang))


def _rope_table(positions):
    b, s = positions.shape
    half = ROPE // 2
    inv_freq = jnp.power(ROPE_THETA, -jnp.arange(half, dtype=_F32) * 2.0 / ROPE)
    freq = jnp.tile(inv_freq, LANES // half).reshape(1, LANES)
    t = min(ROPE_TILE, s)
    return pl.pallas_call(
        _rope_kernel,
        out_shape=jax.ShapeDtypeStruct((b, s, LANES), _F32),
        grid=(b, s // t),
        in_specs=[
            pl.BlockSpec((1, t, 1), lambda i, j: (i, j, 0)),
            pl.BlockSpec((1, LANES), lambda i, j: (0, 0)),
        ],
        out_specs=pl.BlockSpec((1, t, LANES), lambda i, j: (i, j, 0)),
        compiler_params=pltpu.CompilerParams(dimension_semantics=("arbitrary", "arbitrary")),
        name="rope_table",
    )(positions.reshape(b, s, 1), freq)


def _in_kernel(x_ref, mod_ref, prew_ref, win_ref, qnw_ref, qup_ref, kvnw_ref, kvup_ref, tab_ref,
               convw_ref, alog_ref, dtb_ref,
               qt_ref, k_ref, vt_ref, zm_ref, gq_ref, gk_ref, gv_ref, gb_ref, zg_ref,
               tail_ref):
    d = x_ref.shape[-1]
    tm = x_ref.shape[1]

    @pl.when(pl.program_id(1) == 0)
    def _():
        tail_ref[tm:tm + 8, :] = jnp.zeros((8, tail_ref.shape[1]), tail_ref.dtype)

    x = x_ref[0]
    mod = mod_ref[0]
    w_eff = prew_ref[...] * (1.0 + mod[:, d:2 * d])
    h = (x * lax.rsqrt(jnp.mean(x * x, axis=-1, keepdims=True) + EPS) * w_eff + mod[:, 0:d]).astype(_BF16)

    def project(lo, hi):
        return jnp.dot(h, win_ref[:, lo:hi], preferred_element_type=_F32)

    proj = project(0, C_ZMLA)

    tab = tab_ref[0]
    q_scale = DQK ** -0.5 * LOG2E

    def rope(block):
        y = block * tab
        return y + pltpu.roll(y, ROPE, 1)

    qf = _dot(_rms(proj[:, C_QLAT:C_QLAT + Q_RANK], qnw_ref[...]), qup_ref[...])
    kvf = _dot(_rms(proj[:, C_KVLAT:C_KVLAT + KV_RANK], kvnw_ref[...]), kvup_ref[...])
    k_pe = rope(proj[:, C_KPE:C_KPE + LANES])[:, :ROPE].astype(k_ref.dtype)
    for hd in range(HEADS):
        lo = hd * NOPE
        q_pe = rope(qf[:, WIDTH + hd * LANES:WIDTH + (hd + 1) * LANES])
        qt_ref[0, hd, 0:NOPE, :] = (qf[:, lo:lo + NOPE] * q_scale).T.astype(qt_ref.dtype)
        qt_ref[0, hd, NOPE:DQK, :] = (q_pe * q_scale).T[:ROPE, :].astype(qt_ref.dtype)
        k_ref[0, hd, :, 0:NOPE] = kvf[:, lo:lo + NOPE].astype(k_ref.dtype)
        k_ref[0, hd, :, NOPE:DQK] = k_pe
        vt_ref[0, hd, 0, 0:DV, :] = kvf[:, WIDTH + lo:WIDTH + lo + DV].T.astype(vt_ref.dtype)
        vt_ref[0, hd, 0, DV:, :] = jnp.ones((VT_ROWS - DV, tm), vt_ref.dtype)
    zm_ref[0] = project(C_ZMLA, C_QKV).astype(zm_ref.dtype)

    for part, out_ref in enumerate((gq_ref, gk_ref, gv_ref)):
        cols = slice(part * WIDTH, (part + 1) * WIDTH)
        xg = project(C_QKV + part * WIDTH, C_QKV + (part + 1) * WIDTH)
        tail_ref[0:8, cols] = tail_ref[tm:tm + 8, cols]
        tail_ref[8:8 + tm, cols] = xg
        cw = convw_ref[:, cols]
        conv = xg * cw[CONV_K - 1:CONV_K, :]
        for j in range(CONV_K - 1):
            start = 8 - (CONV_K - 1 - j)
            conv = conv + tail_ref[start:start + tm, cols] * cw[j:j + 1, :]
        act = _silu(conv)
        if part == 2:
            out_ref[0] = act.astype(out_ref.dtype)
        else:
            scale = DV ** -0.5 if part == 0 else 1.0
            for hd in range(HEADS):
                lo = hd * DV
                a = act[:, lo:lo + DV]
                inv = lax.rsqrt(jnp.sum(a * a, axis=-1, keepdims=True) + EPS) * scale
                out_ref[0, :, lo:lo + DV] = (a * inv).astype(out_ref.dtype)

    gates = project(C_AB, N_COLS)
    ab = gates[:, 0:LANES]
    g = -jnp.exp(alog_ref[...]) * _softplus(ab + dtb_ref[...])
    lane = lax.broadcasted_iota(jnp.int32, ab.shape, 1)
    gb_ref[0] = jnp.where(lane < HEADS, g, _sigmoid(ab))
    zg_ref[0] = gates[:, C_ZGDN - C_AB:].astype(zg_ref.dtype)


def _in_call(layer, x, mod, prew, win, qnw, qup, kvnw, kvup, tab, convw, alog, dtb):
    b, s, d = x.shape
    tm = min(TOKEN_TILE, s)
    const = lambda shape: pl.BlockSpec((None,) + shape, lambda i, j: (layer,) + (0,) * len(shape))
    tok = lambda w: pl.BlockSpec((1, tm, w), lambda i, j: (i, j, 0))
    head = pl.BlockSpec((1, HEADS, tm, DQK), lambda i, j: (i, 0, j, 0))
    head_t = pl.BlockSpec((1, HEADS, DQK, tm), lambda i, j: (i, 0, 0, j))
    head_vt = pl.BlockSpec((1, HEADS, 1, VT_ROWS, tm), lambda i, j: (i, 0, j, 0, 0))
    out_shape = (
        jax.ShapeDtypeStruct((b, HEADS, DQK, s), _BF16),
        jax.ShapeDtypeStruct((b, HEADS, s, DQK), _BF16),
        jax.ShapeDtypeStruct((b, HEADS, s // tm, VT_ROWS, tm), _BF16),
        jax.ShapeDtypeStruct((b, s, WIDTH), _BF16),
        jax.ShapeDtypeStruct((b, s, WIDTH), _BF16),
        jax.ShapeDtypeStruct((b, s, WIDTH), _BF16),
        jax.ShapeDtypeStruct((b, s, WIDTH), _BF16),
        jax.ShapeDtypeStruct((b, s, LANES), _F32),
        jax.ShapeDtypeStruct((b, s, WIDTH), _BF16),
    )
    return pl.pallas_call(
        _in_kernel,
        out_shape=out_shape,
        grid=(b, s // tm),
        in_specs=[
            tok(d),
            pl.BlockSpec((None, 1, 1, 3 * d), lambda i, j: (layer, i, 0, 0)),
            const((1, d)),
            const((d, N_COLS)),
            const((1, Q_RANK)),
            const((Q_RANK, 2 * WIDTH)),
            const((1, KV_RANK)),
            const((KV_RANK, 2 * WIDTH)),
            tok(LANES),
            const((CONV_K, 3 * WIDTH)),
            const((1, LANES)),
            const((1, LANES)),
        ],
        out_specs=(head_t, head, head_vt, tok(WIDTH), tok(WIDTH), tok(WIDTH), tok(WIDTH),
                   tok(LANES), tok(WIDTH)),
        scratch_shapes=[pltpu.VMEM((tm + 8, 3 * WIDTH), _F32)],
        compiler_params=pltpu.CompilerParams(dimension_semantics=("arbitrary", "arbitrary"),
                                             vmem_limit_bytes=VMEM_LIMIT),
        name="in_proj",
    )(x, mod, prew, win, qnw, qup, kvnw, kvup, tab, convw, alog, dtb)


def _attn_kernel(qt_ref, k_ref, vt_ref, z_ref, o_ref, sa_ref, sb_ref, mxa_ref, mxb_ref, m_ref, acc_ref):
    tq = qt_ref.shape[3]
    uk = vt_ref.shape[4]
    uq = ATTN_SUB
    nq = tq // uq
    nu = tq // uk
    i = pl.program_id(2)
    heads = range(qt_ref.shape[1])
    qts = [[qt_ref[0, g, :, h * uq:(h + 1) * uq] for h in range(nq)] for g in heads]
    bufs = ((sa_ref, mxa_ref), (sb_ref, mxb_ref))

    m_ref[...] = jnp.full_like(m_ref, _NEG)
    acc_ref[...] = jnp.zeros_like(acc_ref)

    def produce(buf, blk, unit, subs, diagonal=False):
        s_ref, mx_ref = bufs[buf]
        for g in heads:
            kj = k_ref[0, g, pl.ds(pl.multiple_of(blk * tq + unit * uk, uk), uk), :]
            for h in subs:
                keys = min(uk, (h - unit * (uk // uq) + 1) * uq) if diagonal else uk
                s = jnp.dot(kj[0:keys], qts[g][h], preferred_element_type=_F32)
                s_ref[g * nq + h, 0:keys, :] = s
                mx_ref[g * nq + h] = jnp.max(s, axis=0, keepdims=True)

    def consume(buf, blk, unit, subs, masked=()):
        s_ref, mx_ref = bufs[buf]
        for g in heads:
            vt = vt_ref[0, g, blk * nu + unit]
            for h in subs:
                f = g * nq + h
                if h in masked:
                    own = h - unit * (uk // uq)
                    tri = (lax.broadcasted_iota(jnp.int32, (uq, uq), 0)
                           <= lax.broadcasted_iota(jnp.int32, (uq, uq), 1))
                    s = jnp.where(tri, s_ref[f, own * uq:(own + 1) * uq, :], _NEG)
                    if own > 0:
                        s = jnp.concatenate([s_ref[f, 0:own * uq, :], s], axis=0)
                    mx = jnp.max(s, axis=0, keepdims=True)
                    vth = vt_ref[0, g, blk * nu + unit, :, 0:(own + 1) * uq]
                else:
                    s = s_ref[f]
                    mx = mx_ref[f]
                    vth = vt
                m = m_ref[f]
                m_new = jnp.maximum(m, mx)
                p = jnp.exp2(s - m_new).astype(_BF16)
                acc_ref[f] = jnp.exp2(m - m_new) * acc_ref[f] + jnp.dot(vth, p, preferred_element_type=_F32)
                m_ref[f] = m_new

    every = tuple(range(nq))
    produce(0, 0, 0, every)

    def block(j):
        for unit in range(nu):
            if unit + 1 < nu:
                produce((unit + 1) % 2, j, unit + 1, every)
            else:
                produce((unit + 1) % 2, j + 1, 0, every)
            consume(unit % 2, j, unit, every)

    def pair(p, carry):
        block(2 * p)
        block(2 * p + 1)
        return carry

    lax.fori_loop(0, i // 2, pair, 0)

    @pl.when(i % 2 == 1)
    def _():
        block(i - 1)
    per = uk // uq
    for unit in range(nu):
        if unit + 1 < nu:
            produce((unit + 1) % 2, i, unit + 1, tuple(range((unit + 1) * per, nq)), diagonal=True)
        consume(unit % 2, i, unit, tuple(range(unit * per, nq)),
                masked=tuple(range(unit * per, (unit + 1) * per)))
    for g in heads:
        for h in range(nq):
            acc = acc_ref[g * nq + h]
            rows, cols = slice(h * uq, (h + 1) * uq), slice(g * DV, (g + 1) * DV)
            gate = _silu(z_ref[0, rows, cols].astype(_F32))
            o_ref[0, rows, cols] = ((acc[:DV] / acc[DV:DV + 1]).T * gate).astype(o_ref.dtype)


def _attn_call(qt, k, vt, z):
    b, _, _, s = qt.shape
    vrows, tv = vt.shape[-2:]
    t = ATTN_Q
    g = ATTN_HEADS
    nq = g * (t // ATTN_SUB)
    return pl.pallas_call(
        _attn_kernel,
        out_shape=jax.ShapeDtypeStruct((b, s, WIDTH), _BF16),
        grid=(b, HEADS // g, s // t),
        in_specs=[
            pl.BlockSpec((1, g, DQK, t), lambda bi, h, i: (bi, h, 0, i)),
            pl.BlockSpec((1, g, s, DQK), lambda bi, h, i: (bi, h, 0, 0)),
            pl.BlockSpec((1, g, s // tv, vrows, tv), lambda bi, h, i: (bi, h, 0, 0, 0)),
            pl.BlockSpec((1, t, g * DV), lambda bi, h, i: (bi, i, h)),
        ],
        out_specs=pl.BlockSpec((1, t, g * DV), lambda bi, h, i: (bi, i, h)),
        scratch_shapes=[
            pltpu.VMEM((nq, tv, ATTN_SUB), _F32),
            pltpu.VMEM((nq, tv, ATTN_SUB), _F32),
            pltpu.VMEM((nq, 1, ATTN_SUB), _F32),
            pltpu.VMEM((nq, 1, ATTN_SUB), _F32),
            pltpu.VMEM((nq, 1, ATTN_SUB), _F32),
            pltpu.VMEM((nq, vrows, ATTN_SUB), _F32),
        ],
        compiler_params=pltpu.CompilerParams(
            dimension_semantics=("arbitrary", "arbitrary", "arbitrary"), vmem_limit_bytes=VMEM_LIMIT),
        name="mla_attention",
    )(qt, k, vt, z)


def _cumsum_rows(tril_bf16, x):
    lane = lax.broadcasted_iota(jnp.int32, x.shape, 1)
    hi = x.astype(_BF16).astype(_F32)
    r1 = x - hi
    mid = r1.astype(_BF16).astype(_F32)
    low = r1 - mid
    packed = jnp.where(lane < 8, hi, jnp.where(lane < 16, pltpu.roll(mid, 8, 1), pltpu.roll(low, 16, 1)))
    res = jnp.dot(tril_bf16, packed.astype(_BF16), preferred_element_type=_F32)
    return res + pltpu.roll(res, LANES - 8, 1) + pltpu.roll(res, LANES - 16, 1)


def _gdn_kernel(q_ref, k_ref, v_ref, gb_ref, o_ref, state_ref, gc_ref, n_ref, t_ref, qk_ref, rhs_ref,
                qe_ref, kd_ref, sol_ref, ou_ref, qw_ref, su_ref, sw_ref):
    c = GDN_CHUNK
    n_chunks = q_ref.shape[1] // c
    n_chains = n_chunks * HEADS

    @pl.when(pl.program_id(1) == 0)
    def _():
        state_ref[...] = jnp.zeros_like(state_ref)

    row = lax.broadcasted_iota(jnp.int32, (c, c), 0)
    col = lax.broadcasted_iota(jnp.int32, (c, c), 1)
    incl = col <= row
    xor = row ^ col
    eye = (xor == 0).astype(_F32)
    tril = incl.astype(_BF16)

    for ci in range(n_chunks):
        rows = slice(ci * c, (ci + 1) * c)
        gb = gb_ref[0, rows, :]
        gcum = _cumsum_rows(tril, gb)
        gc_ref[ci] = gcum
        gcum_t = gcum.T
        for hd in range(HEADS):
            ch = ci * HEADS + hd
            lo = hd * DV
            q = q_ref[0, rows, lo:lo + DV].astype(_F32)
            k = k_ref[0, rows, lo:lo + DV].astype(_F32)
            v = v_ref[0, rows, lo:lo + DV].astype(_F32)
            beta = jnp.broadcast_to(gb[:, HEADS + hd:HEADS + hd + 1], (c, DV))
            g_col = jnp.broadcast_to(gcum[:, hd:hd + 1], (c, DV))
            g_row = gcum_t[hd:hd + 1, :]
            g_last = gcum[c - 1:c, hd:hd + 1]
            decay = jnp.exp(jnp.where(incl, jnp.concatenate([g_col, g_col], axis=1) - g_row, _NEG))
            e_col = jnp.exp(g_col)
            kb = k * beta
            n_neg = _dot_nt(-kb, k) * decay
            n_ref[ch] = n_neg
            t_ref[ch] = jnp.where(xor == 1, n_neg, eye)
            qk_ref[ch] = (_dot_nt(q, k) * decay).astype(qk_ref.dtype)
            rhs_ref[ch] = jnp.concatenate([v * beta, kb * e_col], axis=1).astype(rhs_ref.dtype)
            qe_ref[ch] = (q * e_col).astype(qe_ref.dtype)
            kd_ref[ch] = (k * jnp.exp(g_last - g_col)).astype(kd_ref.dtype)

    m = 2
    while m < c:
        blk = lax.broadcasted_iota(jnp.int32, (c, c), 0) ^ lax.broadcasted_iota(jnp.int32, (c, c), 1)
        mask = (blk >= m) & (blk < 2 * m)
        size = m if m % 8 == 0 else c
        odd = [slice(b * m, (b + 1) * m) for b in range(1, c // m, 2)] if m % 8 == 0 else [slice(0, c)]

        def rows_of(ch, odd=odd):
            return jnp.concatenate([t_ref[ch, r, :] for r in odd], axis=0)

        left = []
        for ch in range(n_chains):
            left.append(_dot(rows_of(ch), jnp.where(mask, n_ref[ch], 0.0)))
        for ch in range(n_chains):
            upd = rows_of(ch) + _dot(left[ch], t_ref[ch])
            for j, r in enumerate(odd):
                t_ref[ch, r, :] = upd[j * size:(j + 1) * size]
        m *= 2

    for ch in range(n_chains):
        sol_ref[ch] = _dot(t_ref[ch], rhs_ref[ch]).astype(sol_ref.dtype)
    for ch in range(n_chains):
        qk_sol = _dot(qk_ref[ch], sol_ref[ch])
        ou_ref[ch] = qk_sol[:, :DV]
        qw_ref[ch] = (qe_ref[ch].astype(_F32) - qk_sol[:, DV:]).astype(qw_ref.dtype)
    for ch in range(n_chains):
        kd_sol = _dot_tn(kd_ref[ch], sol_ref[ch])
        su_ref[ch] = kd_sol[:, :DV]
        sw_ref[ch] = kd_sol[:, DV:].astype(sw_ref.dtype)
    for ci in range(n_chunks):
        rows = slice(ci * c, (ci + 1) * c)
        for hd in range(HEADS):
            ch = ci * HEADS + hd
            lo = hd * DV
            g_last = gc_ref[ci, c - 1:c, hd:hd + 1]
            state = state_ref[hd]
            o_ref[0, rows, lo:lo + DV] = (_dot(qw_ref[ch], state) + ou_ref[ch]).astype(o_ref.dtype)
            state_ref[hd] = state * jnp.exp(g_last) + su_ref[ch] - _dot(sw_ref[ch], state)


def _gdn_call(gq, gk, gv, gb):
    b, s, _ = gq.shape
    t = GDN_TILE
    n_chains = (t // GDN_CHUNK) * HEADS
    tok = lambda w: pl.BlockSpec((1, t, w), lambda i, j: (i, j, 0))
    return pl.pallas_call(
        _gdn_kernel,
        out_shape=jax.ShapeDtypeStruct((b, s, WIDTH), _BF16),
        grid=(b, s // t),
        in_specs=[tok(WIDTH), tok(WIDTH), tok(WIDTH), tok(LANES)],
        out_specs=tok(WIDTH),
        scratch_shapes=[
            pltpu.VMEM((HEADS, DV, DV), _F32),
            pltpu.VMEM((t // GDN_CHUNK, GDN_CHUNK, LANES), _F32),
            pltpu.VMEM((n_chains, GDN_CHUNK, GDN_CHUNK), _F32),
            pltpu.VMEM((n_chains, GDN_CHUNK, GDN_CHUNK), _F32),
            pltpu.VMEM((n_chains, GDN_CHUNK, GDN_CHUNK), _BF16),
            pltpu.VMEM((n_chains, GDN_CHUNK, 2 * DV), _BF16),
            pltpu.VMEM((n_chains, GDN_CHUNK, DV), _BF16),
            pltpu.VMEM((n_chains, GDN_CHUNK, DV), _BF16),
            pltpu.VMEM((n_chains, GDN_CHUNK, 2 * DV), _BF16),
            pltpu.VMEM((n_chains, GDN_CHUNK, DV), _F32),
            pltpu.VMEM((n_chains, GDN_CHUNK, DV), _BF16),
            pltpu.VMEM((n_chains, DV, DV), _F32),
            pltpu.VMEM((n_chains, DV, DV), _BF16),
        ],
        compiler_params=pltpu.CompilerParams(dimension_semantics=("arbitrary", "arbitrary"),
                                             vmem_limit_bytes=VMEM_LIMIT),
        name="gated_delta_rule",
    )(gq, gk, gv, gb)


def _out_kernel(x_ref, mod_ref, om_ref, og_ref, zg_ref, onw_ref, wout_ref, postw_ref, o_ref):
    d = x_ref.shape[-1]
    tm = x_ref.shape[1]
    w_post = mod_ref[0][:, 2 * d:3 * d] * postw_ref[...]
    for r in range(0, tm, OUT_ROWS):
        rows = slice(r, r + OUT_ROWS)
        og = og_ref[0, rows, :].astype(_F32)
        zg = _silu(zg_ref[0, rows, :].astype(_F32))
        y_gdn = jnp.concatenate([_rms(og[:, hd * DV:(hd + 1) * DV], onw_ref[...]) * zg[:, hd * DV:(hd + 1) * DV]
                                 for hd in range(HEADS)], axis=1)
        y = (jnp.dot(om_ref[0, rows, :], wout_ref[0:WIDTH, :], preferred_element_type=_F32)
             + _dot(y_gdn, wout_ref[WIDTH:2 * WIDTH, :]))
        y_n = y * lax.rsqrt(jnp.mean(y * y, axis=-1, keepdims=True) + EPS)
        o_ref[0, rows, :] = x_ref[0, rows, :] + y_n * w_post


def _out_call(layer, x, mod, o_mla, o_gdn, z_gdn, onw, wout, postw):
    b, s, d = x.shape
    tm = min(OUT_TILE, s)
    const = lambda shape: pl.BlockSpec((None,) + shape, lambda i, j: (layer,) + (0,) * len(shape))
    tok = lambda w: pl.BlockSpec((1, tm, w), lambda i, j: (i, j, 0))
    return pl.pallas_call(
        _out_kernel,
        out_shape=jax.ShapeDtypeStruct((b, s, d), _F32),
        grid=(b, s // tm),
        in_specs=[
            tok(d),
            pl.BlockSpec((None, 1, 1, 3 * d), lambda i, j: (layer, i, 0, 0)),
            tok(WIDTH), tok(WIDTH), tok(WIDTH),
            const((1, DV)),
            const((2 * WIDTH, d)),
            const((1, d)),
        ],
        out_specs=tok(d),
        compiler_params=pltpu.CompilerParams(dimension_semantics=("arbitrary", "arbitrary"),
                                             vmem_limit_bytes=VMEM_LIMIT),
        name="out_proj",
    )(x, mod, o_mla, o_gdn, z_gdn, onw, wout, postw)


def _rot_cols(w):
    half = w.shape[-1] // 2
    return jnp.concatenate([-w[..., half:], w[..., :half]], axis=-1)


def _pad_cols(w, n):
    return jnp.pad(w, [(0, 0)] * (w.ndim - 1) + [(0, n - w.shape[-1])])


def _prep_w_in(w):
    w = w.astype(_BF16)
    o_kv = Q_RANK
    o_kpe = o_kv + KV_RANK
    o_zm = o_kpe + ROPE
    o_qkv = o_zm + WIDTH
    o_a = o_qkv + 3 * WIDTH
    o_zg = o_a + 2 * HEADS
    k_pe = w[..., o_kpe:o_zm]
    return jnp.concatenate([
        w[..., :o_kpe], k_pe, _rot_cols(k_pe), w[..., o_zm:o_qkv], w[..., o_qkv:o_a],
        _pad_cols(w[..., o_a:o_zg], LANES), w[..., o_zg:],
    ], axis=-1)


def _prep_q_up(w):
    depth = w.shape[0]
    per = w.astype(_BF16).reshape(depth, Q_RANK, HEADS, DQK)
    nope = per[..., :NOPE].reshape(depth, Q_RANK, HEADS * NOPE)
    pe = [jnp.concatenate([per[:, :, h, NOPE:], _rot_cols(per[:, :, h, NOPE:])], axis=-1) for h in range(HEADS)]
    return jnp.concatenate([nope] + pe, axis=-1)


def _prep_kv_up(w):
    depth = w.shape[0]
    per = w.astype(_BF16).reshape(depth, KV_RANK, HEADS, NOPE + DV)
    return jnp.concatenate([per[..., :NOPE].reshape(depth, KV_RANK, HEADS * NOPE),
                            per[..., NOPE:].reshape(depth, KV_RANK, HEADS * DV)], axis=-1)


def _lane_rows(vals):
    return _pad_cols(vals.astype(_F32), LANES)[:, None, :]


def kernel(x, c, positions, w_mod, b_mod, pre_norm_w, post_norm_w, w_in, mla_q_norm_w, mla_q_up,
           mla_kv_norm_w, mla_kv_up, gdn_conv_w, gdn_a_log, gdn_dt_bias, gdn_o_norm_w, w_out):
    depth = w_mod.shape[0]
    b, s, d = x.shape
    assert s % TOKEN_TILE == 0 and s % GDN_TILE == 0 and s % ATTN_Q == 0 and s % OUT_TILE == 0
    assert TOKEN_TILE % ATTN_SUB == 0 and ATTN_Q % TOKEN_TILE == 0 and (ATTN_Q // ATTN_SUB) % 2 == 0
    mod = _modulation(c, w_mod, b_mod).reshape(depth, b, 1, 3 * d)
    tab = _rope_table(positions)
    in_params = (pre_norm_w[:, None, :], _prep_w_in(w_in), mla_q_norm_w[:, None, :], _prep_q_up(mla_q_up),
                 mla_kv_norm_w[:, None, :], _prep_kv_up(mla_kv_up))
    gdn_params = (gdn_conv_w, _lane_rows(gdn_a_log), _lane_rows(gdn_dt_bias))
    out_params = (gdn_o_norm_w[:, None, :], w_out.astype(_BF16), post_norm_w[:, None, :])
    for l in range(depth):
        q, k, v, z_mla, gq, gk, gv, gb, z_gdn = _in_call(l, x, mod, *in_params, tab, *gdn_params)
        o_mla = _attn_call(q, k, v, z_mla)
        o_gdn = _gdn_call(gq, gk, gv, gb)
        x = _out_call(l, x, mod, o_mla, o_gdn, z_gdn, *out_params)
    return x
```

```python
import functools

import jax
import jax.numpy as jnp
import numpy as np
from jax import lax
from jax.experimental import pallas as pl
from jax.experimental.pallas import tpu as pltpu

HEADS = 4
NOPE = 128
ROPE = 64
DQK = NOPE + ROPE
DV = 128
Q_RANK = 384
KV_RANK = 256
WIDTH = HEADS * DV
CONV_K = 4
ROPE_THETA = 10000.0
EPS = 1e-6

LANES = 128

C_QLAT = 0
C_KVLAT = C_QLAT + Q_RANK
C_KPE = C_KVLAT + KV_RANK
C_ZMLA = C_KPE + LANES
C_QKV = C_ZMLA + WIDTH
C_AB = C_QKV + 3 * WIDTH
C_ZGDN = C_AB + LANES
N_COLS = C_ZGDN + WIDTH

TOKEN_TILE = 512
ATTN_Q = 2048
ATTN_SUB = 256
ATTN_HEADS = 1
VT_ROWS = DV + 16
GDN_CHUNK = 256
GDN_TILE = 512
EPI_ROWS = 128
OUT_TILE = 2048
OUT_ROWS = 256
ROPE_TILE = 1024
VMEM_LIMIT = 56 * 1024 * 1024

_F32 = jnp.float32
_BF16 = jnp.bfloat16
_NEG = -1e30
LOG2E = float(np.log2(np.e))


def _sigmoid(x):
    return 1.0 / (1.0 + jnp.exp2(x * -LOG2E))


def _silu(x):
    h = 0.5 * x
    return h + h * jnp.tanh(h)


def _softplus(x):
    return jnp.maximum(x, 0.0) + jnp.log(1.0 + jnp.exp(-jnp.abs(x)))


def _rms(x, w):
    return x * lax.rsqrt(jnp.mean(x * x, axis=-1, keepdims=True) + EPS) * w


def _dot(a, b):
    return jnp.dot(a.astype(_BF16), b.astype(_BF16), preferred_element_type=_F32)


def _dot_nt(a, b):
    return lax.dot_general(a.astype(_BF16), b.astype(_BF16), (((1,), (1,)), ((), ())),
                           preferred_element_type=_F32)


def _dot_tn(a, b):
    return lax.dot_general(a.astype(_BF16), b.astype(_BF16), (((0,), (0,)), ((), ())),
                           preferred_element_type=_F32)


def _mod_kernel(c_ref, w_ref, b_ref, o_ref):
    c_act = _silu(c_ref[...])
    o_ref[0] = jnp.dot(c_act, w_ref[0], preferred_element_type=_F32,
                       precision=lax.Precision.HIGHEST) + b_ref[0]


def _modulation(c, w_mod, b_mod):
    depth, d, d3 = w_mod.shape
    b = c.shape[0]
    return pl.pallas_call(
        _mod_kernel,
        out_shape=jax.ShapeDtypeStruct((depth, b, d3), _F32),
        grid=(depth, d3 // d),
        in_specs=[
            pl.BlockSpec((b, d), lambda l, j: (0, 0)),
            pl.BlockSpec((1, d, d), lambda l, j: (l, 0, j)),
            pl.BlockSpec((1, 1, d), lambda l, j: (l, 0, j)),
        ],
        out_specs=pl.BlockSpec((1, b, d), lambda l, j: (l, 0, j)),
        compiler_params=pltpu.CompilerParams(dimension_semantics=("arbitrary", "arbitrary")),
        name="modulation",
    )(c, w_mod, b_mod.reshape(depth, 1, d3))


def _rope_kernel(pos_ref, freq_ref, o_ref):
    ang = pos_ref[0].astype(_F32) * freq_ref[...]
    lane = lax.broadcasted_iota(jnp.int32, ang.shape, 1)
    o_ref[0] = jnp.where(lane < ROPE, jnp.cos(ang), jnp.sin(ang))


def _rope_table(positions):
    b, s = positions.shape
    half = ROPE // 2
    inv_freq = jnp.power(ROPE_THETA, -jnp.arange(half, dtype=_F32) * 2.0 / ROPE)
    freq = jnp.tile(inv_freq, LANES // half).reshape(1, LANES)
    t = min(ROPE_TILE, s)
    return pl.pallas_call(
        _rope_kernel,
        out_shape=jax.ShapeDtypeStruct((b, s, LANES), _F32),
        grid=(b, s // t),
        in_specs=[
            pl.BlockSpec((1, t, 1), lambda i, j: (i, j, 0)),
            pl.BlockSpec((1, LANES), lambda i, j: (0, 0)),
        ],
        out_specs=pl.BlockSpec((1, t, LANES), lambda i, j: (i, j, 0)),
        compiler_params=pltpu.CompilerParams(dimension_semantics=("arbitrary", "arbitrary")),
        name="rope_table",
    )(positions.reshape(b, s, 1), freq)


def _in_kernel(x_ref, mod_ref, prew_ref, win_ref, qnw_ref, qup_ref, kvnw_ref, kvup_ref, tab_ref,
               convw_ref, alog_ref, dtb_ref,
               qt_ref, k_ref, vt_ref, zm_ref, gq_ref, gk_ref, gv_ref, gb_ref, zg_ref,
               tail_ref):
    d = x_ref.shape[-1]
    tm = x_ref.shape[1]

    @pl.when(pl.program_id(1) == 0)
    def _():
        tail_ref[tm:tm + 8, :] = jnp.zeros((8, tail_ref.shape[1]), tail_ref.dtype)

    x = x_ref[0]
    mod = mod_ref[0]
    w_eff = prew_ref[...] * (1.0 + mod[:, d:2 * d])
    h = (x * lax.rsqrt(jnp.mean(x * x, axis=-1, keepdims=True) + EPS) * w_eff + mod[:, 0:d]).astype(_BF16)

    def project(lo, hi):
        return jnp.dot(h, win_ref[:, lo:hi], preferred_element_type=_F32)

    proj = project(0, C_ZMLA)

    tab = tab_ref[0]
    q_scale = DQK ** -0.5 * LOG2E

    def rope(block):
        y = block * tab
        return y + pltpu.roll(y, ROPE, 1)

    nq = _rms(proj[:, C_QLAT:C_QLAT + Q_RANK], qnw_ref[...]).astype(_BF16)
    nkv = _rms(proj[:, C_KVLAT:C_KVLAT + KV_RANK], kvnw_ref[...]).astype(_BF16)
    k_pe = rope(proj[:, C_KPE:C_KPE + LANES])[:, :ROPE].astype(k_ref.dtype)
    for hd in range(HEADS):
        hc = slice(hd * 2 * LANES, (hd + 1) * 2 * LANES)
        qh = jnp.dot(nq, qup_ref[:, hc], preferred_element_type=_F32)
        kvh = jnp.dot(nkv, kvup_ref[:, hc], preferred_element_type=_F32)
        q_pe = rope(qh[:, NOPE:])
        qt_ref[0, hd, 0:NOPE, :] = (qh[:, :NOPE] * q_scale).T.astype(qt_ref.dtype)
        qt_ref[0, hd, NOPE:DQK, :] = (q_pe * q_scale).T[:ROPE, :].astype(qt_ref.dtype)
        k_ref[0, hd, :, 0:NOPE] = kvh[:, :NOPE].astype(k_ref.dtype)
        k_ref[0, hd, :, NOPE:DQK] = k_pe
        vt_ref[0, hd, 0, 0:DV, :] = kvh[:, NOPE:].T.astype(vt_ref.dtype)
        vt_ref[0, hd, 0, DV:, :] = jnp.ones((VT_ROWS - DV, tm), vt_ref.dtype)
    zm_ref[0] = project(C_ZMLA, C_QKV).astype(zm_ref.dtype)

    for part, out_ref in enumerate((gq_ref, gk_ref, gv_ref)):
        cols = slice(part * WIDTH, (part + 1) * WIDTH)
        xg = project(C_QKV + part * WIDTH, C_QKV + (part + 1) * WIDTH)
        tail_ref[0:8, cols] = tail_ref[tm:tm + 8, cols]
        tail_ref[8:8 + tm, cols] = xg
        cw = convw_ref[:, cols]
        for r in range(0, tm, EPI_ROWS):
            conv = tail_ref[8 + r:8 + r + EPI_ROWS, cols] * cw[CONV_K - 1:CONV_K, :]
            for j in range(CONV_K - 1):
                start = 8 + r - (CONV_K - 1 - j)
                conv = conv + tail_ref[start:start + EPI_ROWS, cols] * cw[j:j + 1, :]
            act = _silu(conv)
            if part == 2:
                out_ref[0, r:r + EPI_ROWS, :] = act.astype(out_ref.dtype)
            else:
                scale = DV ** -0.5 if part == 0 else 1.0
                for hd in range(HEADS):
                    lo = hd * DV
                    a = act[:, lo:lo + DV]
                    inv = lax.rsqrt(jnp.sum(a * a, axis=-1, keepdims=True) + EPS) * scale
                    out_ref[0, r:r + EPI_ROWS, lo:lo + DV] = (a * inv).astype(out_ref.dtype)

    gates = project(C_AB, N_COLS)
    ab = gates[:, 0:LANES]
    g = -jnp.exp(alog_ref[...]) * _softplus(ab + dtb_ref[...])
    lane = lax.broadcasted_iota(jnp.int32, ab.shape, 1)
    gb_ref[0] = jnp.where(lane < HEADS, g, _sigmoid(ab))
    zg_ref[0] = gates[:, C_ZGDN - C_AB:].astype(zg_ref.dtype)


def _in_call(layer, x, mod, prew, win, qnw, qup, kvnw, kvup, tab, convw, alog, dtb):
    b, s, d = x.shape
    tm = min(TOKEN_TILE, s)
    const = lambda shape: pl.BlockSpec((None,) + shape, lambda i, j: (layer,) + (0,) * len(shape))
    tok = lambda w: pl.BlockSpec((1, tm, w), lambda i, j: (i, j, 0))
    head = pl.BlockSpec((1, HEADS, tm, DQK), lambda i, j: (i, 0, j, 0))
    head_t = pl.BlockSpec((1, HEADS, DQK, tm), lambda i, j: (i, 0, 0, j))
    head_vt = pl.BlockSpec((1, HEADS, 1, VT_ROWS, tm), lambda i, j: (i, 0, j, 0, 0))
    out_shape = (
        jax.ShapeDtypeStruct((b, HEADS, DQK, s), _BF16),
        jax.ShapeDtypeStruct((b, HEADS, s, DQK), _BF16),
        jax.ShapeDtypeStruct((b, HEADS, s // tm, VT_ROWS, tm), _BF16),
        jax.ShapeDtypeStruct((b, s, WIDTH), _BF16),
        jax.ShapeDtypeStruct((b, s, WIDTH), _BF16),
        jax.ShapeDtypeStruct((b, s, WIDTH), _BF16),
        jax.ShapeDtypeStruct((b, s, WIDTH), _BF16),
        jax.ShapeDtypeStruct((b, s, LANES), _F32),
        jax.ShapeDtypeStruct((b, s, WIDTH), _BF16),
    )
    return pl.pallas_call(
        _in_kernel,
        out_shape=out_shape,
        grid=(b, s // tm),
        in_specs=[
            tok(d),
            pl.BlockSpec((None, 1, 1, 3 * d), lambda i, j: (layer, i, 0, 0)),
            const((1, d)),
            const((d, N_COLS)),
            const((1, Q_RANK)),
            const((Q_RANK, 2 * WIDTH)),
            const((1, KV_RANK)),
            const((KV_RANK, 2 * WIDTH)),
            tok(LANES),
            const((CONV_K, 3 * WIDTH)),
            const((1, LANES)),
            const((1, LANES)),
        ],
        out_specs=(head_t, head, head_vt, tok(WIDTH), tok(WIDTH), tok(WIDTH), tok(WIDTH),
                   tok(LANES), tok(WIDTH)),
        scratch_shapes=[pltpu.VMEM((tm + 8, 3 * WIDTH), _F32)],
        compiler_params=pltpu.CompilerParams(dimension_semantics=("arbitrary", "arbitrary"),
                                             vmem_limit_bytes=VMEM_LIMIT),
        name="in_proj",
    )(x, mod, prew, win, qnw, qup, kvnw, kvup, tab, convw, alog, dtb)


def _attn_kernel(qt_ref, k_ref, vt_ref, z_ref, o_ref, sa_ref, sb_ref, mxa_ref, mxb_ref, m_ref, acc_ref):
    tq = qt_ref.shape[3]
    uk = vt_ref.shape[4]
    uq = ATTN_SUB
    nq = tq // uq
    nu = tq // uk
    i = pl.program_id(2)
    heads = range(qt_ref.shape[1])
    qts = [[qt_ref[0, g, :, h * uq:(h + 1) * uq] for h in range(nq)] for g in heads]
    bufs = ((sa_ref, mxa_ref), (sb_ref, mxb_ref))

    m_ref[...] = jnp.full_like(m_ref, _NEG)
    acc_ref[...] = jnp.zeros_like(acc_ref)

    def produce(buf, blk, unit, subs, diagonal=False):
        s_ref, mx_ref = bufs[buf]
        for g in heads:
            kj = k_ref[0, g, pl.ds(pl.multiple_of(blk * tq + unit * uk, uk), uk), :]
            for h in subs:
                keys = min(uk, (h - unit * (uk // uq) + 1) * uq) if diagonal else uk
                s = jnp.dot(kj[0:keys], qts[g][h], preferred_element_type=_F32)
                s_ref[g * nq + h, 0:keys, :] = s
                mx_ref[g * nq + h] = jnp.max(s, axis=0, keepdims=True)

    def consume(buf, blk, unit, subs, masked=()):
        s_ref, mx_ref = bufs[buf]
        for g in heads:
            vt = vt_ref[0, g, blk * nu + unit]
            for h in subs:
                f = g * nq + h
                if h in masked:
                    own = h - unit * (uk // uq)
                    tri = (lax.broadcasted_iota(jnp.int32, (uq, uq), 0)
                           <= lax.broadcasted_iota(jnp.int32, (uq, uq), 1))
                    s = jnp.where(tri, s_ref[f, own * uq:(own + 1) * uq, :], _NEG)
                    if own > 0:
                        s = jnp.concatenate([s_ref[f, 0:own * uq, :], s], axis=0)
                    mx = jnp.max(s, axis=0, keepdims=True)
                    vth = vt_ref[0, g, blk * nu + unit, :, 0:(own + 1) * uq]
                else:
                    s = s_ref[f]
                    mx = mx_ref[f]
                    vth = vt
                m = m_ref[f]
                m_new = jnp.maximum(m, mx)
                p = jnp.exp2(s - m_new).astype(_BF16)
                acc_ref[f] = jnp.exp2(m - m_new) * acc_ref[f] + jnp.dot(vth, p, preferred_element_type=_F32)
                m_ref[f] = m_new

    every = tuple(range(nq))
    produce(0, 0, 0, every)

    def block(j):
        for unit in range(nu):
            if unit + 1 < nu:
                produce((unit + 1) % 2, j, unit + 1, every)
            else:
                produce((unit + 1) % 2, j + 1, 0, every)
            consume(unit % 2, j, unit, every)

    def pair(p, carry):
        block(2 * p)
        block(2 * p + 1)
        return carry

    lax.fori_loop(0, i // 2, pair, 0)

    @pl.when(i % 2 == 1)
    def _():
        block(i - 1)
    per = uk // uq
    for unit in range(nu):
        if unit + 1 < nu:
            produce((unit + 1) % 2, i, unit + 1, tuple(range((unit + 1) * per, nq)), diagonal=True)
        consume(unit % 2, i, unit, tuple(range(unit * per, nq)),
                masked=tuple(range(unit * per, (unit + 1) * per)))
    for g in heads:
        for h in range(nq):
            acc = acc_ref[g * nq + h]
            rows, cols = slice(h * uq, (h + 1) * uq), slice(g * DV, (g + 1) * DV)
            gate = _silu(z_ref[0, rows, cols].astype(_F32))
            o_ref[0, rows, cols] = ((acc[:DV] / acc[DV:DV + 1]).T * gate).astype(o_ref.dtype)


def _attn_call(qt, k, vt, z):
    b, _, _, s = qt.shape
    vrows, tv = vt.shape[-2:]
    t = ATTN_Q
    g = ATTN_HEADS
    nq = g * (t // ATTN_SUB)
    return pl.pallas_call(
        _attn_kernel,
        out_shape=jax.ShapeDtypeStruct((b, s, WIDTH), _BF16),
        grid=(b, HEADS // g, s // t),
        in_specs=[
            pl.BlockSpec((1, g, DQK, t), lambda bi, h, i: (bi, h, 0, i)),
            pl.BlockSpec((1, g, s, DQK), lambda bi, h, i: (bi, h, 0, 0)),
            pl.BlockSpec((1, g, s // tv, vrows, tv), lambda bi, h, i: (bi, h, 0, 0, 0)),
            pl.BlockSpec((1, t, g * DV), lambda bi, h, i: (bi, i, h)),
        ],
        out_specs=pl.BlockSpec((1, t, g * DV), lambda bi, h, i: (bi, i, h)),
        scratch_shapes=[
            pltpu.VMEM((nq, tv, ATTN_SUB), _F32),
            pltpu.VMEM((nq, tv, ATTN_SUB), _F32),
            pltpu.VMEM((nq, 1, ATTN_SUB), _F32),
            pltpu.VMEM((nq, 1, ATTN_SUB), _F32),
            pltpu.VMEM((nq, 1, ATTN_SUB), _F32),
            pltpu.VMEM((nq, vrows, ATTN_SUB), _F32),
        ],
        compiler_params=pltpu.CompilerParams(
            dimension_semantics=("arbitrary", "arbitrary", "arbitrary"), vmem_limit_bytes=VMEM_LIMIT),
        name="mla_attention",
    )(qt, k, vt, z)


def _cumsum_rows(tril_bf16, x):
    lane = lax.broadcasted_iota(jnp.int32, x.shape, 1)
    hi = x.astype(_BF16).astype(_F32)
    r1 = x - hi
    mid = r1.astype(_BF16).astype(_F32)
    low = r1 - mid
    packed = jnp.where(lane < 8, hi, jnp.where(lane < 16, pltpu.roll(mid, 8, 1), pltpu.roll(low, 16, 1)))
    res = jnp.dot(tril_bf16, packed.astype(_BF16), preferred_element_type=_F32)
    return res + pltpu.roll(res, LANES - 8, 1) + pltpu.roll(res, LANES - 16, 1)


def _gdn_kernel(q_ref, k_ref, v_ref, gb_ref, o_ref, state_ref, gc_ref, n_ref, t_ref, qk_ref, rhs_ref,
                qe_ref, kd_ref, sol_ref, ou_ref, qw_ref, su_ref, sw_ref):
    c = GDN_CHUNK
    n_chunks = q_ref.shape[1] // c
    n_chains = n_chunks * HEADS

    @pl.when(pl.program_id(1) == 0)
    def _():
        state_ref[...] = jnp.zeros_like(state_ref)

    row = lax.broadcasted_iota(jnp.int32, (c, c), 0)
    col = lax.broadcasted_iota(jnp.int32, (c, c), 1)
    incl = col <= row
    xor = row ^ col
    eye = (xor == 0).astype(_F32)
    tril = incl.astype(_BF16)

    for ci in range(n_chunks):
        rows = slice(ci * c, (ci + 1) * c)
        gb = gb_ref[0, rows, :]
        gcum = _cumsum_rows(tril, gb)
        gc_ref[ci] = gcum
        gcum_t = gcum.T
        for hd in range(HEADS):
            ch = ci * HEADS + hd
            lo = hd * DV
            q = q_ref[0, rows, lo:lo + DV].astype(_F32)
            k = k_ref[0, rows, lo:lo + DV].astype(_F32)
            v = v_ref[0, rows, lo:lo + DV].astype(_F32)
            beta = jnp.broadcast_to(gb[:, HEADS + hd:HEADS + hd + 1], (c, DV))
            g_col = jnp.broadcast_to(gcum[:, hd:hd + 1], (c, DV))
            g_row = gcum_t[hd:hd + 1, :]
            g_last = gcum[c - 1:c, hd:hd + 1]
            decay = jnp.exp(jnp.where(incl, jnp.concatenate([g_col, g_col], axis=1) - g_row, _NEG))
            e_col = jnp.exp(g_col)
            kb = k * beta
            n_neg = _dot_nt(-kb, k) * decay
            n_ref[ch] = n_neg
            t_ref[ch] = jnp.where(xor == 1, n_neg, eye)
            qk_ref[ch] = (_dot_nt(q, k) * decay).astype(qk_ref.dtype)
            rhs_ref[ch] = jnp.concatenate([v * beta, kb * e_col], axis=1).astype(rhs_ref.dtype)
            qe_ref[ch] = (q * e_col).astype(qe_ref.dtype)
            kd_ref[ch] = (k * jnp.exp(g_last - g_col)).astype(kd_ref.dtype)

    m = 2
    while m < c:
        blk = lax.broadcasted_iota(jnp.int32, (c, c), 0) ^ lax.broadcasted_iota(jnp.int32, (c, c), 1)
        mask = (blk >= m) & (blk < 2 * m)
        size = m if m % 8 == 0 else c
        odd = [slice(b * m, (b + 1) * m) for b in range(1, c // m, 2)] if m % 8 == 0 else [slice(0, c)]

        def rows_of(ch, odd=odd):
            return jnp.concatenate([t_ref[ch, r, :] for r in odd], axis=0)

        left = []
        for ch in range(n_chains):
            left.append(_dot(rows_of(ch), jnp.where(mask, n_ref[ch], 0.0)))
        for ch in range(n_chains):
            upd = rows_of(ch) + _dot(left[ch], t_ref[ch])
            for j, r in enumerate(odd):
                t_ref[ch, r, :] = upd[j * size:(j + 1) * size]
        m *= 2

    for ch in range(n_chains):
        sol_ref[ch] = _dot(t_ref[ch], rhs_ref[ch]).astype(sol_ref.dtype)
    for ch in range(n_chains):
        qk_sol = _dot(qk_ref[ch], sol_ref[ch])
        ou_ref[ch] = qk_sol[:, :DV]
        qw_ref[ch] = (qe_ref[ch].astype(_F32) - qk_sol[:, DV:]).astype(qw_ref.dtype)
    for ch in range(n_chains):
        kd_sol = _dot_tn(kd_ref[ch], sol_ref[ch])
        su_ref[ch] = kd_sol[:, :DV]
        sw_ref[ch] = kd_sol[:, DV:].astype(sw_ref.dtype)
    for ci in range(n_chunks):
        rows = slice(ci * c, (ci + 1) * c)
        for hd in range(HEADS):
            ch = ci * HEADS + hd
            lo = hd * DV
            g_last = gc_ref[ci, c - 1:c, hd:hd + 1]
            state = state_ref[hd]
            o_ref[0, rows, lo:lo + DV] = (_dot(qw_ref[ch], state) + ou_ref[ch]).astype(o_ref.dtype)
            state_ref[hd] = state * jnp.exp(g_last) + su_ref[ch] - _dot(sw_ref[ch], state)


def _gdn_call(gq, gk, gv, gb):
    b, s, _ = gq.shape
    t = GDN_TILE
    n_chains = (t // GDN_CHUNK) * HEADS
    tok = lambda w: pl.BlockSpec((1, t, w), lambda i, j: (i, j, 0))
    return pl.pallas_call(
        _gdn_kernel,
        out_shape=jax.ShapeDtypeStruct((b, s, WIDTH), _BF16),
        grid=(b, s // t),
        in_specs=[tok(WIDTH), tok(WIDTH), tok(WIDTH), tok(LANES)],
        out_specs=tok(WIDTH),
        scratch_shapes=[
            pltpu.VMEM((HEADS, DV, DV), _F32),
            pltpu.VMEM((t // GDN_CHUNK, GDN_CHUNK, LANES), _F32),
            pltpu.VMEM((n_chains, GDN_CHUNK, GDN_CHUNK), _F32),
            pltpu.VMEM((n_chains, GDN_CHUNK, GDN_CHUNK), _F32),
            pltpu.VMEM((n_chains, GDN_CHUNK, GDN_CHUNK), _BF16),
            pltpu.VMEM((n_chains, GDN_CHUNK, 2 * DV), _BF16),
            pltpu.VMEM((n_chains, GDN_CHUNK, DV), _BF16),
            pltpu.VMEM((n_chains, GDN_CHUNK, DV), _BF16),
            pltpu.VMEM((n_chains, GDN_CHUNK, 2 * DV), _BF16),
            pltpu.VMEM((n_chains, GDN_CHUNK, DV), _F32),
            pltpu.VMEM((n_chains, GDN_CHUNK, DV), _BF16),
            pltpu.VMEM((n_chains, DV, DV), _F32),
            pltpu.VMEM((n_chains, DV, DV), _BF16),
        ],
        compiler_params=pltpu.CompilerParams(dimension_semantics=("arbitrary", "arbitrary"),
                                             vmem_limit_bytes=VMEM_LIMIT),
        name="gated_delta_rule",
    )(gq, gk, gv, gb)


def _out_kernel(x_ref, mod_ref, om_ref, og_ref, zg_ref, onw_ref, wout_ref, postw_ref, o_ref):
    d = x_ref.shape[-1]
    tm = x_ref.shape[1]
    w_post = mod_ref[0][:, 2 * d:3 * d] * postw_ref[...]
    for r in range(0, tm, OUT_ROWS):
        rows = slice(r, r + OUT_ROWS)
        og = og_ref[0, rows, :].astype(_F32)
        zg = _silu(zg_ref[0, rows, :].astype(_F32))
        y_gdn = jnp.concatenate([_rms(og[:, hd * DV:(hd + 1) * DV], onw_ref[...]) * zg[:, hd * DV:(hd + 1) * DV]
                                 for hd in range(HEADS)], axis=1)
        y = (jnp.dot(om_ref[0, rows, :], wout_ref[0:WIDTH, :], preferred_element_type=_F32)
             + _dot(y_gdn, wout_ref[WIDTH:2 * WIDTH, :]))
        y_n = y * lax.rsqrt(jnp.mean(y * y, axis=-1, keepdims=True) + EPS)
        o_ref[0, rows, :] = x_ref[0, rows, :] + y_n * w_post


def _out_call(layer, x, mod, o_mla, o_gdn, z_gdn, onw, wout, postw):
    b, s, d = x.shape
    tm = min(OUT_TILE, s)
    const = lambda shape: pl.BlockSpec((None,) + shape, lambda i, j: (layer,) + (0,) * len(shape))
    tok = lambda w: pl.BlockSpec((1, tm, w), lambda i, j: (i, j, 0))
    return pl.pallas_call(
        _out_kernel,
        out_shape=jax.ShapeDtypeStruct((b, s, d), _F32),
        grid=(b, s // tm),
        in_specs=[
            tok(d),
            pl.BlockSpec((None, 1, 1, 3 * d), lambda i, j: (layer, i, 0, 0)),
            tok(WIDTH), tok(WIDTH), tok(WIDTH),
            const((1, DV)),
            const((2 * WIDTH, d)),
            const((1, d)),
        ],
        out_specs=tok(d),
        compiler_params=pltpu.CompilerParams(dimension_semantics=("arbitrary", "arbitrary"),
                                             vmem_limit_bytes=VMEM_LIMIT),
        name="out_proj",
    )(x, mod, o_mla, o_gdn, z_gdn, onw, wout, postw)


def _rot_cols(w):
    half = w.shape[-1] // 2
    return jnp.concatenate([-w[..., half:], w[..., :half]], axis=-1)


def _pad_cols(w, n):
    return jnp.pad(w, [(0, 0)] * (w.ndim - 1) + [(0, n - w.shape[-1])])


def _prep_w_in(w):
    w = w.astype(_BF16)
    o_kv = Q_RANK
    o_kpe = o_kv + KV_RANK
    o_zm = o_kpe + ROPE
    o_qkv = o_zm + WIDTH
    o_a = o_qkv + 3 * WIDTH
    o_zg = o_a + 2 * HEADS
    k_pe = w[..., o_kpe:o_zm]
    return jnp.concatenate([
        w[..., :o_kpe], k_pe, _rot_cols(k_pe), w[..., o_zm:o_qkv], w[..., o_qkv:o_a],
        _pad_cols(w[..., o_a:o_zg], LANES), w[..., o_zg:],
    ], axis=-1)


def _prep_q_up(w):
    depth = w.shape[0]
    per = w.astype(_BF16).reshape(depth, Q_RANK, HEADS, DQK)
    heads = [jnp.concatenate([per[:, :, h, :NOPE], per[:, :, h, NOPE:], _rot_cols(per[:, :, h, NOPE:])], axis=-1)
             for h in range(HEADS)]
    return jnp.concatenate(heads, axis=-1)


def _prep_kv_up(w):
    return w.astype(_BF16)


def _lane_rows(vals):
    return _pad_cols(vals.astype(_F32), LANES)[:, None, :]


def kernel(x, c, positions, w_mod, b_mod, pre_norm_w, post_norm_w, w_in, mla_q_norm_w, mla_q_up,
           mla_kv_norm_w, mla_kv_up, gdn_conv_w, gdn_a_log, gdn_dt_bias, gdn_o_norm_w, w_out):
    depth = w_mod.shape[0]
    b, s, d = x.shape
    assert s % TOKEN_TILE == 0 and s % GDN_TILE == 0 and s % ATTN_Q == 0 and s % OUT_TILE == 0
    assert TOKEN_TILE % ATTN_SUB == 0 and ATTN_Q % TOKEN_TILE == 0 and (ATTN_Q // ATTN_SUB) % 2 == 0
    mod = _modulation(c, w_mod, b_mod).reshape(depth, b, 1, 3 * d)
    tab = _rope_table(positions)
    in_params = (pre_norm_w[:, None, :], _prep_w_in(w_in), mla_q_norm_w[:, None, :], _prep_q_up(mla_q_up),
                 mla_kv_norm_w[:, None, :], _prep_kv_up(mla_kv_up))
    gdn_params = (gdn_conv_w, _lane_rows(gdn_a_log), _lane_rows(gdn_dt_bias))
    out_params = (gdn_o_norm_w[:, None, :], w_out.astype(_BF16), post_norm_w[:, None, :])
    for l in range(depth):
        q, k, v, z_mla, gq, gk, gv, gb, z_gdn = _in_call(l, x, mod, *in_params, tab, *gdn_params)
        o_mla = _attn_call(q, k, v, z_mla)
        o_gdn = _gdn_call(gq, gk, gv, gb)
        x = _out_call(l, x, mod, o_mla, o_gdn, z_gdn, *out_params)
    return x
```

```python
import functools

import jax
import jax.numpy as jnp
import numpy as np
from jax import lax
from jax.experimental import pallas as pl
from jax.experimental.pallas import tpu as pltpu

HEADS = 4
NOPE = 128
ROPE = 64
DQK = NOPE + ROPE
DV = 128
Q_RANK = 384
KV_RANK = 256
WIDTH = HEADS * DV
CONV_K = 4
ROPE_THETA = 10000.0
EPS = 1e-6

LANES = 128

C_QLAT = 0
C_KVLAT = C_QLAT + Q_RANK
C_KPE = C_KVLAT + KV_RANK
C_ZMLA = C_KPE + LANES
C_QKV = C_ZMLA + WIDTH
C_AB = C_QKV + 3 * WIDTH
C_ZGDN = C_AB + LANES
N_COLS = C_ZGDN + WIDTH

TOKEN_TILE = 512
ATTN_Q = 2048
ATTN_SUB = 256
ATTN_HEADS = 1
VT_ROWS = DV + 16
GDN_CHUNK = 128
GDN_TILE = 512
EPI_ROWS = 128
OUT_TILE = 2048
OUT_ROWS = 256
ROPE_TILE = 1024
VMEM_LIMIT = 56 * 1024 * 1024

_F32 = jnp.float32
_BF16 = jnp.bfloat16
_NEG = -1e30
LOG2E = float(np.log2(np.e))


def _sigmoid(x):
    return 1.0 / (1.0 + jnp.exp2(x * -LOG2E))


def _silu(x):
    h = 0.5 * x
    return h + h * jnp.tanh(h)


def _softplus(x):
    return jnp.maximum(x, 0.0) + jnp.log(1.0 + jnp.exp(-jnp.abs(x)))


def _rms(x, w):
    return x * lax.rsqrt(jnp.mean(x * x, axis=-1, keepdims=True) + EPS) * w


def _dot(a, b):
    return jnp.dot(a.astype(_BF16), b.astype(_BF16), preferred_element_type=_F32)


def _dot_nt(a, b):
    return lax.dot_general(a.astype(_BF16), b.astype(_BF16), (((1,), (1,)), ((), ())),
                           preferred_element_type=_F32)


def _dot_tn(a, b):
    return lax.dot_general(a.astype(_BF16), b.astype(_BF16), (((0,), (0,)), ((), ())),
                           preferred_element_type=_F32)


def _mod_kernel(c_ref, w_ref, b_ref, o_ref):
    c_act = _silu(c_ref[...])
    o_ref[0] = jnp.dot(c_act, w_ref[0], preferred_element_type=_F32,
                       precision=lax.Precision.HIGHEST) + b_ref[0]


def _modulation(c, w_mod, b_mod):
    depth, d, d3 = w_mod.shape
    b = c.shape[0]
    return pl.pallas_call(
        _mod_kernel,
        out_shape=jax.ShapeDtypeStruct((depth, b, d3), _F32),
        grid=(depth, d3 // d),
        in_specs=[
            pl.BlockSpec((b, d), lambda l, j: (0, 0)),
            pl.BlockSpec((1, d, d), lambda l, j: (l, 0, j)),
            pl.BlockSpec((1, 1, d), lambda l, j: (l, 0, j)),
        ],
        out_specs=pl.BlockSpec((1, b, d), lambda l, j: (l, 0, j)),
        compiler_params=pltpu.CompilerParams(dimension_semantics=("arbitrary", "arbitrary")),
        name="modulation",
    )(c, w_mod, b_mod.reshape(depth, 1, d3))


def _rope_kernel(pos_ref, freq_ref, o_ref):
    ang = pos_ref[0].astype(_F32) * freq_ref[...]
    lane = lax.broadcasted_iota(jnp.int32, ang.shape, 1)
    o_ref[0] = jnp.where(lane < ROPE, jnp.cos(ang), jnp.sin(ang))


def _rope_table(positions):
    b, s = positions.shape
    half = ROPE // 2
    inv_freq = jnp.power(ROPE_THETA, -jnp.arange(half, dtype=_F32) * 2.0 / ROPE)
    freq = jnp.tile(inv_freq, LANES // half).reshape(1, LANES)
    t = min(ROPE_TILE, s)
    return pl.pallas_call(
        _rope_kernel,
        out_shape=jax.ShapeDtypeStruct((b, s, LANES), _F32),
        grid=(b, s // t),
        in_specs=[
            pl.BlockSpec((1, t, 1), lambda i, j: (i, j, 0)),
            pl.BlockSpec((1, LANES), lambda i, j: (0, 0)),
        ],
        out_specs=pl.BlockSpec((1, t, LANES), lambda i, j: (i, j, 0)),
        compiler_params=pltpu.CompilerParams(dimension_semantics=("arbitrary", "arbitrary")),
        name="rope_table",
    )(positions.reshape(b, s, 1), freq)


def _in_kernel(x_ref, mod_ref, prew_ref, win_ref, qnw_ref, qup_ref, kvnw_ref, kvup_ref, tab_ref,
               convw_ref, alog_ref, dtb_ref,
               qt_ref, k_ref, vt_ref, zm_ref, gq_ref, gk_ref, gv_ref, gb_ref, zg_ref,
               tail_ref):
    d = x_ref.shape[-1]
    tm = x_ref.shape[1]

    @pl.when(pl.program_id(1) == 0)
    def _():
        tail_ref[tm:tm + 8, :] = jnp.zeros((8, tail_ref.shape[1]), tail_ref.dtype)

    x = x_ref[0]
    mod = mod_ref[0]
    w_eff = prew_ref[...] * (1.0 + mod[:, d:2 * d])
    h = (x * lax.rsqrt(jnp.mean(x * x, axis=-1, keepdims=True) + EPS) * w_eff + mod[:, 0:d]).astype(_BF16)

    def project(lo, hi):
        return jnp.dot(h, win_ref[:, lo:hi], preferred_element_type=_F32)

    proj = project(0, C_ZMLA)

    tab = tab_ref[0]
    q_scale = DQK ** -0.5 * LOG2E

    def rope(block):
        y = block * tab
        return y + pltpu.roll(y, ROPE, 1)

    nq = _rms(proj[:, C_QLAT:C_QLAT + Q_RANK], qnw_ref[...]).astype(_BF16)
    nkv = _rms(proj[:, C_KVLAT:C_KVLAT + KV_RANK], kvnw_ref[...]).astype(_BF16)
    k_pe = rope(proj[:, C_KPE:C_KPE + LANES])[:, :ROPE].astype(k_ref.dtype)
    for hd in range(HEADS):
        hc = slice(hd * 2 * LANES, (hd + 1) * 2 * LANES)
        qh = jnp.dot(nq, qup_ref[:, hc], preferred_element_type=_F32)
        kvh = jnp.dot(nkv, kvup_ref[:, hc], preferred_element_type=_F32)
        q_pe = rope(qh[:, NOPE:])
        qt_ref[0, hd, 0:NOPE, :] = (qh[:, :NOPE] * q_scale).T.astype(qt_ref.dtype)
        qt_ref[0, hd, NOPE:DQK, :] = (q_pe * q_scale).T[:ROPE, :].astype(qt_ref.dtype)
        k_ref[0, hd, :, 0:NOPE] = kvh[:, :NOPE].astype(k_ref.dtype)
        k_ref[0, hd, :, NOPE:DQK] = k_pe
        vt_ref[0, hd, 0, 0:DV, :] = kvh[:, NOPE:].T.astype(vt_ref.dtype)
        vt_ref[0, hd, 0, DV:, :] = jnp.ones((VT_ROWS - DV, tm), vt_ref.dtype)
    zm_ref[0] = project(C_ZMLA, C_QKV).astype(zm_ref.dtype)

    for part, out_ref in enumerate((gq_ref, gk_ref, gv_ref)):
        cols = slice(part * WIDTH, (part + 1) * WIDTH)
        xg = project(C_QKV + part * WIDTH, C_QKV + (part + 1) * WIDTH)
        tail_ref[0:8, cols] = tail_ref[tm:tm + 8, cols]
        tail_ref[8:8 + tm, cols] = xg
        cw = convw_ref[:, cols]
        for r in range(0, tm, EPI_ROWS):
            conv = tail_ref[8 + r:8 + r + EPI_ROWS, cols] * cw[CONV_K - 1:CONV_K, :]
            for j in range(CONV_K - 1):
                start = 8 + r - (CONV_K - 1 - j)
                conv = conv + tail_ref[start:start + EPI_ROWS, cols] * cw[j:j + 1, :]
            act = _silu(conv)
            if part == 2:
                out_ref[0, r:r + EPI_ROWS, :] = act.astype(out_ref.dtype)
            else:
                scale = DV ** -0.5 if part == 0 else 1.0
                for hd in range(HEADS):
                    lo = hd * DV
                    a = act[:, lo:lo + DV]
                    inv = lax.rsqrt(jnp.sum(a * a, axis=-1, keepdims=True) + EPS) * scale
                    out_ref[0, r:r + EPI_ROWS, lo:lo + DV] = (a * inv).astype(out_ref.dtype)

    gates = project(C_AB, N_COLS)
    ab = gates[:, 0:LANES]
    g = -jnp.exp(alog_ref[...]) * _softplus(ab + dtb_ref[...])
    lane = lax.broadcasted_iota(jnp.int32, ab.shape, 1)
    gb_ref[0] = jnp.where(lane < HEADS, g, _sigmoid(ab))
    zg_ref[0] = gates[:, C_ZGDN - C_AB:].astype(zg_ref.dtype)


def _in_call(layer, x, mod, prew, win, qnw, qup, kvnw, kvup, tab, convw, alog, dtb):
    b, s, d = x.shape
    tm = min(TOKEN_TILE, s)
    const = lambda shape: pl.BlockSpec((None,) + shape, lambda i, j: (layer,) + (0,) * len(shape))
    tok = lambda w: pl.BlockSpec((1, tm, w), lambda i, j: (i, j, 0))
    head = pl.BlockSpec((1, HEADS, tm, DQK), lambda i, j: (i, 0, j, 0))
    head_t = pl.BlockSpec((1, HEADS, DQK, tm), lambda i, j: (i, 0, 0, j))
    head_vt = pl.BlockSpec((1, HEADS, 1, VT_ROWS, tm), lambda i, j: (i, 0, j, 0, 0))
    out_shape = (
        jax.ShapeDtypeStruct((b, HEADS, DQK, s), _BF16),
        jax.ShapeDtypeStruct((b, HEADS, s, DQK), _BF16),
        jax.ShapeDtypeStruct((b, HEADS, s // tm, VT_ROWS, tm), _BF16),
        jax.ShapeDtypeStruct((b, s, WIDTH), _BF16),
        jax.ShapeDtypeStruct((b, s, WIDTH), _BF16),
        jax.ShapeDtypeStruct((b, s, WIDTH), _BF16),
        jax.ShapeDtypeStruct((b, s, WIDTH), _BF16),
        jax.ShapeDtypeStruct((b, s, LANES), _F32),
        jax.ShapeDtypeStruct((b, s, WIDTH), _BF16),
    )
    return pl.pallas_call(
        _in_kernel,
        out_shape=out_shape,
        grid=(b, s // tm),
        in_specs=[
            tok(d),
            pl.BlockSpec((None, 1, 1, 3 * d), lambda i, j: (layer, i, 0, 0)),
            const((1, d)),
            const((d, N_COLS)),
            const((1, Q_RANK)),
            const((Q_RANK, 2 * WIDTH)),
            const((1, KV_RANK)),
            const((KV_RANK, 2 * WIDTH)),
            tok(LANES),
            const((CONV_K, 3 * WIDTH)),
            const((1, LANES)),
            const((1, LANES)),
        ],
        out_specs=(head_t, head, head_vt, tok(WIDTH), tok(WIDTH), tok(WIDTH), tok(WIDTH),
                   tok(LANES), tok(WIDTH)),
        scratch_shapes=[pltpu.VMEM((tm + 8, 3 * WIDTH), _F32)],
        compiler_params=pltpu.CompilerParams(dimension_semantics=("arbitrary", "arbitrary"),
                                             vmem_limit_bytes=VMEM_LIMIT),
        name="in_proj",
    )(x, mod, prew, win, qnw, qup, kvnw, kvup, tab, convw, alog, dtb)


def _attn_kernel(qt_ref, k_ref, vt_ref, z_ref, o_ref, sa_ref, sb_ref, mxa_ref, mxb_ref, m_ref, acc_ref):
    tq = qt_ref.shape[3]
    uk = vt_ref.shape[4]
    uq = ATTN_SUB
    nq = tq // uq
    nu = tq // uk
    i = pl.program_id(2)
    heads = range(qt_ref.shape[1])
    qts = [[qt_ref[0, g, :, h * uq:(h + 1) * uq] for h in range(nq)] for g in heads]
    bufs = ((sa_ref, mxa_ref), (sb_ref, mxb_ref))

    m_ref[...] = jnp.full_like(m_ref, _NEG)
    acc_ref[...] = jnp.zeros_like(acc_ref)

    def produce(buf, blk, unit, subs, diagonal=False):
        s_ref, mx_ref = bufs[buf]
        for g in heads:
            kj = k_ref[0, g, pl.ds(pl.multiple_of(blk * tq + unit * uk, uk), uk), :]
            for h in subs:
                keys = min(uk, (h - unit * (uk // uq) + 1) * uq) if diagonal else uk
                s = jnp.dot(kj[0:keys], qts[g][h], preferred_element_type=_F32)
                s_ref[g * nq + h, 0:keys, :] = s
                mx_ref[g * nq + h] = jnp.max(s, axis=0, keepdims=True)

    def consume(buf, blk, unit, subs, masked=()):
        s_ref, mx_ref = bufs[buf]
        for g in heads:
            vt = vt_ref[0, g, blk * nu + unit]
            for h in subs:
                f = g * nq + h
                if h in masked:
                    own = h - unit * (uk // uq)
                    tri = (lax.broadcasted_iota(jnp.int32, (uq, uq), 0)
                           <= lax.broadcasted_iota(jnp.int32, (uq, uq), 1))
                    s = jnp.where(tri, s_ref[f, own * uq:(own + 1) * uq, :], _NEG)
                    if own > 0:
                        s = jnp.concatenate([s_ref[f, 0:own * uq, :], s], axis=0)
                    mx = jnp.max(s, axis=0, keepdims=True)
                    vth = vt_ref[0, g, blk * nu + unit, :, 0:(own + 1) * uq]
                else:
                    s = s_ref[f]
                    mx = mx_ref[f]
                    vth = vt
                m = m_ref[f]
                m_new = jnp.maximum(m, mx)
                p = jnp.exp2(s - m_new).astype(_BF16)
                acc_ref[f] = jnp.exp2(m - m_new) * acc_ref[f] + jnp.dot(vth, p, preferred_element_type=_F32)
                m_ref[f] = m_new

    every = tuple(range(nq))
    produce(0, 0, 0, every)

    def block(j):
        for unit in range(nu):
            if unit + 1 < nu:
                produce((unit + 1) % 2, j, unit + 1, every)
            else:
                produce((unit + 1) % 2, j + 1, 0, every)
            consume(unit % 2, j, unit, every)

    def pair(p, carry):
        block(2 * p)
        block(2 * p + 1)
        return carry

    lax.fori_loop(0, i // 2, pair, 0)

    @pl.when(i % 2 == 1)
    def _():
        block(i - 1)
    per = uk // uq
    for unit in range(nu):
        if unit + 1 < nu:
            produce((unit + 1) % 2, i, unit + 1, tuple(range((unit + 1) * per, nq)), diagonal=True)
        consume(unit % 2, i, unit, tuple(range(unit * per, nq)),
                masked=tuple(range(unit * per, (unit + 1) * per)))
    for g in heads:
        for h in range(nq):
            acc = acc_ref[g * nq + h]
            rows, cols = slice(h * uq, (h + 1) * uq), slice(g * DV, (g + 1) * DV)
            gate = _silu(z_ref[0, rows, cols].astype(_F32))
            o_ref[0, rows, cols] = ((acc[:DV] / acc[DV:DV + 1]).T * gate).astype(o_ref.dtype)


def _attn_call(qt, k, vt, z):
    b, _, _, s = qt.shape
    vrows, tv = vt.shape[-2:]
    t = ATTN_Q
    g = ATTN_HEADS
    nq = g * (t // ATTN_SUB)
    return pl.pallas_call(
        _attn_kernel,
        out_shape=jax.ShapeDtypeStruct((b, s, WIDTH), _BF16),
        grid=(b, HEADS // g, s // t),
        in_specs=[
            pl.BlockSpec((1, g, DQK, t), lambda bi, h, i: (bi, h, 0, i)),
            pl.BlockSpec((1, g, s, DQK), lambda bi, h, i: (bi, h, 0, 0)),
            pl.BlockSpec((1, g, s // tv, vrows, tv), lambda bi, h, i: (bi, h, 0, 0, 0)),
            pl.BlockSpec((1, t, g * DV), lambda bi, h, i: (bi, i, h)),
        ],
        out_specs=pl.BlockSpec((1, t, g * DV), lambda bi, h, i: (bi, i, h)),
        scratch_shapes=[
            pltpu.VMEM((nq, tv, ATTN_SUB), _F32),
            pltpu.VMEM((nq, tv, ATTN_SUB), _F32),
            pltpu.VMEM((nq, 1, ATTN_SUB), _F32),
            pltpu.VMEM((nq, 1, ATTN_SUB), _F32),
            pltpu.VMEM((nq, 1, ATTN_SUB), _F32),
            pltpu.VMEM((nq, vrows, ATTN_SUB), _F32),
        ],
        compiler_params=pltpu.CompilerParams(
            dimension_semantics=("arbitrary", "arbitrary", "arbitrary"), vmem_limit_bytes=VMEM_LIMIT),
        name="mla_attention",
    )(qt, k, vt, z)


def _cumsum_rows(tril_bf16, x):
    lane = lax.broadcasted_iota(jnp.int32, x.shape, 1)
    hi = x.astype(_BF16).astype(_F32)
    r1 = x - hi
    mid = r1.astype(_BF16).astype(_F32)
    low = r1 - mid
    packed = jnp.where(lane < 8, hi, jnp.where(lane < 16, pltpu.roll(mid, 8, 1), pltpu.roll(low, 16, 1)))
    res = jnp.dot(tril_bf16, packed.astype(_BF16), preferred_element_type=_F32)
    return res + pltpu.roll(res, LANES - 8, 1) + pltpu.roll(res, LANES - 16, 1)


def _gdn_kernel(q_ref, k_ref, v_ref, gb_ref, o_ref, state_ref, gc_ref, n_ref, t_ref, qk_ref, rhs_ref,
                qe_ref, kd_ref, sol_ref, ou_ref, qw_ref, su_ref, sw_ref):
    c = GDN_CHUNK
    n_chunks = q_ref.shape[1] // c
    n_chains = n_chunks * HEADS

    @pl.when(pl.program_id(1) == 0)
    def _():
        state_ref[...] = jnp.zeros_like(state_ref)

    row = lax.broadcasted_iota(jnp.int32, (c, c), 0)
    col = lax.broadcasted_iota(jnp.int32, (c, c), 1)
    incl = col <= row
    xor = row ^ col
    eye = (xor == 0).astype(_F32)
    tril = incl.astype(_BF16)

    for ci in range(n_chunks):
        rows = slice(ci * c, (ci + 1) * c)
        gb = gb_ref[0, rows, :]
        gcum = _cumsum_rows(tril, gb)
        gc_ref[ci] = gcum
        gcum_t = gcum.T
        for hd in range(HEADS):
            ch = ci * HEADS + hd
            lo = hd * DV
            q = q_ref[0, rows, lo:lo + DV].astype(_F32)
            k = k_ref[0, rows, lo:lo + DV].astype(_F32)
            v = v_ref[0, rows, lo:lo + DV].astype(_F32)
            beta = jnp.broadcast_to(gb[:, HEADS + hd:HEADS + hd + 1], (c, DV))
            g_col = jnp.broadcast_to(gcum[:, hd:hd + 1], (c, DV))
            g_row = gcum_t[hd:hd + 1, :]
            g_last = gcum[c - 1:c, hd:hd + 1]
            decay = jnp.exp(jnp.where(incl, jnp.concatenate([g_col] * (c // DV), axis=1) - g_row, _NEG))
            e_col = jnp.exp(g_col)
            kb = k * beta
            n_neg = _dot_nt(-kb, k) * decay
            n_ref[ch] = n_neg
            t_ref[ch] = jnp.where(xor == 1, n_neg, eye)
            qk_ref[ch] = (_dot_nt(q, k) * decay).astype(qk_ref.dtype)
            rhs_ref[ch] = jnp.concatenate([v * beta, kb * e_col], axis=1).astype(rhs_ref.dtype)
            qe_ref[ch] = (q * e_col).astype(qe_ref.dtype)
            kd_ref[ch] = (k * jnp.exp(g_last - g_col)).astype(kd_ref.dtype)

    m = 2
    while m < c:
        blk = lax.broadcasted_iota(jnp.int32, (c, c), 0) ^ lax.broadcasted_iota(jnp.int32, (c, c), 1)
        mask = (blk >= m) & (blk < 2 * m)
        size = m if m % 8 == 0 else c
        odd = [slice(b * m, (b + 1) * m) for b in range(1, c // m, 2)] if m % 8 == 0 else [slice(0, c)]

        def rows_of(ch, odd=odd):
            return jnp.concatenate([t_ref[ch, r, :] for r in odd], axis=0)

        left = []
        for ch in range(n_chains):
            left.append(_dot(rows_of(ch), jnp.where(mask, n_ref[ch], 0.0)))
        for ch in range(n_chains):
            upd = rows_of(ch) + _dot(left[ch], t_ref[ch])
            for j, r in enumerate(odd):
                t_ref[ch, r, :] = upd[j * size:(j + 1) * size]
        m *= 2

    for ch in range(n_chains):
        sol_ref[ch] = _dot(t_ref[ch], rhs_ref[ch]).astype(sol_ref.dtype)
    for ch in range(n_chains):
        qk_sol = _dot(qk_ref[ch], sol_ref[ch])
        ou_ref[ch] = qk_sol[:, :DV]
        qw_ref[ch] = (qe_ref[ch].astype(_F32) - qk_sol[:, DV:]).astype(qw_ref.dtype)
    for ch in range(n_chains):
        kd_sol = _dot_tn(kd_ref[ch], sol_ref[ch])
        su_ref[ch] = kd_sol[:, :DV]
        sw_ref[ch] = kd_sol[:, DV:].astype(sw_ref.dtype)
    for ci in range(n_chunks):
        rows = slice(ci * c, (ci + 1) * c)
        for hd in range(HEADS):
            ch = ci * HEADS + hd
            lo = hd * DV
            g_last = gc_ref[ci, c - 1:c, hd:hd + 1]
            state = state_ref[hd]
            o_ref[0, rows, lo:lo + DV] = (_dot(qw_ref[ch], state) + ou_ref[ch]).astype(o_ref.dtype)
            state_ref[hd] = state * jnp.exp(g_last) + su_ref[ch] - _dot(sw_ref[ch], state)


def _gdn_call(gq, gk, gv, gb):
    b, s, _ = gq.shape
    t = GDN_TILE
    n_chains = (t // GDN_CHUNK) * HEADS
    tok = lambda w: pl.BlockSpec((1, t, w), lambda i, j: (i, j, 0))
    return pl.pallas_call(
        _gdn_kernel,
        out_shape=jax.ShapeDtypeStruct((b, s, WIDTH), _BF16),
        grid=(b, s // t),
        in_specs=[tok(WIDTH), tok(WIDTH), tok(WIDTH), tok(LANES)],
        out_specs=tok(WIDTH),
        scratch_shapes=[
            pltpu.VMEM((HEADS, DV, DV), _F32),
            pltpu.VMEM((t // GDN_CHUNK, GDN_CHUNK, LANES), _F32),
            pltpu.VMEM((n_chains, GDN_CHUNK, GDN_CHUNK), _F32),
            pltpu.VMEM((n_chains, GDN_CHUNK, GDN_CHUNK), _F32),
            pltpu.VMEM((n_chains, GDN_CHUNK, GDN_CHUNK), _BF16),
            pltpu.VMEM((n_chains, GDN_CHUNK, 2 * DV), _BF16),
            pltpu.VMEM((n_chains, GDN_CHUNK, DV), _BF16),
            pltpu.VMEM((n_chains, GDN_CHUNK, DV), _BF16),
            pltpu.VMEM((n_chains, GDN_CHUNK, 2 * DV), _BF16),
            pltpu.VMEM((n_chains, GDN_CHUNK, DV), _F32),
            pltpu.VMEM((n_chains, GDN_CHUNK, DV), _BF16),
            pltpu.VMEM((n_chains, DV, DV), _F32),
            pltpu.VMEM((n_chains, DV, DV), _BF16),
        ],
        compiler_params=pltpu.CompilerParams(dimension_semantics=("arbitrary", "arbitrary"),
                                             vmem_limit_bytes=VMEM_LIMIT),
        name="gated_delta_rule",
    )(gq, gk, gv, gb)


def _out_kernel(x_ref, mod_ref, om_ref, og_ref, zg_ref, onw_ref, wout_ref, postw_ref, o_ref):
    d = x_ref.shape[-1]
    tm = x_ref.shape[1]
    w_post = mod_ref[0][:, 2 * d:3 * d] * postw_ref[...]
    for r in range(0, tm, OUT_ROWS):
        rows = slice(r, r + OUT_ROWS)
        og = og_ref[0, rows, :].astype(_F32)
        zg = _silu(zg_ref[0, rows, :].astype(_F32))
        y_gdn = jnp.concatenate([_rms(og[:, hd * DV:(hd + 1) * DV], onw_ref[...]) * zg[:, hd * DV:(hd + 1) * DV]
                                 for hd in range(HEADS)], axis=1)
        y = (jnp.dot(om_ref[0, rows, :], wout_ref[0:WIDTH, :], preferred_element_type=_F32)
             + _dot(y_gdn, wout_ref[WIDTH:2 * WIDTH, :]))
        y_n = y * lax.rsqrt(jnp.mean(y * y, axis=-1, keepdims=True) + EPS)
        o_ref[0, rows, :] = x_ref[0, rows, :] + y_n * w_post


def _out_call(layer, x, mod, o_mla, o_gdn, z_gdn, onw, wout, postw):
    b, s, d = x.shape
    tm = min(OUT_TILE, s)
    const = lambda shape: pl.BlockSpec((None,) + shape, lambda i, j: (layer,) + (0,) * len(shape))
    tok = lambda w: pl.BlockSpec((1, tm, w), lambda i, j: (i, j, 0))
    return pl.pallas_call(
        _out_kernel,
        out_shape=jax.ShapeDtypeStruct((b, s, d), _F32),
        grid=(b, s // tm),
        in_specs=[
            tok(d),
            pl.BlockSpec((None, 1, 1, 3 * d), lambda i, j: (layer, i, 0, 0)),
            tok(WIDTH), tok(WIDTH), tok(WIDTH),
            const((1, DV)),
            const((2 * WIDTH, d)),
            const((1, d)),
        ],
        out_specs=tok(d),
        compiler_params=pltpu.CompilerParams(dimension_semantics=("arbitrary", "arbitrary"),
                                             vmem_limit_bytes=VMEM_LIMIT),
        name="out_proj",
    )(x, mod, o_mla, o_gdn, z_gdn, onw, wout, postw)


def _rot_cols(w):
    half = w.shape[-1] // 2
    return jnp.concatenate([-w[..., half:], w[..., :half]], axis=-1)


def _pad_cols(w, n):
    return jnp.pad(w, [(0, 0)] * (w.ndim - 1) + [(0, n - w.shape[-1])])


def _prep_w_in(w):
    w = w.astype(_BF16)
    o_kv = Q_RANK
    o_kpe = o_kv + KV_RANK
    o_zm = o_kpe + ROPE
    o_qkv = o_zm + WIDTH
    o_a = o_qkv + 3 * WIDTH
    o_zg = o_a + 2 * HEADS
    k_pe = w[..., o_kpe:o_zm]
    return jnp.concatenate([
        w[..., :o_kpe], k_pe, _rot_cols(k_pe), w[..., o_zm:o_qkv], w[..., o_qkv:o_a],
        _pad_cols(w[..., o_a:o_zg], LANES), w[..., o_zg:],
    ], axis=-1)


def _prep_q_up(w):
    depth = w.shape[0]
    per = w.astype(_BF16).reshape(depth, Q_RANK, HEADS, DQK)
    heads = [jnp.concatenate([per[:, :, h, :NOPE], per[:, :, h, NOPE:], _rot_cols(per[:, :, h, NOPE:])], axis=-1)
             for h in range(HEADS)]
    return jnp.concatenate(heads, axis=-1)


def _prep_kv_up(w):
    return w.astype(_BF16)


def _lane_rows(vals):
    return _pad_cols(vals.astype(_F32), LANES)[:, None, :]


def kernel(x, c, positions, w_mod, b_mod, pre_norm_w, post_norm_w, w_in, mla_q_norm_w, mla_q_up,
           mla_kv_norm_w, mla_kv_up, gdn_conv_w, gdn_a_log, gdn_dt_bias, gdn_o_norm_w, w_out):
    depth = w_mod.shape[0]
    b, s, d = x.shape
    assert s % TOKEN_TILE == 0 and s % GDN_TILE == 0 and s % ATTN_Q == 0 and s % OUT_TILE == 0
    assert TOKEN_TILE % ATTN_SUB == 0 and ATTN_Q % TOKEN_TILE == 0 and (ATTN_Q // ATTN_SUB) % 2 == 0
    mod = _modulation(c, w_mod, b_mod).reshape(depth, b, 1, 3 * d)
    tab = _rope_table(positions)
    in_params = (pre_norm_w[:, None, :], _prep_w_in(w_in), mla_q_norm_w[:, None, :], _prep_q_up(mla_q_up),
                 mla_kv_norm_w[:, None, :], _prep_kv_up(mla_kv_up))
    gdn_params = (gdn_conv_w, _lane_rows(gdn_a_log), _lane_rows(gdn_dt_bias))
    out_params = (gdn_o_norm_w[:, None, :], w_out.astype(_BF16), post_norm_w[:, None, :])
    for l in range(depth):
        q, k, v, z_mla, gq, gk, gv, gb, z_gdn = _in_call(l, x, mod, *in_params, tab, *gdn_params)
        o_mla = _attn_call(q, k, v, z_mla)
        o_gdn = _gdn_call(gq, gk, gv, gb)
        x = _out_call(l, x, mod, o_mla, o_gdn, z_gdn, *out_params)
    return x
```
